```python
import jax, jax.numpy as jnp
from jax import lax
import numpy as np

D_MODEL = 1024
BATCH = 2
SEQ = 8192
DEPTH = 1
DEC_BATCH = 8
DEC_SEQ = 32
PAST_LEN = 2048

CHUNK = 64
MIX_WIDTH = D_MODEL
CONV_DIM = MIX_WIDTH // 2
RWKV_DIM = MIX_WIDTH - CONV_DIM
HEAD_SIZE = 64
N_HEADS = RWKV_DIM // HEAD_SIZE
CONV_K = 3
DECAY_LORA = 64
AAA_LORA = 64
GATE_LORA = 128
RWKV_PROJ = 3 * RWKV_DIM + DECAY_LORA + AAA_LORA + GATE_LORA
PROJ_DIM = 3 * CONV_DIM + RWKV_PROJ
D_FF = 4 * D_MODEL
NORM_EPS = 1e-6
GN_EPS = 64e-5
DECAY_SCALE = float(np.exp(-0.5))

kernel_name = "hymba_conv_rwkv7_stream_step"


def _rmsnorm(x, g):
    xf = x.astype(jnp.float32)
    y = xf * lax.rsqrt(jnp.mean(xf * xf, axis=-1, keepdims=True) + NORM_EPS)
    return (y * g.astype(jnp.float32)).astype(x.dtype)


def _causal_conv(u, buf, w):
    T = u.shape[1]
    up = jnp.concatenate([buf.astype(u.dtype), u], axis=1)
    y = up[:, 0:T] * w[0]
    for j in range(1, CONV_K):
        y = y + up[:, j:j + T] * w[j]
    return y, up[:, T:]


def _wkv7(r, w, k, v, kk, a, s0):
    def step(S, inp):
        r_t, w_t, k_t, v_t, kk_t, a_t = inp
        sa = jnp.einsum('bhvk,bhk->bhv', S, -kk_t)
        S = (S * w_t[:, :, None, :]
             + sa[..., None] * (kk_t * a_t)[:, :, None, :]
             + v_t[..., None] * k_t[:, :, None, :])
        return S, jnp.einsum('bhvk,bhk->bhv', S, r_t)
    xs = tuple(jnp.swapaxes(t, 0, 1) for t in (r, w, k, v, kk, a))
    S, o = lax.scan(step, s0, xs)
    return jnp.swapaxes(o, 0, 1), S


def _mixer(h, conv_buf, shift_prev, wkv_state, w_in, conv_w, shift_mu, w_decay2, decay_w0,
           w_a2, a0, w_g2, k_k, k_a, r_k, gn_gain, gn_bias, w_out):
    B, T, _ = h.shape
    z = h @ w_in
    zb, zc, zh, zr = jnp.split(z, [CONV_DIM, 2 * CONV_DIM, 3 * CONV_DIM], axis=-1)

    yc, new_conv = _causal_conv(zc * zh, conv_buf, conv_w)
    y_conv = zb * yc

    zprev = jnp.concatenate([shift_prev[:, None].astype(zr.dtype), zr[:, :-1]], axis=1)
    zs = (zr + shift_mu * (zprev - zr)).astype(jnp.float32)
    new_shift = zr[:, -1]
    r, k, v, lw, la, lg = jnp.split(
        zs, [RWKV_DIM, 2 * RWKV_DIM, 3 * RWKV_DIM, 3 * RWKV_DIM + DECAY_LORA,
             3 * RWKV_DIM + DECAY_LORA + AAA_LORA], axis=-1)
    f32 = lambda t: t.astype(jnp.float32)
    logit_w = f32(decay_w0) + jnp.tanh(lw) @ f32(w_decay2)
    w = jnp.exp(-DECAY_SCALE * jax.nn.sigmoid(logit_w))
    a = jax.nn.sigmoid(f32(a0) + la @ f32(w_a2))
    g = jax.nn.sigmoid(lg) @ f32(w_g2)

    hs = lambda t: t.reshape(B, T, N_HEADS, HEAD_SIZE)
    hp = lambda t: f32(t).reshape(N_HEADS, HEAD_SIZE)
    r, k, v, w, a = hs(r), hs(k), hs(v), hs(w), hs(a)
    kk = k * hp(k_k)
    kk = kk / jnp.maximum(jnp.sqrt(jnp.sum(kk * kk, axis=-1, keepdims=True)), 1e-12)
    k = k * (1.0 + (a - 1.0) * hp(k_a))
    o, S = _wkv7(r, w, k, v, kk, a, wkv_state.astype(jnp.float32))

    mu = jnp.mean(o, axis=-1, keepdims=True)
    var = jnp.mean(jnp.square(o - mu), axis=-1, keepdims=True)
    o = (o - mu) * lax.rsqrt(var + GN_EPS)
    o = o * hp(gn_gain) + hp(gn_bias)
    o = o + jnp.sum(r * k * f32(r_k), axis=-1, keepdims=True) * v
    y_rwkv = (o.reshape(B, T, RWKV_DIM) * g).astype(h.dtype)

    y = jnp.concatenate([y_conv.astype(h.dtype), y_rwkv], axis=-1) @ w_out
    return y, new_conv, new_shift, S.astype(h.dtype)


def _layer(x, conv_buf, shift_prev, wkv_state, p):
    (norm_mix_pre, norm_mix_post, norm_ffn_pre, norm_ffn_post, w_in, conv_w, shift_mu,
     w_decay2, decay_w0, w_a2, a0, w_g2, k_k, k_a, r_k, gn_gain, gn_bias, w_out,
     w_ff1, w_ff2) = p
    h = _rmsnorm(x, norm_mix_pre)
    m, new_conv, new_shift, new_wkv = _mixer(
        h, conv_buf, shift_prev, wkv_state, w_in, conv_w, shift_mu, w_decay2, decay_w0,
        w_a2, a0, w_g2, k_k, k_a, r_k, gn_gain, gn_bias, w_out)
    x = x + _rmsnorm(m, norm_mix_post)
    h2 = _rmsnorm(x, norm_ffn_pre)
    f = jnp.square(jax.nn.relu(h2 @ w_ff1)) @ w_ff2
    x = x + _rmsnorm(f, norm_ffn_post)
    return x, new_conv, new_shift, new_wkv


def setup_inputs(seed: int = 0) -> dict:
    key = jax.random.key(seed)
    ks = jax.random.split(key, 32)
    nrm = lambda k, shape, s: jax.random.normal(k, shape, jnp.float32) * s
    L = DEPTH
    return {
        "x_prompt": nrm(ks[0], (BATCH, SEQ, D_MODEL), 1.0),
        "x_sample": nrm(ks[1], (DEC_BATCH, DEC_SEQ, D_MODEL), 1.0),
        "state_conv": nrm(ks[2], (L, DEC_BATCH, CONV_K - 1, CONV_DIM), 1.0),
        "state_shift": nrm(ks[3], (L, DEC_BATCH, RWKV_PROJ), 1.0),
        "state_wkv": nrm(ks[4], (L, DEC_BATCH, N_HEADS, HEAD_SIZE, HEAD_SIZE), 0.5),
        "norm_mix_pre": 1.0 + nrm(ks[5], (L, D_MODEL), 0.05),
        "norm_mix_post": 1.0 + nrm(ks[6], (L, D_MODEL), 0.05),
        "norm_ffn_pre": 1.0 + nrm(ks[7], (L, D_MODEL), 0.05),
        "norm_ffn_post": 1.0 + nrm(ks[8], (L, D_MODEL), 0.05),
        "w_in": nrm(ks[9], (L, D_MODEL, PROJ_DIM), D_MODEL ** -0.5),
        "conv_w": nrm(ks[10], (L, CONV_K, CONV_DIM), CONV_K ** -0.5),
        "shift_mu": jax.random.uniform(ks[11], (L, RWKV_PROJ), jnp.float32),
        "w_decay2": nrm(ks[12], (L, DECAY_LORA, RWKV_DIM), DECAY_LORA ** -0.5),
        "decay_w0": nrm(ks[13], (L, RWKV_DIM), 0.5),
        "w_a2": nrm(ks[14], (L, AAA_LORA, RWKV_DIM), AAA_LORA ** -0.5),
        "a0": nrm(ks[15], (L, RWKV_DIM), 0.1),
        "w_g2": nrm(ks[16], (L, GATE_LORA, RWKV_DIM), GATE_LORA ** -0.5),
        "k_k": 0.85 + nrm(ks[17], (L, RWKV_DIM), 0.05),
        "k_a": 1.0 + nrm(ks[18], (L, RWKV_DIM), 0.05),
        "r_k": nrm(ks[19], (L, N_HEADS, HEAD_SIZE), 0.1),
        "gn_gain": 1.0 + nrm(ks[20], (L, RWKV_DIM), 0.05),
        "gn_bias": nrm(ks[21], (L, RWKV_DIM), 0.02),
        "w_out": nrm(ks[22], (L, MIX_WIDTH, D_MODEL), MIX_WIDTH ** -0.5),
        "w_ff1": nrm(ks[23], (L, D_MODEL, D_FF), D_MODEL ** -0.5),
        "w_ff2": nrm(ks[24], (L, D_FF, D_MODEL), D_FF ** -0.5),
    }


def reference(x_prompt, x_sample, state_conv, state_shift, state_wkv,
              norm_mix_pre, norm_mix_post, norm_ffn_pre, norm_ffn_post, w_in, conv_w,
              shift_mu, w_decay2, decay_w0, w_a2, a0, w_g2, k_k, k_a, r_k, gn_gain,
              gn_bias, w_out, w_ff1, w_ff2):
    Bp = x_prompt.shape[0]
    dt = x_prompt.dtype
    xp, xs = x_prompt, x_sample
    conv_p, shift_p, wkv_p, conv_s, shift_s, wkv_s = [], [], [], [], [], []
    for l in range(DEPTH):
        p = (norm_mix_pre[l], norm_mix_post[l], norm_ffn_pre[l], norm_ffn_post[l], w_in[l],
             conv_w[l], shift_mu[l], w_decay2[l], decay_w0[l], w_a2[l], a0[l], w_g2[l],
             k_k[l], k_a[l], r_k[l], gn_gain[l], gn_bias[l], w_out[l], w_ff1[l], w_ff2[l])
        xp, c_p, s_p, w_p = _layer(
            xp,
            jnp.zeros((Bp, CONV_K - 1, CONV_DIM), dt),
            jnp.zeros((Bp, RWKV_PROJ), dt),
            jnp.zeros((Bp, N_HEADS, HEAD_SIZE, HEAD_SIZE), dt),
            p)
        xs, c_s, s_s, w_s = _layer(xs, state_conv[l], state_shift[l], state_wkv[l], p)
        conv_p.append(c_p); shift_p.append(s_p); wkv_p.append(w_p)
        conv_s.append(c_s); shift_s.append(s_s); wkv_s.append(w_s)
    conv_prompt = jnp.stack(conv_p)
    shift_prompt = jnp.stack(shift_p)
    wkv_prompt = jnp.stack(wkv_p)
    conv_sample = jnp.stack(conv_s)
    shift_sample = jnp.stack(shift_s)
    wkv_sample = jnp.stack(wkv_s)
    return (xp, xs, conv_prompt, shift_prompt, wkv_prompt, conv_sample, shift_sample, wkv_sample)
```

```python
import functools

import jax
import jax.numpy as jnp
import numpy as np
from jax import lax
from jax.experimental import pallas as pl
from jax.experimental.pallas import tpu as pltpu

D_MODEL = 1024
CONV_DIM = 512
RWKV_DIM = 512
HEAD_SIZE = 64
N_HEADS = 8
CONV_K = 3
DECAY_LORA = 64
AAA_LORA = 64
GATE_LORA = 128
RWKV_PROJ = 3 * RWKV_DIM + DECAY_LORA + AAA_LORA + GATE_LORA
PROJ_DIM = 3 * CONV_DIM + RWKV_PROJ
D_FF = 4 * D_MODEL
NORM_EPS = 1e-6
GN_EPS = 64e-5
DECAY_SCALE = float(np.exp(-0.5))

SUBLANES = 8
INV_BASE = 16
VMEM_LIMIT = 56 * 1024 * 1024

F32 = jnp.float32
BF16 = jnp.bfloat16

_NN = (((1,), (0,)), ((), ()))
_NT = (((1,), (1,)), ((), ()))
_TN = (((0,), (0,)), ((), ()))


def _dg(a, b, dims=_NN):
    return lax.dot_general(a, b, dims, preferred_element_type=F32)


def _split(a):
    hi = a.astype(BF16)
    lo = (a - hi.astype(F32)).astype(BF16)
    return hi, lo


def _dot3(a, b, dims=_NN):
    ah, al = _split(a)
    bh, bl = _split(b)
    return _dg(ah, bh, dims) + _dg(ah, bl, dims) + _dg(al, bh, dims)


def _dot_exact_rhs(a, b_bf16):
    a1 = a.astype(BF16)
    r1 = a - a1.astype(F32)
    a2 = r1.astype(BF16)
    a3 = (r1 - a2.astype(F32)).astype(BF16)
    return _dg(a1, b_bf16) + _dg(a2, b_bf16) + _dg(a3, b_bf16)


def _rmsnorm(x, g):
    return x * lax.rsqrt(jnp.mean(x * x, axis=-1, keepdims=True) + NORM_EPS) * g


def _proj_kernel(x_ref, cbuf_ref, sprev_ref, gpre_ref, win_ref, convw_ref, mu_ref,
                 wd_ref, w0_ref, wa_ref, a0_ref, wg_ref, kk_ref, ka_ref, rk_ref, ones_ref,
                 yconv_ref, r_ref, k_ref, v_ref, kkn_ref, b_ref, lw_ref, g_ref, bg_ref,
                 nconv_ref, nshift_ref, ubuf, zbuf, *, nb, tt):
    m = nb * tt
    h0 = SUBLANES

    @pl.when(pl.program_id(1) == 0)
    def _():
        ubuf[:, h0 - 2:h0, :] = cbuf_ref[...]
        zbuf[:, h0 - 1:h0, :] = sprev_ref[...]

    x = x_ref[...].reshape(m, D_MODEL)
    h = _rmsnorm(x, gpre_ref[...]).astype(BF16)
    z = _dg(h, win_ref[...])
    zb = z[:, :CONV_DIM]
    zc = z[:, CONV_DIM:2 * CONV_DIM]
    zh = z[:, 2 * CONV_DIM:3 * CONV_DIM]
    zr = z[:, 3 * CONV_DIM:]

    ubuf[:, h0:, :] = (zc * zh).reshape(nb, tt, CONV_DIM)
    cw = convw_ref[...]
    yc = (ubuf[:, h0 - 2:h0 - 2 + tt, :] * cw[0:1, :]
          + ubuf[:, h0 - 1:h0 - 1 + tt, :] * cw[1:2, :]
          + ubuf[:, h0:h0 + tt, :] * cw[2:3, :])
    yconv_ref[...] = (zb.reshape(nb, tt, CONV_DIM) * yc)
    last_u = ubuf[:, h0 + tt - 2:h0 + tt, :]
    nconv_ref[...] = last_u
    ubuf[:, h0 - 2:h0, :] = last_u

    zbuf[:, h0:, :] = zr.reshape(nb, tt, RWKV_PROJ)
    zprev = zbuf[:, h0 - 1:h0 - 1 + tt, :].reshape(m, RWKV_PROJ)
    last_z = zbuf[:, h0 + tt - 1:h0 + tt, :]
    nshift_ref[...] = last_z
    zbuf[:, h0 - 1:h0, :] = last_z
    zs = zr + mu_ref[...] * (zprev - zr)

    r = zs[:, :RWKV_DIM]
    k = zs[:, RWKV_DIM:2 * RWKV_DIM]
    v = zs[:, 2 * RWKV_DIM:3 * RWKV_DIM]
    lwa = zs[:, 3 * RWKV_DIM:3 * RWKV_DIM + DECAY_LORA + AAA_LORA]
    lg = zs[:, 3 * RWKV_DIM + DECAY_LORA + AAA_LORA:]

    logit_w = w0_ref[...] + _dot3(jnp.tanh(lwa), wd_ref[...])
    logw = -DECAY_SCALE * jax.nn.sigmoid(logit_w)
    a = jax.nn.sigmoid(a0_ref[...] + _dot3(lwa, wa_ref[...]))
    g = _dot3(jax.nn.sigmoid(lg), wg_ref[...])

    ones_bd = ones_ref[...]
    kk = k * kk_ref[...]
    ss = _dot_exact_rhs(kk * kk, ones_bd)
    kk = kk / jnp.maximum(jnp.sqrt(ss), 1e-12)
    kp = k * (1.0 + (a - 1.0) * ka_ref[...])
    bonus = _dot_exact_rhs(r * kp * rk_ref[...], ones_bd) * v

    shp = (nb, tt, RWKV_DIM)
    r_ref[...] = r.reshape(shp)
    k_ref[...] = kp.reshape(shp)
    v_ref[...] = v.reshape(shp)
    kkn_ref[...] = kk.reshape(shp)
    b_ref[...] = (kk * a).reshape(shp)
    lw_ref[...] = logw.reshape(shp)
    g_ref[...] = g.reshape(shp)
    bg_ref[...] = (bonus * g).reshape(shp)


def _proj_call(x, conv_buf, shift_prev, params, *, nb, tt):
    B, T, _ = x.shape
    grid = (B // nb, T // tt)
    tok = lambda w: pl.BlockSpec((nb, tt, w), lambda b, t: (b, t, 0))
    full = lambda arr: pl.BlockSpec(arr.shape, lambda b, t: (0,) * arr.ndim)
    in_specs = [tok(D_MODEL),
                pl.BlockSpec((nb, CONV_K - 1, CONV_DIM), lambda b, t: (b, 0, 0)),
                pl.BlockSpec((nb, 1, RWKV_PROJ), lambda b, t: (b, 0, 0))]
    in_specs += [full(p) for p in params]
    seq = lambda w: jax.ShapeDtypeStruct((B, T, w), F32)
    out_shape = [seq(CONV_DIM)] + [seq(RWKV_DIM)] * 8 + [
        jax.ShapeDtypeStruct((B, CONV_K - 1, CONV_DIM), F32),
        jax.ShapeDtypeStruct((B, 1, RWKV_PROJ), F32)]
    out_specs = [tok(CONV_DIM)] + [tok(RWKV_DIM)] * 8 + [
        pl.BlockSpec((nb, CONV_K - 1, CONV_DIM), lambda b, t: (b, 0, 0)),
        pl.BlockSpec((nb, 1, RWKV_PROJ), lambda b, t: (b, 0, 0))]
    return pl.pallas_call(
        functools.partial(_proj_kernel, nb=nb, tt=tt),
        grid=grid, in_specs=in_specs, out_specs=out_specs, out_shape=out_shape,
        scratch_shapes=[pltpu.VMEM((nb, tt + SUBLANES, CONV_DIM), F32),
                        pltpu.VMEM((nb, tt + SUBLANES, RWKV_PROJ), F32)],
        compiler_params=pltpu.CompilerParams(
            dimension_semantics=("arbitrary", "arbitrary"), vmem_limit_bytes=VMEM_LIMIT),
        name="proj",
    )(x, conv_buf, shift_prev, *params)


def _unit_lower_inverse(low, c):
    row = lax.broadcasted_iota(jnp.int32, (c, c), 0)
    col = lax.broadcasted_iota(jnp.int32, (c, c), 1)
    eye = (row == col).astype(F32)
    base = min(INV_BASE, c)
    same = (row // base) == (col // base)
    p = jnp.where(same, low, 0.0)
    inv = eye - p
    span = 2
    while span < base:
        p = _dot3(p, p)
        inv = inv + _dot3(inv, p)
        span *= 2
    blk = base
    while blk < c:
        off = jnp.where(((row // (2 * blk)) == (col // (2 * blk))) & ((row // blk) != (col // blk)),
                        low, 0.0)
        inv = inv - _dot3(inv, _dot3(off, inv))
        blk *= 2
    return inv


def _wkv_kernel(r_ref, k_ref, v_ref, kk_ref, b_ref, lw_ref, g_ref, bg_ref, s0_ref,
                gain_ref, bias_ref, y_ref, sout_ref, s_scr, *, c):
    @pl.when(pl.program_id(1) == 0)
    def _():
        s_scr[...] = s0_ref[...]

    row = lax.broadcasted_iota(jnp.int32, (c, c), 0)
    col = lax.broadcasted_iota(jnp.int32, (c, c), 1)
    strict = row > col
    incl = row >= col

    logw = lw_ref[...]
    cum = _dot_exact_rhs_lhs(incl.astype(BF16), logw)
    cum_end = cum[c - 1:c, :]
    r = r_ref[...]
    k = k_ref[...]
    b = b_ref[...]
    e_neg = jnp.exp(-cum)
    rt = r * jnp.exp(cum)
    at = kk_ref[...] * jnp.exp(cum - logw)
    bt = b * e_neg
    kt = k * e_neg
    d_end = jnp.exp(cum_end - cum)
    bh = b * d_end
    kh = k * d_end
    p_end = jnp.exp(cum_end)
    v = v_ref[...]
    gain = gain_ref[...]
    bias = bias_ref[...]

    for hd in range(N_HEADS):
        sl = slice(hd * HEAD_SIZE, (hd + 1) * HEAD_SIZE)
        a_h, r_h, b_h, k_h, v_h = at[:, sl], rt[:, sl], bt[:, sl], kt[:, sl], v[:, sl]
        a_ab = _dot3(a_h, b_h, _NT)
        a_ak = _dot3(a_h, k_h, _NT)
        a_rb = _dot3(r_h, b_h, _NT)
        a_rk = _dot3(r_h, k_h, _NT)
        tinv = _unit_lower_inverse(jnp.where(strict, a_ab, 0.0), c)
        w_h = _dot3(tinv, a_h)
        u0 = -_dot3(tinv, _dot3(jnp.where(strict, a_ak, 0.0), v_h))

        s = s_scr[hd]
        u = u0 - _dot3(w_h, s, _NT)
        o = (_dot3(r_h, s, _NT) + _dot3(jnp.where(incl, a_rb, 0.0), u)
             + _dot3(jnp.where(incl, a_rk, 0.0), v_h))
        s_new = (s * p_end[:, sl] + _dot3(u, bh[:, sl], _TN) + _dot3(v_h, kh[:, sl], _TN))
        s_scr[hd] = s_new
        sout_ref[hd] = s_new

        mu = jnp.mean(o, axis=-1, keepdims=True)
        var = jnp.mean(jnp.square(o - mu), axis=-1, keepdims=True)
        on = (o - mu) * lax.rsqrt(var + GN_EPS)
        y_ref[:, sl] = (on * gain[:, sl] + bias[:, sl]) * g_ref[:, sl] + bg_ref[:, sl]


def _dot_exact_rhs_lhs(a_bf16, b):
    b1 = b.astype(BF16)
    r1 = b - b1.astype(F32)
    b2 = r1.astype(BF16)
    b3 = (r1 - b2.astype(F32)).astype(BF16)
    return _dg(a_bf16, b1) + _dg(a_bf16, b2) + _dg(a_bf16, b3)


def _wkv_call(seqs, s0, gain, bias, *, c):
    B, T, _ = seqs[0].shape
    tok = pl.BlockSpec((None, c, RWKV_DIM), lambda b, t: (b, t, 0))
    st = pl.BlockSpec((None, N_HEADS, HEAD_SIZE, HEAD_SIZE), lambda b, t: (b, 0, 0, 0))
    vec = pl.BlockSpec((1, RWKV_DIM), lambda b, t: (0, 0))
    return pl.pallas_call(
        functools.partial(_wkv_kernel, c=c),
        grid=(B, T // c),
        in_specs=[tok] * 8 + [st, vec, vec],
        out_specs=[tok, st],
        out_shape=[jax.ShapeDtypeStruct((B, T, RWKV_DIM), F32),
                   jax.ShapeDtypeStruct((B, N_HEADS, HEAD_SIZE, HEAD_SIZE), F32)],
        scratch_shapes=[pltpu.VMEM((N_HEADS, HEAD_SIZE, HEAD_SIZE), F32)],
        compiler_params=pltpu.CompilerParams(
            dimension_semantics=("arbitrary", "arbitrary"), vmem_limit_bytes=VMEM_LIMIT),
        name="wkv",
    )(*seqs, s0, gain, bias)


def _out_kernel(x_ref, yc_ref, yr_ref, wout_ref, gpost_ref, gfpre_ref, gfpost_ref,
                wff1_ref, wff2_ref, o_ref):
    wout = wout_ref[...]
    mix = (_dg(yc_ref[...].astype(BF16), wout[:CONV_DIM])
           + _dg(yr_ref[...].astype(BF16), wout[CONV_DIM:]))
    x1 = x_ref[...] + _rmsnorm(mix, gpost_ref[...])
    h2 = _rmsnorm(x1, gfpre_ref[...]).astype(BF16)
    f1 = _dg(h2, wff1_ref[...])
    f1 = jnp.square(jnp.maximum(f1, 0.0)).astype(BF16)
    f2 = _dg(f1, wff2_ref[...])
    o_ref[...] = x1 + _rmsnorm(f2, gfpost_ref[...])


def _out_call(x, yc, yr, params, *, tm):
    n = x.shape[0]
    tok = lambda w: pl.BlockSpec((tm, w), lambda i: (i, 0))
    full = lambda arr: pl.BlockSpec(arr.shape, lambda i: (0,) * arr.ndim)
    return pl.pallas_call(
        _out_kernel,
        grid=(n // tm,),
        in_specs=[tok(D_MODEL), tok(CONV_DIM), tok(RWKV_DIM)] + [full(p) for p in params],
        out_specs=tok(D_MODEL),
        out_shape=jax.ShapeDtypeStruct((n, D_MODEL), F32),
        compiler_params=pltpu.CompilerParams(
            dimension_semantics=("arbitrary",), vmem_limit_bytes=VMEM_LIMIT),
        name="outffn",
    )(x, yc, yr, *params)


def _layer(x, conv_buf, shift_prev, wkv_state, proj_params, gn, out_params, *, nb, tt, c, tm):
    B, T, _ = x.shape
    outs = _proj_call(x, conv_buf, shift_prev.reshape(B, 1, RWKV_PROJ), proj_params, nb=nb, tt=tt)
    yconv, seqs, new_conv, new_shift = outs[0], outs[1:9], outs[9], outs[10]
    yr, new_wkv = _wkv_call(seqs, wkv_state, gn[0], gn[1], c=c)
    y = _out_call(x.reshape(B * T, D_MODEL), yconv.reshape(B * T, CONV_DIM),
                  yr.reshape(B * T, RWKV_DIM), out_params, tm=tm)
    return y.reshape(B, T, D_MODEL), new_conv, new_shift.reshape(B, RWKV_PROJ), new_wkv


def kernel(x_prompt, x_sample, state_conv, state_shift, state_wkv, norm_mix_pre, norm_mix_post,
           norm_ffn_pre, norm_ffn_post, w_in, conv_w, shift_mu, w_decay2, decay_w0, w_a2, a0,
           w_g2, k_k, k_a, r_k, gn_gain, gn_bias, w_out, w_ff1, w_ff2):
    depth = w_in.shape[0]
    Bp = x_prompt.shape[0]
    xp, xs = x_prompt, x_sample
    hid = jnp.arange(RWKV_DIM) // HEAD_SIZE
    ones_bd = (hid[:, None] == hid[None, :]).astype(BF16)
    row = lambda t: t.reshape(1, -1).astype(F32)
    zpad = jnp.zeros((DECAY_LORA, RWKV_DIM), F32)
    res = [[] for _ in range(6)]
    for l in range(depth):
        proj_params = (
            row(norm_mix_pre[l]), w_in[l].astype(BF16), conv_w[l], row(shift_mu[l]),
            jnp.concatenate([w_decay2[l], zpad], axis=0), row(decay_w0[l]),
            jnp.concatenate([zpad, w_a2[l]], axis=0), row(a0[l]), w_g2[l],
            row(k_k[l]), row(k_a[l]), row(r_k[l]), ones_bd)
        gn = (row(gn_gain[l]), row(gn_bias[l]))
        out_params = (w_out[l].astype(BF16), row(norm_mix_post[l]), row(norm_ffn_pre[l]),
                      row(norm_ffn_post[l]), w_ff1[l].astype(BF16), w_ff2[l].astype(BF16))
        xp, c_p, s_p, w_p = _layer(
            xp, jnp.zeros((Bp, CONV_K - 1, CONV_DIM), F32), jnp.zeros((Bp, RWKV_PROJ), F32),
            jnp.zeros((Bp, N_HEADS, HEAD_SIZE, HEAD_SIZE), F32),
            proj_params, gn, out_params, nb=1, tt=256, c=64, tm=256)
        xs, c_s, s_s, w_s = _layer(
            xs, state_conv[l], state_shift[l], state_wkv[l],
            proj_params, gn, out_params, nb=xs.shape[0], tt=xs.shape[1], c=xs.shape[1],
            tm=xs.shape[0] * xs.shape[1])
        for lst, val in zip(res, (c_p, s_p, w_p, c_s, s_s, w_s)):
            lst.append(val)
    return (xp, xs) + tuple(jnp.stack(r) for r in res)
```

```python
import functools

import jax
import jax.numpy as jnp
import numpy as np
from jax import lax
from jax.experimental import pallas as pl
from jax.experimental.pallas import tpu as pltpu

D_MODEL = 1024
CONV_DIM = 512
RWKV_DIM = 512
HEAD_SIZE = 64
N_HEADS = 8
CONV_K = 3
DECAY_LORA = 64
AAA_LORA = 64
GATE_LORA = 128
RWKV_PROJ = 3 * RWKV_DIM + DECAY_LORA + AAA_LORA + GATE_LORA
PROJ_DIM = 3 * CONV_DIM + RWKV_PROJ
D_FF = 4 * D_MODEL
NORM_EPS = 1e-6
GN_EPS = 64e-5
DECAY_SCALE = float(np.exp(-0.5))

SUBLANES = 8
INV_BASE = 16
VMEM_LIMIT = 56 * 1024 * 1024

F32 = jnp.float32
BF16 = jnp.bfloat16

_NN = (((1,), (0,)), ((), ()))
_NT = (((1,), (1,)), ((), ()))
_TN = (((0,), (0,)), ((), ()))


def _dg(a, b, dims=_NN):
    return lax.dot_general(a, b, dims, preferred_element_type=F32)


def _split(a):
    hi = a.astype(BF16)
    lo = (a - hi.astype(F32)).astype(BF16)
    return hi, lo


def _dot3(a, b, dims=_NN):
    ah, al = _split(a)
    bh, bl = _split(b)
    return _dg(ah, bh, dims) + _dg(ah, bl, dims) + _dg(al, bh, dims)


def _dot_exact_rhs(a, b_bf16):
    a1 = a.astype(BF16)
    r1 = a - a1.astype(F32)
    a2 = r1.astype(BF16)
    a3 = (r1 - a2.astype(F32)).astype(BF16)
    return _dg(a1, b_bf16) + _dg(a2, b_bf16) + _dg(a3, b_bf16)


def _rmsnorm(x, g):
    return x * lax.rsqrt(jnp.mean(x * x, axis=-1, keepdims=True) + NORM_EPS) * g


def _proj_kernel(x_ref, cbuf_ref, sprev_ref, gpre_ref, win_ref, convw_ref, mu_ref,
                 wd_ref, w0_ref, wa_ref, a0_ref, wg_ref, kk_ref, ka_ref, rk_ref, ones_ref,
                 yconv_ref, r_ref, k_ref, v_ref, kkn_ref, b_ref, lw_ref, g_ref, bg_ref,
                 nconv_ref, nshift_ref, ubuf, zbuf, *, nb, tt):
    m = nb * tt
    h0 = SUBLANES

    @pl.when(pl.program_id(1) == 0)
    def _():
        ubuf[:, h0 - 2:h0, :] = cbuf_ref[...]
        zbuf[:, h0 - 1:h0, :] = sprev_ref[...]

    x = x_ref[...].reshape(m, D_MODEL)
    h = _rmsnorm(x, gpre_ref[...]).astype(BF16)
    z = _dg(h, win_ref[...])
    zb = z[:, :CONV_DIM]
    zc = z[:, CONV_DIM:2 * CONV_DIM]
    zh = z[:, 2 * CONV_DIM:3 * CONV_DIM]
    zr = z[:, 3 * CONV_DIM:]

    ubuf[:, h0:, :] = (zc * zh).reshape(nb, tt, CONV_DIM)
    cw = convw_ref[...]
    yc = (ubuf[:, h0 - 2:h0 - 2 + tt, :] * cw[0:1, :]
          + ubuf[:, h0 - 1:h0 - 1 + tt, :] * cw[1:2, :]
          + ubuf[:, h0:h0 + tt, :] * cw[2:3, :])
    yconv_ref[...] = (zb.reshape(nb, tt, CONV_DIM) * yc)
    last_u = ubuf[:, h0 + tt - 2:h0 + tt, :]
    nconv_ref[...] = last_u
    ubuf[:, h0 - 2:h0, :] = last_u

    zbuf[:, h0:, :] = zr.reshape(nb, tt, RWKV_PROJ)
    zprev = zbuf[:, h0 - 1:h0 - 1 + tt, :].reshape(m, RWKV_PROJ)
    last_z = zbuf[:, h0 + tt - 1:h0 + tt, :]
    nshift_ref[...] = last_z
    zbuf[:, h0 - 1:h0, :] = last_z
    zs = zr + mu_ref[...] * (zprev - zr)

    r = zs[:, :RWKV_DIM]
    k = zs[:, RWKV_DIM:2 * RWKV_DIM]
    v = zs[:, 2 * RWKV_DIM:3 * RWKV_DIM]
    lwa = zs[:, 3 * RWKV_DIM:3 * RWKV_DIM + DECAY_LORA + AAA_LORA]
    lg = zs[:, 3 * RWKV_DIM + DECAY_LORA + AAA_LORA:]

    logit_w = w0_ref[...] + _dot3(jnp.tanh(lwa), wd_ref[...])
    logw = -DECAY_SCALE * jax.nn.sigmoid(logit_w)
    a = jax.nn.sigmoid(a0_ref[...] + _dot3(lwa, wa_ref[...]))
    g = _dot3(jax.nn.sigmoid(lg), wg_ref[...])

    ones_bd = ones_ref[...]
    kk = k * kk_ref[...]
    ss = _dot_exact_rhs(kk * kk, ones_bd)
    kk = kk / jnp.maximum(jnp.sqrt(ss), 1e-12)
    kp = k * (1.0 + (a - 1.0) * ka_ref[...])
    bonus = _dot_exact_rhs(r * kp * rk_ref[...], ones_bd) * v

    shp = (nb, tt, RWKV_DIM)
    r_ref[...] = r.reshape(shp)
    k_ref[...] = kp.reshape(shp)
    v_ref[...] = v.reshape(shp)
    kkn_ref[...] = kk.reshape(shp)
    b_ref[...] = (kk * a).reshape(shp)
    lw_ref[...] = logw.reshape(shp)
    g_ref[...] = g.reshape(shp)
    bg_ref[...] = (bonus * g).reshape(shp)


def _proj_call(x, conv_buf, shift_prev, params, *, nb, tt):
    B, T, _ = x.shape
    grid = (B // nb, T // tt)
    tok = lambda w: pl.BlockSpec((nb, tt, w), lambda b, t: (b, t, 0))
    full = lambda arr: pl.BlockSpec(arr.shape, lambda b, t: (0,) * arr.ndim)
    in_specs = [tok(D_MODEL),
                pl.BlockSpec((nb, CONV_K - 1, CONV_DIM), lambda b, t: (b, 0, 0)),
                pl.BlockSpec((nb, 1, RWKV_PROJ), lambda b, t: (b, 0, 0))]
    in_specs += [full(p) for p in params]
    seq = lambda w: jax.ShapeDtypeStruct((B, T, w), F32)
    out_shape = [seq(CONV_DIM)] + [seq(RWKV_DIM)] * 8 + [
        jax.ShapeDtypeStruct((B, CONV_K - 1, CONV_DIM), F32),
        jax.ShapeDtypeStruct((B, 1, RWKV_PROJ), F32)]
    out_specs = [tok(CONV_DIM)] + [tok(RWKV_DIM)] * 8 + [
        pl.BlockSpec((nb, CONV_K - 1, CONV_DIM), lambda b, t: (b, 0, 0)),
        pl.BlockSpec((nb, 1, RWKV_PROJ), lambda b, t: (b, 0, 0))]
    return pl.pallas_call(
        functools.partial(_proj_kernel, nb=nb, tt=tt),
        grid=grid, in_specs=in_specs, out_specs=out_specs, out_shape=out_shape,
        scratch_shapes=[pltpu.VMEM((nb, tt + SUBLANES, CONV_DIM), F32),
                        pltpu.VMEM((nb, tt + SUBLANES, RWKV_PROJ), F32)],
        compiler_params=pltpu.CompilerParams(
            dimension_semantics=("arbitrary", "arbitrary"), vmem_limit_bytes=VMEM_LIMIT),
        name="proj",
    )(x, conv_buf, shift_prev, *params)


def _unit_lower_inverse(lows, c):
    row = lax.broadcasted_iota(jnp.int32, (c, c), 0)
    col = lax.broadcasted_iota(jnp.int32, (c, c), 1)
    eye = (row == col).astype(F32)
    base = min(INV_BASE, c)
    same = (row // base) == (col // base)
    ps = [jnp.where(same, low, 0.0) for low in lows]
    invs = [eye - p for p in ps]
    span = 2
    while span < base:
        ps = [_dot3(p, p) for p in ps]
        invs = [inv + _dot3(inv, p) for inv, p in zip(invs, ps)]
        span *= 2
    blk = base
    while blk < c:
        sel = ((row // (2 * blk)) == (col // (2 * blk))) & ((row // blk) != (col // blk))
        tmp = [_dot3(jnp.where(sel, low, 0.0), inv) for low, inv in zip(lows, invs)]
        invs = [inv - _dot3(inv, t) for inv, t in zip(invs, tmp)]
        blk *= 2
    return invs


def _dot_exact_lhs(a_bf16, b):
    b1 = b.astype(BF16)
    r1 = b - b1.astype(F32)
    b2 = r1.astype(BF16)
    b3 = (r1 - b2.astype(F32)).astype(BF16)
    return _dg(a_bf16, b1) + _dg(a_bf16, b2) + _dg(a_bf16, b3)


def _wkv_kernel(r_ref, k_ref, v_ref, kk_ref, b_ref, lw_ref, g_ref, bg_ref, s0_ref,
                gain_ref, bias_ref, y_ref, sout_ref, s_scr, *, nb, c):
    @pl.when(pl.program_id(1) == 0)
    def _():
        s_scr[...] = s0_ref[...]

    row = lax.broadcasted_iota(jnp.int32, (c, c), 0)
    col = lax.broadcasted_iota(jnp.int32, (c, c), 1)
    strict = row > col
    incl = row >= col
    row2 = lax.broadcasted_iota(jnp.int32, (c, 2 * c), 0)
    col2 = lax.broadcasted_iota(jnp.int32, (c, 2 * c), 1)
    incl2 = row2 >= jnp.where(col2 >= c, col2 - c, col2)
    tri = incl.astype(BF16)

    xl, xr, vh, xe, pe, st = [], [], [], [], [], []
    for i in range(nb):
        logw = lw_ref[i]
        cum = _dot_exact_lhs(tri, logw)
        cum_end = cum[c - 1:c, :]
        k = k_ref[i]
        b = b_ref[i]
        e_neg = jnp.exp(-cum)
        rt = r_ref[i] * jnp.exp(cum)
        at = kk_ref[i] * jnp.exp(cum - logw)
        bt = b * e_neg
        kt = k * e_neg
        d_end = jnp.exp(cum_end - cum)
        bh = b * d_end
        kh = k * d_end
        p_end = jnp.exp(cum_end)
        v = v_ref[i]
        for hd in range(N_HEADS):
            sl = slice(hd * HEAD_SIZE, (hd + 1) * HEAD_SIZE)
            xl.append(jnp.concatenate([at[:, sl], rt[:, sl]], axis=0))
            xr.append(jnp.concatenate([bt[:, sl], kt[:, sl]], axis=0))
            xe.append(jnp.concatenate([bh[:, sl], kh[:, sl]], axis=0))
            vh.append(v[:, sl])
            pe.append(p_end[:, sl])
            st.append(s_scr[i, hd])
    n = len(xl)

    gram = [_dot3(xl[j], xr[j], _NT) for j in range(n)]
    tinv = _unit_lower_inverse([jnp.where(strict, g[:c, :c], 0.0) for g in gram], c)
    yk = [_dot3(jnp.where(strict, gram[j][:c, c:], 0.0), vh[j]) for j in range(n)]
    wu = [_dot3(tinv[j], jnp.concatenate([xl[j][:c], -yk[j]], axis=1)) for j in range(n)]
    m_r = [jnp.where(incl2, g[c:, :], 0.0) for g in gram]

    gs = [_dot3(jnp.concatenate([wu[j][:, :HEAD_SIZE], xl[j][c:]], axis=0), st[j], _NT)
          for j in range(n)]
    uv = [jnp.concatenate([wu[j][:, HEAD_SIZE:] - gs[j][:c], vh[j]], axis=0) for j in range(n)]
    s_new = [st[j] * pe[j] + _dot3(uv[j], xe[j], _TN) for j in range(n)]
    o = [gs[j][c:] + _dot3(m_r[j], uv[j]) for j in range(n)]

    gain = gain_ref[...]
    bias = bias_ref[...]
    for i in range(nb):
        for hd in range(N_HEADS):
            j = i * N_HEADS + hd
            sl = slice(hd * HEAD_SIZE, (hd + 1) * HEAD_SIZE)
            s_scr[i, hd] = s_new[j]
            sout_ref[i, hd] = s_new[j]
            mu = jnp.mean(o[j], axis=-1, keepdims=True)
            var = jnp.mean(jnp.square(o[j] - mu), axis=-1, keepdims=True)
            on = (o[j] - mu) * lax.rsqrt(var + GN_EPS)
            y_ref[i, :, sl] = (on * gain[:, sl] + bias[:, sl]) * g_ref[i, :, sl] + bg_ref[i, :, sl]


def _wkv_call(seqs, s0, gain, bias, *, nb, c):
    B, T, _ = seqs[0].shape
    tok = pl.BlockSpec((nb, c, RWKV_DIM), lambda b, t: (b, t, 0))
    st = pl.BlockSpec((nb, N_HEADS, HEAD_SIZE, HEAD_SIZE), lambda b, t: (b, 0, 0, 0))
    vec = pl.BlockSpec((1, RWKV_DIM), lambda b, t: (0, 0))
    return pl.pallas_call(
        functools.partial(_wkv_kernel, nb=nb, c=c),
        grid=(B // nb, T // c),
        in_specs=[tok] * 8 + [st, vec, vec],
        out_specs=[tok, st],
        out_shape=[jax.ShapeDtypeStruct((B, T, RWKV_DIM), F32),
                   jax.ShapeDtypeStruct((B, N_HEADS, HEAD_SIZE, HEAD_SIZE), F32)],
        scratch_shapes=[pltpu.VMEM((nb, N_HEADS, HEAD_SIZE, HEAD_SIZE), F32)],
        compiler_params=pltpu.CompilerParams(
            dimension_semantics=("arbitrary", "arbitrary"), vmem_limit_bytes=VMEM_LIMIT),
        name="wkv",
    )(*seqs, s0, gain, bias)


def _out_kernel(x_ref, yc_ref, yr_ref, wout_ref, gpost_ref, gfpre_ref, gfpost_ref,
                wff1_ref, wff2_ref, o_ref):
    wout = wout_ref[...]
    mix = (_dg(yc_ref[...].astype(BF16), wout[:CONV_DIM])
           + _dg(yr_ref[...].astype(BF16), wout[CONV_DIM:]))
    x1 = x_ref[...] + _rmsnorm(mix, gpost_ref[...])
    h2 = _rmsnorm(x1, gfpre_ref[...]).astype(BF16)
    f1 = _dg(h2, wff1_ref[...])
    f1 = jnp.square(jnp.maximum(f1, 0.0)).astype(BF16)
    f2 = _dg(f1, wff2_ref[...])
    o_ref[...] = x1 + _rmsnorm(f2, gfpost_ref[...])


def _out_call(x, yc, yr, params, *, tm):
    n = x.shape[0]
    tok = lambda w: pl.BlockSpec((tm, w), lambda i: (i, 0))
    full = lambda arr: pl.BlockSpec(arr.shape, lambda i: (0,) * arr.ndim)
    return pl.pallas_call(
        _out_kernel,
        grid=(n // tm,),
        in_specs=[tok(D_MODEL), tok(CONV_DIM), tok(RWKV_DIM)] + [full(p) for p in params],
        out_specs=tok(D_MODEL),
        out_shape=jax.ShapeDtypeStruct((n, D_MODEL), F32),
        compiler_params=pltpu.CompilerParams(
            dimension_semantics=("arbitrary",), vmem_limit_bytes=VMEM_LIMIT),
        name="outffn",
    )(x, yc, yr, *params)


def _layer(x, conv_buf, shift_prev, wkv_state, proj_params, gn, out_params, *, nb, tt, c, tm):
    B, T, _ = x.shape
    outs = _proj_call(x, conv_buf, shift_prev.reshape(B, 1, RWKV_PROJ), proj_params, nb=nb, tt=tt)
    yconv, seqs, new_conv, new_shift = outs[0], outs[1:9], outs[9], outs[10]
    yr, new_wkv = _wkv_call(seqs, wkv_state, gn[0], gn[1], nb=2, c=c)
    y = _out_call(x.reshape(B * T, D_MODEL), yconv.reshape(B * T, CONV_DIM),
                  yr.reshape(B * T, RWKV_DIM), out_params, tm=tm)
    return y.reshape(B, T, D_MODEL), new_conv, new_shift.reshape(B, RWKV_PROJ), new_wkv


def kernel(x_prompt, x_sample, state_conv, state_shift, state_wkv, norm_mix_pre, norm_mix_post,
           norm_ffn_pre, norm_ffn_post, w_in, conv_w, shift_mu, w_decay2, decay_w0, w_a2, a0,
           w_g2, k_k, k_a, r_k, gn_gain, gn_bias, w_out, w_ff1, w_ff2):
    depth = w_in.shape[0]
    Bp = x_prompt.shape[0]
    xp, xs = x_prompt, x_sample
    hid = jnp.arange(RWKV_DIM) // HEAD_SIZE
    ones_bd = (hid[:, None] == hid[None, :]).astype(BF16)
    row = lambda t: t.reshape(1, -1).astype(F32)
    zpad = jnp.zeros((DECAY_LORA, RWKV_DIM), F32)
    res = [[] for _ in range(6)]
    for l in range(depth):
        proj_params = (
            row(norm_mix_pre[l]), w_in[l].astype(BF16), conv_w[l], row(shift_mu[l]),
            jnp.concatenate([w_decay2[l], zpad], axis=0), row(decay_w0[l]),
            jnp.concatenate([zpad, w_a2[l]], axis=0), row(a0[l]), w_g2[l],
            row(k_k[l]), row(k_a[l]), row(r_k[l]), ones_bd)
        gn = (row(gn_gain[l]), row(gn_bias[l]))
        out_params = (w_out[l].astype(BF16), row(norm_mix_post[l]), row(norm_ffn_pre[l]),
                      row(norm_ffn_post[l]), w_ff1[l].astype(BF16), w_ff2[l].astype(BF16))
        xp, c_p, s_p, w_p = _layer(
            xp, jnp.zeros((Bp, CONV_K - 1, CONV_DIM), F32), jnp.zeros((Bp, RWKV_PROJ), F32),
            jnp.zeros((Bp, N_HEADS, HEAD_SIZE, HEAD_SIZE), F32),
            proj_params, gn, out_params, nb=1, tt=256, c=64, tm=256)
        xs, c_s, s_s, w_s = _layer(
            xs, state_conv[l], state_shift[l], state_wkv[l],
            proj_params, gn, out_params, nb=xs.shape[0], tt=xs.shape[1], c=xs.shape[1],
            tm=xs.shape[0] * xs.shape[1])
        for lst, val in zip(res, (c_p, s_p, w_p, c_s, s_s, w_s)):
            lst.append(val)
    return (xp, xs) + tuple(jnp.stack(r) for r in res)
```

```python
import functools

import jax
import jax.numpy as jnp
import numpy as np
from jax import lax
from jax.experimental import pallas as pl
from jax.experimental.pallas import tpu as pltpu

D_MODEL = 1024
CONV_DIM = 512
RWKV_DIM = 512
HEAD_SIZE = 64
N_HEADS = 8
CONV_K = 3
DECAY_LORA = 64
AAA_LORA = 64
GATE_LORA = 128
RWKV_PROJ = 3 * RWKV_DIM + DECAY_LORA + AAA_LORA + GATE_LORA
PROJ_DIM = 3 * CONV_DIM + RWKV_PROJ
D_FF = 4 * D_MODEL
NORM_EPS = 1e-6
GN_EPS = 64e-5
DECAY_SCALE = float(np.exp(-0.5))

SUBLANES = 8
INV_BASE = 16
VMEM_LIMIT = 56 * 1024 * 1024

F32 = jnp.float32
BF16 = jnp.bfloat16

_NN = (((1,), (0,)), ((), ()))
_NT = (((1,), (1,)), ((), ()))
_TN = (((0,), (0,)), ((), ()))


def _dg(a, b, dims=_NN):
    return lax.dot_general(a, b, dims, preferred_element_type=F32)


def _dot1(a, b, dims=_NN):
    return _dg(a.astype(BF16), b.astype(BF16), dims)


def _bf16_terms(a, n):
    terms = []
    for _ in range(n - 1):
        t = a.astype(BF16)
        terms.append(t)
        a = a - t.astype(F32)
    terms.append(a.astype(BF16))
    return terms


def _rmsnorm(x, g):
    return x * lax.rsqrt(jnp.mean(x * x, axis=-1, keepdims=True) + NORM_EPS) * g


def _proj_kernel(x_ref, cbuf_ref, sprev_ref, gpre_ref, win_ref, convw_ref, mu_ref,
                 wd_ref, w0_ref, wa_ref, a0_ref, wg_ref, kk_ref, ka_ref, rk_ref, ones_ref,
                 yconv_ref, r_ref, k_ref, v_ref, kkn_ref, b_ref, lw_ref, g_ref, bg_ref,
                 nconv_ref, nshift_ref, ubuf, zbuf, *, nb, tt):
    m = nb * tt
    h0 = SUBLANES

    @pl.when(pl.program_id(1) == 0)
    def _():
        ubuf[:, h0 - 2:h0, :] = cbuf_ref[...]
        zbuf[:, h0 - 1:h0, :] = sprev_ref[...]

    x = x_ref[...].reshape(m, D_MODEL)
    h = _rmsnorm(x, gpre_ref[...]).astype(BF16)
    z = _dg(h, win_ref[...])
    zb = z[:, :CONV_DIM]
    zc = z[:, CONV_DIM:2 * CONV_DIM]
    zh = z[:, 2 * CONV_DIM:3 * CONV_DIM]
    zr = z[:, 3 * CONV_DIM:]

    ubuf[:, h0:, :] = (zc * zh).reshape(nb, tt, CONV_DIM)
    cw = convw_ref[...]
    yc = (ubuf[:, h0 - 2:h0 - 2 + tt, :] * cw[0:1, :]
          + ubuf[:, h0 - 1:h0 - 1 + tt, :] * cw[1:2, :]
          + ubuf[:, h0:h0 + tt, :] * cw[2:3, :])
    yconv_ref[...] = (zb.reshape(nb, tt, CONV_DIM) * yc)
    last_u = ubuf[:, h0 + tt - 2:h0 + tt, :]
    nconv_ref[...] = last_u
    ubuf[:, h0 - 2:h0, :] = last_u

    zbuf[:, h0:, :] = zr.reshape(nb, tt, RWKV_PROJ)
    zprev = zbuf[:, h0 - 1:h0 - 1 + tt, :].reshape(m, RWKV_PROJ)
    last_z = zbuf[:, h0 + tt - 1:h0 + tt, :]
    nshift_ref[...] = last_z
    zbuf[:, h0 - 1:h0, :] = last_z
    zs = zr + mu_ref[...] * (zprev - zr)

    r = zs[:, :RWKV_DIM]
    k = zs[:, RWKV_DIM:2 * RWKV_DIM]
    v = zs[:, 2 * RWKV_DIM:3 * RWKV_DIM]
    lwa = zs[:, 3 * RWKV_DIM:3 * RWKV_DIM + DECAY_LORA + AAA_LORA]
    lg = zs[:, 3 * RWKV_DIM + DECAY_LORA + AAA_LORA:]

    logit_w = w0_ref[...] + _dot1(jnp.tanh(lwa), wd_ref[...])
    logw = -DECAY_SCALE * jax.nn.sigmoid(logit_w)
    a = jax.nn.sigmoid(a0_ref[...] + _dot1(lwa, wa_ref[...]))
    g = _dot1(jax.nn.sigmoid(lg), wg_ref[...])

    ones_bd = ones_ref[...]
    kk = k * kk_ref[...]
    ss = _dg((kk * kk).astype(BF16), ones_bd)
    kk = kk / jnp.maximum(jnp.sqrt(ss), 1e-12)
    kp = k * (1.0 + (a - 1.0) * ka_ref[...])
    bonus = sum(_dg(t, ones_bd) for t in _bf16_terms(r * kp * rk_ref[...], 2)) * v

    shp = (nb, tt, RWKV_DIM)
    r_ref[...] = r.reshape(shp)
    k_ref[...] = kp.reshape(shp)
    v_ref[...] = v.reshape(shp)
    kkn_ref[...] = kk.reshape(shp)
    b_ref[...] = (kk * a).reshape(shp)
    lw_ref[...] = logw.reshape(shp)
    g_ref[...] = g.reshape(shp)
    bg_ref[...] = (bonus * g).reshape(shp)


def _proj_call(x, conv_buf, shift_prev, params, *, nb, tt):
    B, T, _ = x.shape
    grid = (B // nb, T // tt)
    tok = lambda w: pl.BlockSpec((nb, tt, w), lambda b, t: (b, t, 0))
    full = lambda arr: pl.BlockSpec(arr.shape, lambda b, t: (0,) * arr.ndim)
    in_specs = [tok(D_MODEL),
                pl.BlockSpec((nb, CONV_K - 1, CONV_DIM), lambda b, t: (b, 0, 0)),
                pl.BlockSpec((nb, 1, RWKV_PROJ), lambda b, t: (b, 0, 0))]
    in_specs += [full(p) for p in params]
    seq = lambda w: jax.ShapeDtypeStruct((B, T, w), F32)
    out_shape = [seq(CONV_DIM)] + [seq(RWKV_DIM)] * 8 + [
        jax.ShapeDtypeStruct((B, CONV_K - 1, CONV_DIM), F32),
        jax.ShapeDtypeStruct((B, 1, RWKV_PROJ), F32)]
    out_specs = [tok(CONV_DIM)] + [tok(RWKV_DIM)] * 8 + [
        pl.BlockSpec((nb, CONV_K - 1, CONV_DIM), lambda b, t: (b, 0, 0)),
        pl.BlockSpec((nb, 1, RWKV_PROJ), lambda b, t: (b, 0, 0))]
    return pl.pallas_call(
        functools.partial(_proj_kernel, nb=nb, tt=tt),
        grid=grid, in_specs=in_specs, out_specs=out_specs, out_shape=out_shape,
        scratch_shapes=[pltpu.VMEM((nb, tt + SUBLANES, CONV_DIM), F32),
                        pltpu.VMEM((nb, tt + SUBLANES, RWKV_PROJ), F32)],
        compiler_params=pltpu.CompilerParams(
            dimension_semantics=("arbitrary", "arbitrary"), vmem_limit_bytes=VMEM_LIMIT),
        name="proj",
    )(x, conv_buf, shift_prev, *params)


def _unit_lower_inverse(lows, c):
    row = lax.broadcasted_iota(jnp.int32, (c, c), 0)
    col = lax.broadcasted_iota(jnp.int32, (c, c), 1)
    eye = (row == col).astype(F32)
    base = min(INV_BASE, c)
    same = (row // base) == (col // base)
    ps = [jnp.where(same, low, 0.0) for low in lows]
    invs = [eye - p for p in ps]
    span = 2
    while span < base:
        ps = [_dot1(p, p) for p in ps]
        invs = [inv + _dot1(inv, p) for inv, p in zip(invs, ps)]
        span *= 2
    blk = base
    while blk < c:
        sel = ((row // (2 * blk)) == (col // (2 * blk))) & ((row // blk) != (col // blk))
        tmp = [_dot1(jnp.where(sel, low, 0.0), inv) for low, inv in zip(lows, invs)]
        invs = [inv - _dot1(inv, t) for inv, t in zip(invs, tmp)]
        blk *= 2
    return invs


def _wkv_kernel(r_ref, k_ref, v_ref, kk_ref, b_ref, lw_ref, g_ref, bg_ref, s0_ref,
                gain_ref, bias_ref, y_ref, sout_ref, s_scr, *, nb, c):
    @pl.when(pl.program_id(1) == 0)
    def _():
        s_scr[...] = s0_ref[...]

    row = lax.broadcasted_iota(jnp.int32, (c, c), 0)
    col = lax.broadcasted_iota(jnp.int32, (c, c), 1)
    strict = row > col
    incl = row >= col
    row2 = lax.broadcasted_iota(jnp.int32, (c, 2 * c), 0)
    col2 = lax.broadcasted_iota(jnp.int32, (c, 2 * c), 1)
    incl2 = row2 >= jnp.where(col2 >= c, col2 - c, col2)
    tri = incl.astype(BF16)

    xl, xr, vh, xe, pe, st = [], [], [], [], [], []
    for i in range(nb):
        logw = lw_ref[i]
        cum = sum(_dg(tri, t) for t in _bf16_terms(logw, 3))
        cum_end = cum[c - 1:c, :]
        k = k_ref[i]
        b = b_ref[i]
        e_neg = jnp.exp(-cum)
        rt = r_ref[i] * jnp.exp(cum)
        at = kk_ref[i] * jnp.exp(cum - logw)
        bt = b * e_neg
        kt = k * e_neg
        d_end = jnp.exp(cum_end - cum)
        bh = b * d_end
        kh = k * d_end
        p_end = jnp.exp(cum_end)
        v = v_ref[i]
        for hd in range(N_HEADS):
            sl = slice(hd * HEAD_SIZE, (hd + 1) * HEAD_SIZE)
            xl.append(jnp.concatenate([at[:, sl], rt[:, sl]], axis=0))
            xr.append(jnp.concatenate([bt[:, sl], kt[:, sl]], axis=0))
            xe.append(jnp.concatenate([bh[:, sl], kh[:, sl]], axis=0))
            vh.append(v[:, sl])
            pe.append(p_end[:, sl])
            st.append(s_scr[i, hd])
    n = len(xl)

    gram = [_dot1(xl[j], xr[j], _NT) for j in range(n)]
    tinv = _unit_lower_inverse([jnp.where(strict, g[:c, :c], 0.0) for g in gram], c)
    yk = [_dot1(jnp.where(strict, gram[j][:c, c:], 0.0), vh[j]) for j in range(n)]
    wu = [_dot1(tinv[j], jnp.concatenate([xl[j][:c], -yk[j]], axis=1)) for j in range(n)]
    m_r = [jnp.where(incl2, g[c:, :], 0.0) for g in gram]

    gs = [_dot1(jnp.concatenate([wu[j][:, :HEAD_SIZE], xl[j][c:]], axis=0), st[j], _NT)
          for j in range(n)]
    uv = [jnp.concatenate([wu[j][:, HEAD_SIZE:] - gs[j][:c], vh[j]], axis=0) for j in range(n)]
    s_new = [st[j] * pe[j] + _dot1(uv[j], xe[j], _TN) for j in range(n)]
    o = [gs[j][c:] + _dot1(m_r[j], uv[j]) for j in range(n)]

    gain = gain_ref[...]
    bias = bias_ref[...]
    for i in range(nb):
        for hd in range(N_HEADS):
            j = i * N_HEADS + hd
            sl = slice(hd * HEAD_SIZE, (hd + 1) * HEAD_SIZE)
            s_scr[i, hd] = s_new[j]
            sout_ref[i, hd] = s_new[j]
            mu = jnp.mean(o[j], axis=-1, keepdims=True)
            var = jnp.mean(jnp.square(o[j] - mu), axis=-1, keepdims=True)
            on = (o[j] - mu) * lax.rsqrt(var + GN_EPS)
            y_ref[i, :, sl] = (on * gain[:, sl] + bias[:, sl]) * g_ref[i, :, sl] + bg_ref[i, :, sl]


def _wkv_call(seqs, s0, gain, bias, *, nb, c):
    B, T, _ = seqs[0].shape
    tok = pl.BlockSpec((nb, c, RWKV_DIM), lambda b, t: (b, t, 0))
    st = pl.BlockSpec((nb, N_HEADS, HEAD_SIZE, HEAD_SIZE), lambda b, t: (b, 0, 0, 0))
    vec = pl.BlockSpec((1, RWKV_DIM), lambda b, t: (0, 0))
    return pl.pallas_call(
        functools.partial(_wkv_kernel, nb=nb, c=c),
        grid=(B // nb, T // c),
        in_specs=[tok] * 8 + [st, vec, vec],
        out_specs=[tok, st],
        out_shape=[jax.ShapeDtypeStruct((B, T, RWKV_DIM), F32),
                   jax.ShapeDtypeStruct((B, N_HEADS, HEAD_SIZE, HEAD_SIZE), F32)],
        scratch_shapes=[pltpu.VMEM((nb, N_HEADS, HEAD_SIZE, HEAD_SIZE), F32)],
        compiler_params=pltpu.CompilerParams(
            dimension_semantics=("arbitrary", "arbitrary"), vmem_limit_bytes=VMEM_LIMIT),
        name="wkv",
    )(*seqs, s0, gain, bias)


def _out_kernel(x_ref, yc_ref, yr_ref, wout_ref, gpost_ref, gfpre_ref, gfpost_ref,
                wff1_ref, wff2_ref, o_ref):
    wout = wout_ref[...]
    mix = (_dg(yc_ref[...].astype(BF16), wout[:CONV_DIM])
           + _dg(yr_ref[...].astype(BF16), wout[CONV_DIM:]))
    x1 = x_ref[...] + _rmsnorm(mix, gpost_ref[...])
    h2 = _rmsnorm(x1, gfpre_ref[...]).astype(BF16)
    f1 = _dg(h2, wff1_ref[...])
    f1 = jnp.square(jnp.maximum(f1, 0.0)).astype(BF16)
    f2 = _dg(f1, wff2_ref[...])
    o_ref[...] = x1 + _rmsnorm(f2, gfpost_ref[...])


def _out_call(x, yc, yr, params, *, tm):
    n = x.shape[0]
    tok = lambda w: pl.BlockSpec((tm, w), lambda i: (i, 0))
    full = lambda arr: pl.BlockSpec(arr.shape, lambda i: (0,) * arr.ndim)
    return pl.pallas_call(
        _out_kernel,
        grid=(n // tm,),
        in_specs=[tok(D_MODEL), tok(CONV_DIM), tok(RWKV_DIM)] + [full(p) for p in params],
        out_specs=tok(D_MODEL),
        out_shape=jax.ShapeDtypeStruct((n, D_MODEL), F32),
        compiler_params=pltpu.CompilerParams(
            dimension_semantics=("arbitrary",), vmem_limit_bytes=VMEM_LIMIT),
        name="outffn",
    )(x, yc, yr, *params)


def _layer(x, conv_buf, shift_prev, wkv_state, proj_params, gn, out_params, *, nb, tt, c, tm):
    B, T, _ = x.shape
    outs = _proj_call(x, conv_buf, shift_prev.reshape(B, 1, RWKV_PROJ), proj_params, nb=nb, tt=tt)
    yconv, seqs, new_conv, new_shift = outs[0], outs[1:9], outs[9], outs[10]
    yr, new_wkv = _wkv_call(seqs, wkv_state, gn[0], gn[1], nb=2, c=c)
    y = _out_call(x.reshape(B * T, D_MODEL), yconv.reshape(B * T, CONV_DIM),
                  yr.reshape(B * T, RWKV_DIM), out_params, tm=tm)
    return y.reshape(B, T, D_MODEL), new_conv, new_shift.reshape(B, RWKV_PROJ), new_wkv


def kernel(x_prompt, x_sample, state_conv, state_shift, state_wkv, norm_mix_pre, norm_mix_post,
           norm_ffn_pre, norm_ffn_post, w_in, conv_w, shift_mu, w_decay2, decay_w0, w_a2, a0,
           w_g2, k_k, k_a, r_k, gn_gain, gn_bias, w_out, w_ff1, w_ff2):
    depth = w_in.shape[0]
    Bp = x_prompt.shape[0]
    xp, xs = x_prompt, x_sample
    hid = jnp.arange(RWKV_DIM) // HEAD_SIZE
    ones_bd = (hid[:, None] == hid[None, :]).astype(BF16)
    row = lambda t: t.reshape(1, -1).astype(F32)
    zpad = jnp.zeros((DECAY_LORA, RWKV_DIM), F32)
    res = [[] for _ in range(6)]
    for l in range(depth):
        proj_params = (
            row(norm_mix_pre[l]), w_in[l].astype(BF16), conv_w[l], row(shift_mu[l]),
            jnp.concatenate([w_decay2[l], zpad], axis=0), row(decay_w0[l]),
            jnp.concatenate([zpad, w_a2[l]], axis=0), row(a0[l]), w_g2[l],
            row(k_k[l]), row(k_a[l]), row(r_k[l]), ones_bd)
        gn = (row(gn_gain[l]), row(gn_bias[l]))
        out_params = (w_out[l].astype(BF16), row(norm_mix_post[l]), row(norm_ffn_pre[l]),
                      row(norm_ffn_post[l]), w_ff1[l].astype(BF16), w_ff2[l].astype(BF16))
        xp, c_p, s_p, w_p = _layer(
            xp, jnp.zeros((Bp, CONV_K - 1, CONV_DIM), F32), jnp.zeros((Bp, RWKV_PROJ), F32),
            jnp.zeros((Bp, N_HEADS, HEAD_SIZE, HEAD_SIZE), F32),
            proj_params, gn, out_params, nb=1, tt=256, c=64, tm=256)
        xs, c_s, s_s, w_s = _layer(
            xs, state_conv[l], state_shift[l], state_wkv[l],
            proj_params, gn, out_params, nb=xs.shape[0], tt=xs.shape[1], c=xs.shape[1],
            tm=xs.shape[0] * xs.shape[1])
        for lst, val in zip(res, (c_p, s_p, w_p, c_s, s_s, w_s)):
            lst.append(val)
    return (xp, xs) + tuple(jnp.stack(r) for r in res)
```

```python
import functools

import jax
import jax.numpy as jnp
import numpy as np
from jax import lax
from jax.experimental import pallas as pl
from jax.experimental.pallas import tpu as pltpu

D_MODEL = 1024
CONV_DIM = 512
RWKV_DIM = 512
HEAD_SIZE = 64
N_HEADS = 8
CONV_K = 3
DECAY_LORA = 64
AAA_LORA = 64
GATE_LORA = 128
RWKV_PROJ = 3 * RWKV_DIM + DECAY_LORA + AAA_LORA + GATE_LORA
PROJ_DIM = 3 * CONV_DIM + RWKV_PROJ
D_FF = 4 * D_MODEL
NORM_EPS = 1e-6
GN_EPS = 64e-5
DECAY_SCALE = float(np.exp(-0.5))

SUBLANES = 8
INV_BASE = 16
VMEM_LIMIT = 56 * 1024 * 1024

F32 = jnp.float32
BF16 = jnp.bfloat16

_NN = (((1,), (0,)), ((), ()))
_NT = (((1,), (1,)), ((), ()))


def _dg(a, b, dims=_NN):
    return lax.dot_general(a, b, dims, preferred_element_type=F32)


def _dot1(a, b, dims=_NN):
    return _dg(a.astype(BF16), b.astype(BF16), dims)


def _bf16_terms(a, n):
    terms = []
    for _ in range(n - 1):
        t = a.astype(BF16)
        terms.append(t)
        a = a - t.astype(F32)
    terms.append(a.astype(BF16))
    return terms


def _rmsnorm(x, g):
    return x * lax.rsqrt(jnp.mean(x * x, axis=-1, keepdims=True) + NORM_EPS) * g


def _proj_kernel(x_ref, cbuf_ref, sprev_ref, gpre_ref, win_ref, convw_ref, mu_ref,
                 wd_ref, w0_ref, wa_ref, a0_ref, wg_ref, kk_ref, ka_ref, rk_ref, ones_ref,
                 yconv_ref, r_ref, k_ref, v_ref, kkn_ref, b_ref, lw_ref, g_ref, bg_ref,
                 nconv_ref, nshift_ref, ubuf, zbuf, *, nb, tt):
    m = nb * tt
    h0 = SUBLANES

    @pl.when(pl.program_id(1) == 0)
    def _():
        ubuf[:, h0 - 2:h0, :] = cbuf_ref[...]
        zbuf[:, h0 - 1:h0, :] = sprev_ref[...]

    x = x_ref[...].reshape(m, D_MODEL)
    h = _rmsnorm(x, gpre_ref[...]).astype(BF16)
    z = _dg(h, win_ref[...])
    zb = z[:, :CONV_DIM]
    zc = z[:, CONV_DIM:2 * CONV_DIM]
    zh = z[:, 2 * CONV_DIM:3 * CONV_DIM]
    zr = z[:, 3 * CONV_DIM:]

    ubuf[:, h0:, :] = (zc * zh).reshape(nb, tt, CONV_DIM)
    cw = convw_ref[...]
    yc = (ubuf[:, h0 - 2:h0 - 2 + tt, :] * cw[0:1, :]
          + ubuf[:, h0 - 1:h0 - 1 + tt, :] * cw[1:2, :]
          + ubuf[:, h0:h0 + tt, :] * cw[2:3, :])
    yconv_ref[...] = (zb.reshape(nb, tt, CONV_DIM) * yc)
    last_u = ubuf[:, h0 + tt - 2:h0 + tt, :]
    nconv_ref[...] = last_u
    ubuf[:, h0 - 2:h0, :] = last_u

    zbuf[:, h0:, :] = zr.reshape(nb, tt, RWKV_PROJ)
    zprev = zbuf[:, h0 - 1:h0 - 1 + tt, :].reshape(m, RWKV_PROJ)
    last_z = zbuf[:, h0 + tt - 1:h0 + tt, :]
    nshift_ref[...] = last_z
    zbuf[:, h0 - 1:h0, :] = last_z
    zs = zr + mu_ref[...] * (zprev - zr)

    r = zs[:, :RWKV_DIM]
    k = zs[:, RWKV_DIM:2 * RWKV_DIM]
    v = zs[:, 2 * RWKV_DIM:3 * RWKV_DIM]
    lwa = zs[:, 3 * RWKV_DIM:3 * RWKV_DIM + DECAY_LORA + AAA_LORA]
    lg = zs[:, 3 * RWKV_DIM + DECAY_LORA + AAA_LORA:]

    logit_w = w0_ref[...] + _dot1(jnp.tanh(lwa), wd_ref[...])
    logw = -DECAY_SCALE * jax.nn.sigmoid(logit_w)
    a = jax.nn.sigmoid(a0_ref[...] + _dot1(lwa, wa_ref[...]))
    g = _dot1(jax.nn.sigmoid(lg), wg_ref[...])

    ones_bd = ones_ref[...]
    kk = k * kk_ref[...]
    ss = _dg((kk * kk).astype(BF16), ones_bd)
    kk = kk / jnp.maximum(jnp.sqrt(ss), 1e-12)
    kp = k * (1.0 + (a - 1.0) * ka_ref[...])
    bonus = sum(_dg(t, ones_bd) for t in _bf16_terms(r * kp * rk_ref[...], 2)) * v

    shp = (nb, tt, RWKV_DIM)
    r_ref[...] = r.reshape(shp)
    k_ref[...] = kp.reshape(shp)
    v_ref[...] = v.reshape(shp)
    kkn_ref[...] = kk.reshape(shp)
    b_ref[...] = (kk * a).reshape(shp)
    lw_ref[...] = logw.reshape(shp)
    g_ref[...] = g.reshape(shp)
    bg_ref[...] = (bonus * g).reshape(shp)


def _proj_call(x, conv_buf, shift_prev, params, *, nb, tt):
    B, T, _ = x.shape
    grid = (B // nb, T // tt)
    tok = lambda w: pl.BlockSpec((nb, tt, w), lambda b, t: (b, t, 0))
    full = lambda arr: pl.BlockSpec(arr.shape, lambda b, t: (0,) * arr.ndim)
    in_specs = [tok(D_MODEL),
                pl.BlockSpec((nb, CONV_K - 1, CONV_DIM), lambda b, t: (b, 0, 0)),
                pl.BlockSpec((nb, 1, RWKV_PROJ), lambda b, t: (b, 0, 0))]
    in_specs += [full(p) for p in params]
    seq = lambda w: jax.ShapeDtypeStruct((B, T, w), F32)
    out_shape = [seq(CONV_DIM)] + [seq(RWKV_DIM)] * 8 + [
        jax.ShapeDtypeStruct((B, CONV_K - 1, CONV_DIM), F32),
        jax.ShapeDtypeStruct((B, 1, RWKV_PROJ), F32)]
    out_specs = [tok(CONV_DIM)] + [tok(RWKV_DIM)] * 8 + [
        pl.BlockSpec((nb, CONV_K - 1, CONV_DIM), lambda b, t: (b, 0, 0)),
        pl.BlockSpec((nb, 1, RWKV_PROJ), lambda b, t: (b, 0, 0))]
    return pl.pallas_call(
        functools.partial(_proj_kernel, nb=nb, tt=tt),
        grid=grid, in_specs=in_specs, out_specs=out_specs, out_shape=out_shape,
        scratch_shapes=[pltpu.VMEM((nb, tt + SUBLANES, CONV_DIM), F32),
                        pltpu.VMEM((nb, tt + SUBLANES, RWKV_PROJ), F32)],
        compiler_params=pltpu.CompilerParams(
            dimension_semantics=("arbitrary", "arbitrary"), vmem_limit_bytes=VMEM_LIMIT),
        name="proj",
    )(x, conv_buf, shift_prev, *params)


def _unit_lower_inverse(lows, c):
    row = lax.broadcasted_iota(jnp.int32, (c, c), 0)
    col = lax.broadcasted_iota(jnp.int32, (c, c), 1)
    eye = (row == col).astype(F32)
    base = min(INV_BASE, c)
    same = (row // base) == (col // base)
    ps = [jnp.where(same, low, 0.0) for low in lows]
    invs = [eye - p for p in ps]
    span = 2
    while span < base:
        ps = [_dot1(p, p) for p in ps]
        invs = [inv + _dot1(inv, p) for inv, p in zip(invs, ps)]
        span *= 2
    blk = base
    while blk < c:
        sel = ((row // (2 * blk)) == (col // (2 * blk))) & ((row // blk) != (col // blk))
        tmp = [_dot1(jnp.where(sel, low, 0.0), inv) for low, inv in zip(lows, invs)]
        invs = [inv - _dot1(inv, t) for inv, t in zip(invs, tmp)]
        blk *= 2
    return invs


def _wkv_kernel(r_ref, k_ref, v_ref, kk_ref, b_ref, lw_ref, g_ref, bg_ref, s0_ref,
                gain_ref, bias_ref, y_ref, sout_ref, s_scr, *, nb, c, nc):
    @pl.when(pl.program_id(1) == 0)
    def _():
        s_scr[...] = s0_ref[...]

    row = lax.broadcasted_iota(jnp.int32, (c, c), 0)
    col = lax.broadcasted_iota(jnp.int32, (c, c), 1)
    strict = row > col
    incl = row >= col
    row2 = lax.broadcasted_iota(jnp.int32, (c, 2 * c), 0)
    col2 = lax.broadcasted_iota(jnp.int32, (c, 2 * c), 1)
    incl2 = row2 >= jnp.where(col2 >= c, col2 - c, col2)
    tri = incl.astype(BF16)

    xl, xr, xe, vh, pe = [], [], [], [], []
    for q in range(nc):
        rows = slice(q * c, (q + 1) * c)
        for i in range(nb):
            logw = lw_ref[i, rows, :]
            cum = sum(_dg(tri, t) for t in _bf16_terms(logw, 3))
            cum_end = cum[c - 1:c, :]
            k = k_ref[i, rows, :]
            b = b_ref[i, rows, :]
            e_neg = jnp.exp(-cum)
            rt = r_ref[i, rows, :] * jnp.exp(cum)
            at = kk_ref[i, rows, :] * jnp.exp(cum - logw)
            bt = b * e_neg
            kt = k * e_neg
            d_end = jnp.exp(cum_end - cum)
            bh = b * d_end
            kh = k * d_end
            p_end = jnp.exp(cum_end)
            v = v_ref[i, rows, :]
            for hd in range(N_HEADS):
                sl = slice(hd * HEAD_SIZE, (hd + 1) * HEAD_SIZE)
                xl.append(jnp.concatenate([at[:, sl], rt[:, sl]], axis=0))
                xr.append(jnp.concatenate([bt[:, sl], kt[:, sl]], axis=0))
                xe.append(jnp.concatenate([bh[:, sl], kh[:, sl]], axis=0).astype(BF16))
                vh.append(v[:, sl])
                pe.append(p_end[:, sl])
    n = len(xl)

    gram = [_dot1(xl[j], xr[j], _NT) for j in range(n)]
    tinv = _unit_lower_inverse([jnp.where(strict, g[:c, :c], 0.0) for g in gram], c)
    yk = [_dot1(jnp.where(strict, gram[j][:c, c:], 0.0), vh[j]) for j in range(n)]
    wu = [_dot1(tinv[j], jnp.concatenate([xl[j][:c], -yk[j]], axis=1)) for j in range(n)]
    w_b = [wu[j][:, :HEAD_SIZE].astype(BF16) for j in range(n)]
    lg = [jnp.concatenate([w_b[j], xl[j][c:].astype(BF16)], axis=0) for j in range(n)]
    u0 = [wu[j][:, HEAD_SIZE:] for j in range(n)]
    u0t = [u.T for u in u0]
    vt = [x.T.astype(BF16) for x in vh]
    vb = [x.astype(BF16) for x in vh]
    m_r = [jnp.where(incl2, g[c:, :], 0.0).astype(BF16) for g in gram]

    nh = nb * N_HEADS
    st = [s_scr[jj // N_HEADS, jj % N_HEADS] for jj in range(nh)]
    gain = gain_ref[...]
    bias = bias_ref[...]
    for q in range(nc):
        rows = slice(q * c, (q + 1) * c)
        js = [q * nh + jj for jj in range(nh)]
        sb = [s.astype(BF16) for s in st]
        ut = [u0t[j] - _dg(sb[jj], w_b[j], _NT) for jj, j in enumerate(js)]
        gs = [_dg(lg[j], sb[jj], _NT) for jj, j in enumerate(js)]
        st = [st[jj] * pe[j]
              + _dg(jnp.concatenate([ut[jj].astype(BF16), vt[j]], axis=1), xe[j])
              for jj, j in enumerate(js)]
        uv = [jnp.concatenate([(u0[j] - gs[jj][:c]).astype(BF16), vb[j]], axis=0)
              for jj, j in enumerate(js)]
        o = [gs[jj][c:] + _dg(m_r[j], uv[jj]) for jj, j in enumerate(js)]
        mu = [jnp.mean(x, axis=-1, keepdims=True) for x in o]
        dev = [x - m for x, m in zip(o, mu)]
        var = [jnp.mean(jnp.square(d), axis=-1, keepdims=True) for d in dev]
        for jj in range(nh):
            i, hd = jj // N_HEADS, jj % N_HEADS
            sl = slice(hd * HEAD_SIZE, (hd + 1) * HEAD_SIZE)
            on = dev[jj] * lax.rsqrt(var[jj] + GN_EPS)
            y_ref[i, rows, sl] = ((on * gain[:, sl] + bias[:, sl]) * g_ref[i, rows, sl]
                                  + bg_ref[i, rows, sl])

    for jj in range(nh):
        s_scr[jj // N_HEADS, jj % N_HEADS] = st[jj]
        sout_ref[jj // N_HEADS, jj % N_HEADS] = st[jj]


def _wkv_call(seqs, s0, gain, bias, *, nb, c, nc):
    B, T, _ = seqs[0].shape
    tok = pl.BlockSpec((nb, nc * c, RWKV_DIM), lambda b, t: (b, t, 0))
    st = pl.BlockSpec((nb, N_HEADS, HEAD_SIZE, HEAD_SIZE), lambda b, t: (b, 0, 0, 0))
    vec = pl.BlockSpec((1, RWKV_DIM), lambda b, t: (0, 0))
    return pl.pallas_call(
        functools.partial(_wkv_kernel, nb=nb, c=c, nc=nc),
        grid=(B // nb, T // (nc * c)),
        in_specs=[tok] * 8 + [st, vec, vec],
        out_specs=[tok, st],
        out_shape=[jax.ShapeDtypeStruct((B, T, RWKV_DIM), F32),
                   jax.ShapeDtypeStruct((B, N_HEADS, HEAD_SIZE, HEAD_SIZE), F32)],
        scratch_shapes=[pltpu.VMEM((nb, N_HEADS, HEAD_SIZE, HEAD_SIZE), F32)],
        compiler_params=pltpu.CompilerParams(
            dimension_semantics=("arbitrary", "arbitrary"), vmem_limit_bytes=VMEM_LIMIT),
        name="wkv",
    )(*seqs, s0, gain, bias)


def _out_kernel(x_ref, yc_ref, yr_ref, wout_ref, gpost_ref, gfpre_ref, gfpost_ref,
                wff1_ref, wff2_ref, o_ref):
    wout = wout_ref[...]
    mix = (_dg(yc_ref[...].astype(BF16), wout[:CONV_DIM])
           + _dg(yr_ref[...].astype(BF16), wout[CONV_DIM:]))
    x1 = x_ref[...] + _rmsnorm(mix, gpost_ref[...])
    h2 = _rmsnorm(x1, gfpre_ref[...]).astype(BF16)
    f1 = _dg(h2, wff1_ref[...])
    f1 = jnp.square(jnp.maximum(f1, 0.0)).astype(BF16)
    f2 = _dg(f1, wff2_ref[...])
    o_ref[...] = x1 + _rmsnorm(f2, gfpost_ref[...])


def _out_call(x, yc, yr, params, *, tm):
    n = x.shape[0]
    tok = lambda w: pl.BlockSpec((tm, w), lambda i: (i, 0))
    full = lambda arr: pl.BlockSpec(arr.shape, lambda i: (0,) * arr.ndim)
    return pl.pallas_call(
        _out_kernel,
        grid=(n // tm,),
        in_specs=[tok(D_MODEL), tok(CONV_DIM), tok(RWKV_DIM)] + [full(p) for p in params],
        out_specs=tok(D_MODEL),
        out_shape=jax.ShapeDtypeStruct((n, D_MODEL), F32),
        compiler_params=pltpu.CompilerParams(
            dimension_semantics=("arbitrary",), vmem_limit_bytes=VMEM_LIMIT),
        name="outffn",
    )(x, yc, yr, *params)


def _tiles(B, T):
    if T >= 256:
        return 1, 256, 2, 64, 2, 256
    return B, T, 4, T, 1, B * T


def _layer(x, conv_buf, shift_prev, wkv_state, proj_params, gn, out_params):
    B, T, _ = x.shape
    pnb, ptt, wnb, wc, wnc, tm = _tiles(B, T)
    outs = _proj_call(x, conv_buf, shift_prev.reshape(B, 1, RWKV_PROJ), proj_params,
                      nb=pnb, tt=ptt)
    yconv, seqs, new_conv, new_shift = outs[0], outs[1:9], outs[9], outs[10]
    yr, new_wkv = _wkv_call(seqs, wkv_state, *gn, nb=wnb, c=wc, nc=wnc)
    y = _out_call(x.reshape(B * T, D_MODEL), yconv.reshape(B * T, CONV_DIM),
                  yr.reshape(B * T, RWKV_DIM), out_params, tm=tm)
    return y.reshape(B, T, D_MODEL), new_conv, new_shift.reshape(B, RWKV_PROJ), new_wkv


def kernel(x_prompt, x_sample, state_conv, state_shift, state_wkv, norm_mix_pre, norm_mix_post,
           norm_ffn_pre, norm_ffn_post, w_in, conv_w, shift_mu, w_decay2, decay_w0, w_a2, a0,
           w_g2, k_k, k_a, r_k, gn_gain, gn_bias, w_out, w_ff1, w_ff2):
    depth = w_in.shape[0]
    Bp = x_prompt.shape[0]
    xp, xs = x_prompt, x_sample
    hid = jnp.arange(RWKV_DIM) // HEAD_SIZE
    ones_bd = (hid[:, None] == hid[None, :]).astype(BF16)
    row = lambda t: t.reshape(1, -1).astype(F32)
    zpad = jnp.zeros((DECAY_LORA, RWKV_DIM), F32)
    res = [[] for _ in range(6)]
    for l in range(depth):
        proj_params = (
            row(norm_mix_pre[l]), w_in[l].astype(BF16), conv_w[l], row(shift_mu[l]),
            jnp.concatenate([w_decay2[l], zpad], axis=0), row(decay_w0[l]),
            jnp.concatenate([zpad, w_a2[l]], axis=0), row(a0[l]), w_g2[l],
            row(k_k[l]), row(k_a[l]), row(r_k[l]), ones_bd)
        gn = (row(gn_gain[l]), row(gn_bias[l]))
        out_params = (w_out[l].astype(BF16), row(norm_mix_post[l]), row(norm_ffn_pre[l]),
                      row(norm_ffn_post[l]), w_ff1[l].astype(BF16), w_ff2[l].astype(BF16))
        xp, c_p, s_p, w_p = _layer(
            xp, jnp.zeros((Bp, CONV_K - 1, CONV_DIM), F32), jnp.zeros((Bp, RWKV_PROJ), F32),
            jnp.zeros((Bp, N_HEADS, HEAD_SIZE, HEAD_SIZE), F32), proj_params, gn, out_params)
        xs, c_s, s_s, w_s = _layer(xs, state_conv[l], state_shift[l], state_wkv[l],
                                   proj_params, gn, out_params)
        for lst, val in zip(res, (c_p, s_p, w_p, c_s, s_s, w_s)):
            lst.append(val)
    return (xp, xs) + tuple(jnp.stack(r) for r in res)
```

```python
import functools

import jax
import jax.numpy as jnp
import numpy as np
from jax import lax
from jax.experimental import pallas as pl
from jax.experimental.pallas import tpu as pltpu

D_MODEL = 1024
CONV_DIM = 512
RWKV_DIM = 512
HEAD_SIZE = 64
N_HEADS = 8
CONV_K = 3
DECAY_LORA = 64
AAA_LORA = 64
GATE_LORA = 128
RWKV_PROJ = 3 * RWKV_DIM + DECAY_LORA + AAA_LORA + GATE_LORA
PROJ_DIM = 3 * CONV_DIM + RWKV_PROJ
D_FF = 4 * D_MODEL
NORM_EPS = 1e-6
GN_EPS = 64e-5
DECAY_SCALE = float(np.exp(-0.5))

SUBLANES = 8
INV_BASE = 16
VMEM_LIMIT = 56 * 1024 * 1024

F32 = jnp.float32
BF16 = jnp.bfloat16

_NN = (((1,), (0,)), ((), ()))
_NT = (((1,), (1,)), ((), ()))


def _dg(a, b, dims=_NN):
    return lax.dot_general(a, b, dims, preferred_element_type=F32)


def _dot1(a, b, dims=_NN):
    return _dg(a.astype(BF16), b.astype(BF16), dims)


def _bf16_terms(a, n):
    terms = []
    for _ in range(n - 1):
        t = a.astype(BF16)
        terms.append(t)
        a = a - t.astype(F32)
    terms.append(a.astype(BF16))
    return terms


def _sigmoid(x):
    return 0.5 * jnp.tanh(0.5 * x) + 0.5


def _rmsnorm(x, g):
    return x * lax.rsqrt(jnp.mean(x * x, axis=-1, keepdims=True) + NORM_EPS) * g


def _proj_stream(sq, tt, x_ref, gpre_ref, win_ref, convw_ref, mu_ref, wd_ref, w0_ref, wa_ref,
                 a0_ref, wg_ref, kk_ref, ka_ref, rk_ref, ones_ref, yconv_ref, r_ref, k_ref,
                 v_ref, kkn_ref, b_ref, lw_ref, g_ref, bg_ref, nconv_ref, nshift_ref, ubuf, zbuf):
    nbs = sq.stop - sq.start
    m = nbs * tt
    h0 = SUBLANES

    x = x_ref[sq].reshape(m, D_MODEL)
    h = _rmsnorm(x, gpre_ref[...]).astype(BF16)
    z = _dg(h, win_ref[...])
    zb = z[:, :CONV_DIM]
    zc = z[:, CONV_DIM:2 * CONV_DIM]
    zh = z[:, 2 * CONV_DIM:3 * CONV_DIM]
    zr = z[:, 3 * CONV_DIM:]
    yield

    ubuf[sq, h0:, :] = (zc * zh).reshape(nbs, tt, CONV_DIM)
    cw = convw_ref[...]
    yc = (ubuf[sq, h0 - 2:h0 - 2 + tt, :] * cw[0:1, :]
          + ubuf[sq, h0 - 1:h0 - 1 + tt, :] * cw[1:2, :]
          + ubuf[sq, h0:h0 + tt, :] * cw[2:3, :])
    yconv_ref[sq] = (zb.reshape(nbs, tt, CONV_DIM) * yc)
    last_u = ubuf[sq, h0 + tt - 2:h0 + tt, :]
    nconv_ref[sq] = last_u
    ubuf[sq, h0 - 2:h0, :] = last_u

    zbuf[sq, h0:, :] = zr.reshape(nbs, tt, RWKV_PROJ)
    zprev = zbuf[sq, h0 - 1:h0 - 1 + tt, :].reshape(m, RWKV_PROJ)
    last_z = zbuf[sq, h0 + tt - 1:h0 + tt, :]
    nshift_ref[sq] = last_z
    zbuf[sq, h0 - 1:h0, :] = last_z
    zs = zr + mu_ref[...] * (zprev - zr)

    r = zs[:, :RWKV_DIM]
    k = zs[:, RWKV_DIM:2 * RWKV_DIM]
    v = zs[:, 2 * RWKV_DIM:3 * RWKV_DIM]
    lwa = zs[:, 3 * RWKV_DIM:3 * RWKV_DIM + DECAY_LORA + AAA_LORA]
    lg = zs[:, 3 * RWKV_DIM + DECAY_LORA + AAA_LORA:]
    yield

    logit_w = w0_ref[...] + _dot1(jnp.tanh(lwa), wd_ref[...])
    logw = -DECAY_SCALE * _sigmoid(logit_w)
    a = _sigmoid(a0_ref[...] + _dot1(lwa, wa_ref[...]))
    g = _dot1(_sigmoid(lg), wg_ref[...])
    yield

    ones_bd = ones_ref[...]
    kk = k * kk_ref[...]
    ss = _dg((kk * kk).astype(BF16), ones_bd)
    kp = k * (1.0 + (a - 1.0) * ka_ref[...])
    bsum = sum(_dg(t, ones_bd) for t in _bf16_terms(r * kp * rk_ref[...], 2))
    yield

    kk = kk * lax.rsqrt(jnp.maximum(ss, 1e-24))
    shp = (nbs, tt, RWKV_DIM)
    r_ref[sq] = r.reshape(shp)
    k_ref[sq] = kp.reshape(shp)
    v_ref[sq] = v.reshape(shp)
    kkn_ref[sq] = kk.reshape(shp)
    b_ref[sq] = (kk * a).reshape(shp)
    lw_ref[sq] = logw.reshape(shp)
    g_ref[sq] = g.reshape(shp)
    bg_ref[sq] = (bsum * v * g).reshape(shp)


def _proj_kernel(x_ref, cbuf_ref, sprev_ref, *rest, nb, tt, ns):
    ubuf, zbuf = rest[-2:]

    @pl.when(pl.program_id(1) == 0)
    def _():
        ubuf[:, SUBLANES - 2:SUBLANES, :] = cbuf_ref[...]
        zbuf[:, SUBLANES - 1:SUBLANES, :] = sprev_ref[...]

    per = nb // ns
    streams = [_proj_stream(slice(i * per, (i + 1) * per), tt, x_ref, *rest) for i in range(ns)]
    while streams:
        streams = [s for s in streams if next(s, True) is None]


def _proj_call(x, conv_buf, shift_prev, params, *, nb, tt, ns):
    B, T, _ = x.shape
    grid = (B // nb, T // tt)
    tok = lambda w: pl.BlockSpec((nb, tt, w), lambda b, t: (b, t, 0))
    full = lambda arr: pl.BlockSpec(arr.shape, lambda b, t: (0,) * arr.ndim)
    in_specs = [tok(D_MODEL),
                pl.BlockSpec((nb, CONV_K - 1, CONV_DIM), lambda b, t: (b, 0, 0)),
                pl.BlockSpec((nb, 1, RWKV_PROJ), lambda b, t: (b, 0, 0))]
    in_specs += [full(p) for p in params]
    seq = lambda w: jax.ShapeDtypeStruct((B, T, w), F32)
    out_shape = [seq(CONV_DIM)] + [seq(RWKV_DIM)] * 8 + [
        jax.ShapeDtypeStruct((B, CONV_K - 1, CONV_DIM), F32),
        jax.ShapeDtypeStruct((B, 1, RWKV_PROJ), F32)]
    out_specs = [tok(CONV_DIM)] + [tok(RWKV_DIM)] * 8 + [
        pl.BlockSpec((nb, CONV_K - 1, CONV_DIM), lambda b, t: (b, 0, 0)),
        pl.BlockSpec((nb, 1, RWKV_PROJ), lambda b, t: (b, 0, 0))]
    return pl.pallas_call(
        functools.partial(_proj_kernel, nb=nb, tt=tt, ns=ns),
        grid=grid, in_specs=in_specs, out_specs=out_specs, out_shape=out_shape,
        scratch_shapes=[pltpu.VMEM((nb, tt + SUBLANES, CONV_DIM), F32),
                        pltpu.VMEM((nb, tt + SUBLANES, RWKV_PROJ), F32)],
        compiler_params=pltpu.CompilerParams(
            dimension_semantics=("arbitrary", "arbitrary"), vmem_limit_bytes=VMEM_LIMIT),
        name="proj",
    )(x, conv_buf, shift_prev, *params)


def _unit_lower_inverse(lows, c):
    row = lax.broadcasted_iota(jnp.int32, (c, c), 0)
    col = lax.broadcasted_iota(jnp.int32, (c, c), 1)
    eye = (row == col).astype(F32)
    base = min(INV_BASE, c)
    same = (row // base) == (col // base)
    ps = [jnp.where(same, low, 0.0) for low in lows]
    invs = [eye - p for p in ps]
    span = 2
    while span < base:
        ps = [_dot1(p, p) for p in ps]
        invs = [inv + _dot1(inv, p) for inv, p in zip(invs, ps)]
        span *= 2
    blk = base
    while blk < c:
        sel = ((row // (2 * blk)) == (col // (2 * blk))) & ((row // blk) != (col // blk))
        tmp = [_dot1(jnp.where(sel, low, 0.0), inv) for low, inv in zip(lows, invs)]
        invs = [inv - _dot1(inv, t) for inv, t in zip(invs, tmp)]
        blk *= 2
    return invs


def _wkv_kernel(r_ref, k_ref, v_ref, kk_ref, b_ref, lw_ref, g_ref, bg_ref, s0_ref,
                gain_ref, bias_ref, y_ref, sout_ref, s_scr, *, nb, c, nc):
    @pl.when(pl.program_id(1) == 0)
    def _():
        s_scr[...] = s0_ref[...]

    row = lax.broadcasted_iota(jnp.int32, (c, c), 0)
    col = lax.broadcasted_iota(jnp.int32, (c, c), 1)
    strict = row > col
    incl = row >= col
    row2 = lax.broadcasted_iota(jnp.int32, (c, 2 * c), 0)
    col2 = lax.broadcasted_iota(jnp.int32, (c, 2 * c), 1)
    incl2 = row2 >= jnp.where(col2 >= c, col2 - c, col2)
    tri = incl.astype(BF16)

    xl, xr, xe, vh, pe = [], [], [], [], []
    for q in range(nc):
        rows = slice(q * c, (q + 1) * c)
        for i in range(nb):
            logw = lw_ref[i, rows, :]
            cum = sum(_dg(tri, t) for t in _bf16_terms(logw, 3))
            cum_end = cum[c - 1:c, :]
            k = k_ref[i, rows, :]
            b = b_ref[i, rows, :]
            e_neg = jnp.exp(-cum)
            rt = r_ref[i, rows, :] * jnp.exp(cum)
            at = kk_ref[i, rows, :] * jnp.exp(cum - logw)
            bt = b * e_neg
            kt = k * e_neg
            d_end = jnp.exp(cum_end - cum)
            bh = b * d_end
            kh = k * d_end
            p_end = jnp.exp(cum_end)
            v = v_ref[i, rows, :]
            for hd in range(N_HEADS):
                sl = slice(hd * HEAD_SIZE, (hd + 1) * HEAD_SIZE)
                xl.append(jnp.concatenate([at[:, sl], rt[:, sl]], axis=0))
                xr.append(jnp.concatenate([bt[:, sl], kt[:, sl]], axis=0))
                xe.append(jnp.concatenate([bh[:, sl], kh[:, sl]], axis=0).astype(BF16))
                vh.append(v[:, sl])
                pe.append(p_end[:, sl])
    n = len(xl)

    gram = [_dot1(xl[j], xr[j], _NT) for j in range(n)]
    tinv = _unit_lower_inverse([jnp.where(strict, g[:c, :c], 0.0) for g in gram], c)
    yk = [_dot1(jnp.where(strict, gram[j][:c, c:], 0.0), vh[j]) for j in range(n)]
    wu = [_dot1(tinv[j], jnp.concatenate([xl[j][:c], -yk[j]], axis=1)) for j in range(n)]
    w_b = [wu[j][:, :HEAD_SIZE].astype(BF16) for j in range(n)]
    lg = [jnp.concatenate([w_b[j], xl[j][c:].astype(BF16)], axis=0) for j in range(n)]
    u0 = [wu[j][:, HEAD_SIZE:] for j in range(n)]
    u0t = [u.T for u in u0]
    vt = [x.T.astype(BF16) for x in vh]
    vb = [x.astype(BF16) for x in vh]
    m_r = [jnp.where(incl2, g[c:, :], 0.0).astype(BF16) for g in gram]

    nh = nb * N_HEADS
    st = [s_scr[jj // N_HEADS, jj % N_HEADS] for jj in range(nh)]
    gain = gain_ref[...]
    bias = bias_ref[...]
    for q in range(nc):
        rows = slice(q * c, (q + 1) * c)
        js = [q * nh + jj for jj in range(nh)]
        sb = [s.astype(BF16) for s in st]
        ut = [u0t[j] - _dg(sb[jj], w_b[j], _NT) for jj, j in enumerate(js)]
        gs = [_dg(lg[j], sb[jj], _NT) for jj, j in enumerate(js)]
        st = [st[jj] * pe[j]
              + _dg(jnp.concatenate([ut[jj].astype(BF16), vt[j]], axis=1), xe[j])
              for jj, j in enumerate(js)]
        uv = [jnp.concatenate([(u0[j] - gs[jj][:c]).astype(BF16), vb[j]], axis=0)
              for jj, j in enumerate(js)]
        o = [gs[jj][c:] + _dg(m_r[j], uv[jj]) for jj, j in enumerate(js)]
        mu = [jnp.mean(x, axis=-1, keepdims=True) for x in o]
        dev = [x - m for x, m in zip(o, mu)]
        var = [jnp.mean(jnp.square(d), axis=-1, keepdims=True) for d in dev]
        for jj in range(nh):
            i, hd = jj // N_HEADS, jj % N_HEADS
            sl = slice(hd * HEAD_SIZE, (hd + 1) * HEAD_SIZE)
            on = dev[jj] * lax.rsqrt(var[jj] + GN_EPS)
            y_ref[i, rows, sl] = ((on * gain[:, sl] + bias[:, sl]) * g_ref[i, rows, sl]
                                  + bg_ref[i, rows, sl])

    for jj in range(nh):
        s_scr[jj // N_HEADS, jj % N_HEADS] = st[jj]
        sout_ref[jj // N_HEADS, jj % N_HEADS] = st[jj]


def _wkv_call(seqs, s0, gain, bias, *, nb, c, nc):
    B, T, _ = seqs[0].shape
    tok = pl.BlockSpec((nb, nc * c, RWKV_DIM), lambda b, t: (b, t, 0))
    st = pl.BlockSpec((nb, N_HEADS, HEAD_SIZE, HEAD_SIZE), lambda b, t: (b, 0, 0, 0))
    vec = pl.BlockSpec((1, RWKV_DIM), lambda b, t: (0, 0))
    return pl.pallas_call(
        functools.partial(_wkv_kernel, nb=nb, c=c, nc=nc),
        grid=(B // nb, T // (nc * c)),
        in_specs=[tok] * 8 + [st, vec, vec],
        out_specs=[tok, st],
        out_shape=[jax.ShapeDtypeStruct((B, T, RWKV_DIM), F32),
                   jax.ShapeDtypeStruct((B, N_HEADS, HEAD_SIZE, HEAD_SIZE), F32)],
        scratch_shapes=[pltpu.VMEM((nb, N_HEADS, HEAD_SIZE, HEAD_SIZE), F32)],
        compiler_params=pltpu.CompilerParams(
            dimension_semantics=("arbitrary", "arbitrary"), vmem_limit_bytes=VMEM_LIMIT),
        name="wkv",
    )(*seqs, s0, gain, bias)


def _out_kernel(x_ref, yc_ref, yr_ref, wout_ref, gpost_ref, gfpre_ref, gfpost_ref,
                wff1_ref, wff2_ref, o_ref):
    wout = wout_ref[...]
    mix = (_dg(yc_ref[...].astype(BF16), wout[:CONV_DIM])
           + _dg(yr_ref[...].astype(BF16), wout[CONV_DIM:]))
    x1 = x_ref[...] + _rmsnorm(mix, gpost_ref[...])
    h2 = _rmsnorm(x1, gfpre_ref[...]).astype(BF16)
    f1 = _dg(h2, wff1_ref[...])
    f1 = jnp.square(jnp.maximum(f1, 0.0)).astype(BF16)
    f2 = _dg(f1, wff2_ref[...])
    o_ref[...] = x1 + _rmsnorm(f2, gfpost_ref[...])


def _out_call(x, yc, yr, params, *, tm):
    n = x.shape[0]
    tok = lambda w: pl.BlockSpec((tm, w), lambda i: (i, 0))
    full = lambda arr: pl.BlockSpec(arr.shape, lambda i: (0,) * arr.ndim,
                                    pipeline_mode=pl.Buffered(1))
    return pl.pallas_call(
        _out_kernel,
        grid=(n // tm,),
        in_specs=[tok(D_MODEL), tok(CONV_DIM), tok(RWKV_DIM)] + [full(p) for p in params],
        out_specs=tok(D_MODEL),
        out_shape=jax.ShapeDtypeStruct((n, D_MODEL), F32),
        compiler_params=pltpu.CompilerParams(
            dimension_semantics=("arbitrary",), vmem_limit_bytes=VMEM_LIMIT),
        name="outffn",
    )(x, yc, yr, *params)


def _tiles(B, T):
    if T >= 256:
        return 2, 256, 2, 2, 64, 2, 512
    return B, T, 1, 4, T, 1, B * T


def _layer(x, conv_buf, shift_prev, wkv_state, proj_params, gn, out_params):
    B, T, _ = x.shape
    pnb, ptt, pns, wnb, wc, wnc, tm = _tiles(B, T)
    outs = _proj_call(x, conv_buf, shift_prev.reshape(B, 1, RWKV_PROJ), proj_params,
                      nb=pnb, tt=ptt, ns=pns)
    yconv, seqs, new_conv, new_shift = outs[0], outs[1:9], outs[9], outs[10]
    yr, new_wkv = _wkv_call(seqs, wkv_state, *gn, nb=wnb, c=wc, nc=wnc)
    y = _out_call(x.reshape(B * T, D_MODEL), yconv.reshape(B * T, CONV_DIM),
                  yr.reshape(B * T, RWKV_DIM), out_params, tm=tm)
    return y.reshape(B, T, D_MODEL), new_conv, new_shift.reshape(B, RWKV_PROJ), new_wkv


def kernel(x_prompt, x_sample, state_conv, state_shift, state_wkv, norm_mix_pre, norm_mix_post,
           norm_ffn_pre, norm_ffn_post, w_in, conv_w, shift_mu, w_decay2, decay_w0, w_a2, a0,
           w_g2, k_k, k_a, r_k, gn_gain, gn_bias, w_out, w_ff1, w_ff2):
    depth = w_in.shape[0]
    Bp = x_prompt.shape[0]
    xp, xs = x_prompt, x_sample
    hid = jnp.arange(RWKV_DIM) // HEAD_SIZE
    ones_bd = (hid[:, None] == hid[None, :]).astype(BF16)
    row = lambda t: t.reshape(1, -1).astype(F32)
    zpad = jnp.zeros((DECAY_LORA, RWKV_DIM), F32)
    res = [[] for _ in range(6)]
    for l in range(depth):
        proj_params = (
            row(norm_mix_pre[l]), w_in[l].astype(BF16), conv_w[l], row(shift_mu[l]),
            jnp.concatenate([w_decay2[l], zpad], axis=0), row(decay_w0[l]),
            jnp.concatenate([zpad, w_a2[l]], axis=0), row(a0[l]), w_g2[l],
            row(k_k[l]), row(k_a[l]), row(r_k[l]), ones_bd)
        gn = (row(gn_gain[l]), row(gn_bias[l]))
        out_params = (w_out[l].astype(BF16), row(norm_mix_post[l]), row(norm_ffn_pre[l]),
                      row(norm_ffn_post[l]), w_ff1[l].astype(BF16), w_ff2[l].astype(BF16))
        xp, c_p, s_p, w_p = _layer(
            xp, jnp.zeros((Bp, CONV_K - 1, CONV_DIM), F32), jnp.zeros((Bp, RWKV_PROJ), F32),
            jnp.zeros((Bp, N_HEADS, HEAD_SIZE, HEAD_SIZE), F32), proj_params, gn, out_params)
        xs, c_s, s_s, w_s = _layer(xs, state_conv[l], state_shift[l], state_wkv[l],
                                   proj_params, gn, out_params)
        for lst, val in zip(res, (c_p, s_p, w_p, c_s, s_s, w_s)):
            lst.append(val)
    return (xp, xs) + tuple(jnp.stack(r) for r in res)
```

```python
import functools

import jax
import jax.numpy as jnp
import numpy as np
from jax import lax
from jax.experimental import pallas as pl
from jax.experimental.pallas import tpu as pltpu

D_MODEL = 1024
CONV_DIM = 512
RWKV_DIM = 512
HEAD_SIZE = 64
N_HEADS = 8
CONV_K = 3
DECAY_LORA = 64
AAA_LORA = 64
GATE_LORA = 128
RWKV_PROJ = 3 * RWKV_DIM + DECAY_LORA + AAA_LORA + GATE_LORA
PROJ_DIM = 3 * CONV_DIM + RWKV_PROJ
D_FF = 4 * D_MODEL
NORM_EPS = 1e-6
GN_EPS = 64e-5
DECAY_SCALE = float(np.exp(-0.5))

SUBLANES = 8
INV_BASE = 16
VMEM_LIMIT = 56 * 1024 * 1024

F32 = jnp.float32
BF16 = jnp.bfloat16

_NN = (((1,), (0,)), ((), ()))
_NT = (((1,), (1,)), ((), ()))


def _dg(a, b, dims=_NN):
    return lax.dot_general(a, b, dims, preferred_element_type=F32)


def _dot1(a, b, dims=_NN):
    return _dg(a.astype(BF16), b.astype(BF16), dims)


def _bf16_terms(a, n):
    terms = []
    for _ in range(n - 1):
        t = a.astype(BF16)
        terms.append(t)
        a = a - t.astype(F32)
    terms.append(a.astype(BF16))
    return terms


def _sigmoid(x):
    return 0.5 * jnp.tanh(0.5 * x) + 0.5


def _rmsnorm(x, g):
    return x * lax.rsqrt(jnp.mean(x * x, axis=-1, keepdims=True) + NORM_EPS) * g


def _proj_stream(sq, tt, x_ref, gpre_ref, win_ref, convw_ref, mu_ref, wd_ref, w0_ref, wa_ref,
                 a0_ref, wg_ref, kk_ref, ka_ref, rk_ref, ones_ref, yconv_ref, r_ref, k_ref,
                 v_ref, kkn_ref, b_ref, lw_ref, g_ref, bg_ref, nconv_ref, nshift_ref, ubuf, zbuf):
    nbs = sq.stop - sq.start
    m = nbs * tt
    h0 = SUBLANES

    x = x_ref[sq].reshape(m, D_MODEL)
    h = _rmsnorm(x, gpre_ref[...]).astype(BF16)
    z = _dg(h, win_ref[...])
    zb = z[:, :CONV_DIM]
    zc = z[:, CONV_DIM:2 * CONV_DIM]
    zh = z[:, 2 * CONV_DIM:3 * CONV_DIM]
    zr = z[:, 3 * CONV_DIM:]
    yield

    ubuf[sq, h0:, :] = (zc * zh).reshape(nbs, tt, CONV_DIM)
    cw = convw_ref[...]
    yc = (ubuf[sq, h0 - 2:h0 - 2 + tt, :] * cw[0:1, :]
          + ubuf[sq, h0 - 1:h0 - 1 + tt, :] * cw[1:2, :]
          + ubuf[sq, h0:h0 + tt, :] * cw[2:3, :])
    yconv_ref[sq] = (zb.reshape(nbs, tt, CONV_DIM) * yc)
    last_u = ubuf[sq, h0 + tt - 2:h0 + tt, :]
    nconv_ref[sq] = last_u
    ubuf[sq, h0 - 2:h0, :] = last_u

    zbuf[sq, h0:, :] = zr.reshape(nbs, tt, RWKV_PROJ)
    zprev = zbuf[sq, h0 - 1:h0 - 1 + tt, :].reshape(m, RWKV_PROJ)
    last_z = zbuf[sq, h0 + tt - 1:h0 + tt, :]
    nshift_ref[sq] = last_z
    zbuf[sq, h0 - 1:h0, :] = last_z
    zs = zr + mu_ref[...] * (zprev - zr)

    r = zs[:, :RWKV_DIM]
    k = zs[:, RWKV_DIM:2 * RWKV_DIM]
    v = zs[:, 2 * RWKV_DIM:3 * RWKV_DIM]
    lwa = zs[:, 3 * RWKV_DIM:3 * RWKV_DIM + DECAY_LORA + AAA_LORA]
    lg = zs[:, 3 * RWKV_DIM + DECAY_LORA + AAA_LORA:]
    yield

    logit_w = w0_ref[...] + _dot1(jnp.tanh(lwa), wd_ref[...])
    logw = -DECAY_SCALE * _sigmoid(logit_w)
    a = _sigmoid(a0_ref[...] + _dot1(lwa, wa_ref[...]))
    g = _dot1(_sigmoid(lg), wg_ref[...])
    yield

    ones_bd = ones_ref[...]
    kk = k * kk_ref[...]
    ss = _dg((kk * kk).astype(BF16), ones_bd)
    kp = k * (1.0 + (a - 1.0) * ka_ref[...])
    bsum = sum(_dg(t, ones_bd) for t in _bf16_terms(r * kp * rk_ref[...], 2))
    yield

    kk = kk * lax.rsqrt(jnp.maximum(ss, 1e-24))
    shp = (nbs, tt, RWKV_DIM)
    r_ref[sq] = r.reshape(shp)
    k_ref[sq] = kp.reshape(shp)
    v_ref[sq] = v.reshape(shp)
    kkn_ref[sq] = kk.reshape(shp)
    b_ref[sq] = (kk * a).reshape(shp)
    lw_ref[sq] = logw.reshape(shp)
    g_ref[sq] = g.reshape(shp)
    bg_ref[sq] = (bsum * v * g).reshape(shp)


def _proj_kernel(x_ref, cbuf_ref, sprev_ref, *rest, nb, tt, ns):
    ubuf, zbuf = rest[-2:]

    @pl.when(pl.program_id(1) == 0)
    def _():
        ubuf[:, SUBLANES - 2:SUBLANES, :] = cbuf_ref[...]
        zbuf[:, SUBLANES - 1:SUBLANES, :] = sprev_ref[...]

    per = nb // ns
    streams = [_proj_stream(slice(i * per, (i + 1) * per), tt, x_ref, *rest) for i in range(ns)]
    while streams:
        streams = [s for s in streams if next(s, True) is None]


def _proj_call(x, conv_buf, shift_prev, params, *, nb, tt, ns):
    B, T, _ = x.shape
    grid = (B // nb, T // tt)
    tok = lambda w: pl.BlockSpec((nb, tt, w), lambda b, t: (b, t, 0))
    full = lambda arr: pl.BlockSpec(arr.shape, lambda b, t: (0,) * arr.ndim)
    in_specs = [tok(D_MODEL),
                pl.BlockSpec((nb, CONV_K - 1, CONV_DIM), lambda b, t: (b, 0, 0)),
                pl.BlockSpec((nb, 1, RWKV_PROJ), lambda b, t: (b, 0, 0))]
    in_specs += [full(p) for p in params]
    seq = lambda w: jax.ShapeDtypeStruct((B, T, w), F32)
    out_shape = [seq(CONV_DIM)] + [seq(RWKV_DIM)] * 8 + [
        jax.ShapeDtypeStruct((B, CONV_K - 1, CONV_DIM), F32),
        jax.ShapeDtypeStruct((B, 1, RWKV_PROJ), F32)]
    out_specs = [tok(CONV_DIM)] + [tok(RWKV_DIM)] * 8 + [
        pl.BlockSpec((nb, CONV_K - 1, CONV_DIM), lambda b, t: (b, 0, 0)),
        pl.BlockSpec((nb, 1, RWKV_PROJ), lambda b, t: (b, 0, 0))]
    return pl.pallas_call(
        functools.partial(_proj_kernel, nb=nb, tt=tt, ns=ns),
        grid=grid, in_specs=in_specs, out_specs=out_specs, out_shape=out_shape,
        scratch_shapes=[pltpu.VMEM((nb, tt + SUBLANES, CONV_DIM), F32),
                        pltpu.VMEM((nb, tt + SUBLANES, RWKV_PROJ), F32)],
        compiler_params=pltpu.CompilerParams(
            dimension_semantics=("arbitrary", "arbitrary"), vmem_limit_bytes=VMEM_LIMIT),
        name="proj",
    )(x, conv_buf, shift_prev, *params)


def _unit_lower_inverse(lows, c):
    row = lax.broadcasted_iota(jnp.int32, (c, c), 0)
    col = lax.broadcasted_iota(jnp.int32, (c, c), 1)
    eye = (row == col).astype(F32)
    base = min(INV_BASE, c)
    same = (row // base) == (col // base)
    ps = [jnp.where(same, low, 0.0) for low in lows]
    invs = [eye - p for p in ps]
    span = 2
    while span < base:
        ps = [_dot1(p, p) for p in ps]
        invs = [inv + _dot1(inv, p) for inv, p in zip(invs, ps)]
        span *= 2
    blk = base
    while blk < c:
        sel = ((row // (2 * blk)) == (col // (2 * blk))) & ((row // blk) != (col // blk))
        tmp = [_dot1(jnp.where(sel, low, 0.0), inv) for low, inv in zip(lows, invs)]
        invs = [inv - _dot1(inv, t) for inv, t in zip(invs, tmp)]
        blk *= 2
    return invs


def _wkv_kernel(r_ref, k_ref, v_ref, kk_ref, b_ref, lw_ref, g_ref, bg_ref, s0_ref,
                gain_ref, bias_ref, y_ref, sout_ref, s_scr, *, nb, c, nc):
    pair = 2 * HEAD_SIZE
    zpad = jnp.zeros((HEAD_SIZE, HEAD_SIZE), F32)

    @pl.when(pl.program_id(1) == 0)
    def _():
        for i in range(nb):
            for hd in range(N_HEADS):
                s0 = s0_ref[i, hd]
                s_scr[i, hd] = jnp.concatenate([s0, zpad] if hd % 2 == 0 else [zpad, s0], axis=1)

    row = lax.broadcasted_iota(jnp.int32, (c, c), 0)
    col = lax.broadcasted_iota(jnp.int32, (c, c), 1)
    strict = row > col
    incl = row >= col
    row2 = lax.broadcasted_iota(jnp.int32, (c, 2 * c), 0)
    col2 = lax.broadcasted_iota(jnp.int32, (c, 2 * c), 1)
    incl2 = row2 >= jnp.where(col2 >= c, col2 - c, col2)
    strict_k = (col2 >= c) & (row2 > col2 - c)
    tri = incl.astype(BF16)
    lane_half = lax.broadcasted_iota(jnp.int32, (c, pair), 1) // HEAD_SIZE
    own = [lane_half == 0, lane_half == 1]
    zrows = jnp.zeros((c, HEAD_SIZE), F32)

    xl, xr, xe, vh, pe = [], [], [], [], []
    for q in range(nc):
        rows = slice(q * c, (q + 1) * c)
        for i in range(nb):
            logw = lw_ref[i, rows, :]
            cum = sum(_dg(tri, t) for t in _bf16_terms(logw, 3))
            cum_end = cum[c - 1:c, :]
            k = k_ref[i, rows, :]
            b = b_ref[i, rows, :]
            e_neg = jnp.exp(-cum)
            rt = r_ref[i, rows, :] * jnp.exp(cum)
            at = kk_ref[i, rows, :] * jnp.exp(cum - logw)
            bt = b * e_neg
            kt = k * e_neg
            d_end = jnp.exp(cum_end - cum)
            bh = b * d_end
            kh = k * d_end
            p_end = jnp.exp(cum_end)
            v = v_ref[i, rows, :]
            for hd in range(N_HEADS):
                lp = slice((hd // 2) * pair, (hd // 2 + 1) * pair)
                m = own[hd % 2]
                xl.append(jnp.concatenate([jnp.where(m, at[:, lp], 0.0),
                                           jnp.where(m, rt[:, lp], 0.0)], axis=0))
                xr.append(jnp.concatenate([bt[:, lp], kt[:, lp]], axis=0))
                xe.append(jnp.concatenate([jnp.where(m, bh[:, lp], 0.0),
                                           jnp.where(m, kh[:, lp], 0.0)], axis=0).astype(BF16))
                vh.append(v[:, hd * HEAD_SIZE:(hd + 1) * HEAD_SIZE])
                pe.append(p_end[:, lp])
    n = len(xl)

    gram = [_dot1(xl[j], xr[j], _NT) for j in range(n)]
    tinv = _unit_lower_inverse([jnp.where(strict, g[:c, :c], 0.0) for g in gram], c)
    yk = [_dot1(jnp.where(strict_k, gram[j][:c], 0.0), jnp.concatenate([zrows, vh[j]], axis=0))
          for j in range(n)]
    wu = [_dot1(tinv[j], jnp.concatenate([xl[j][:c], -yk[j]], axis=1)) for j in range(n)]
    w_b = [wu[j][:, :pair].astype(BF16) for j in range(n)]
    lg = [jnp.concatenate([w_b[j], xl[j][c:].astype(BF16)], axis=0) for j in range(n)]
    u0 = [wu[j][:, pair:] for j in range(n)]
    u0t = [u.T for u in u0]
    vt = [x.T.astype(BF16) for x in vh]
    vb = [x.astype(BF16) for x in vh]
    m_r = [jnp.where(incl2, g[c:, :], 0.0).astype(BF16) for g in gram]

    nh = nb * N_HEADS
    st = [s_scr[jj // N_HEADS, jj % N_HEADS] for jj in range(nh)]
    gain = gain_ref[...]
    bias = bias_ref[...]
    for q in range(nc):
        rows = slice(q * c, (q + 1) * c)
        js = [q * nh + jj for jj in range(nh)]
        sb = [s.astype(BF16) for s in st]
        ut = [u0t[j] - _dg(sb[jj], w_b[j], _NT) for jj, j in enumerate(js)]
        gs = [_dg(lg[j], sb[jj], _NT) for jj, j in enumerate(js)]
        st = [st[jj] * pe[j]
              + _dg(jnp.concatenate([ut[jj].astype(BF16), vt[j]], axis=1), xe[j])
              for jj, j in enumerate(js)]
        uv = [jnp.concatenate([(u0[j] - gs[jj][:c]).astype(BF16), vb[j]], axis=0)
              for jj, j in enumerate(js)]
        o = [gs[jj][c:] + _dg(m_r[j], uv[jj]) for jj, j in enumerate(js)]
        mu = [jnp.mean(x, axis=-1, keepdims=True) for x in o]
        dev = [x - m for x, m in zip(o, mu)]
        var = [jnp.mean(jnp.square(d), axis=-1, keepdims=True) for d in dev]
        for jj in range(nh):
            i, hd = jj // N_HEADS, jj % N_HEADS
            sl = slice(hd * HEAD_SIZE, (hd + 1) * HEAD_SIZE)
            on = dev[jj] * lax.rsqrt(var[jj] + GN_EPS)
            y_ref[i, rows, sl] = ((on * gain[:, sl] + bias[:, sl]) * g_ref[i, rows, sl]
                                  + bg_ref[i, rows, sl])

    for jj in range(nh):
        s_scr[jj // N_HEADS, jj % N_HEADS] = st[jj]

    @pl.when(pl.program_id(1) == pl.num_programs(1) - 1)
    def _():
        for jj in range(nh):
            hd = jj % N_HEADS
            sout_ref[jj // N_HEADS, hd] = st[jj][:, (hd % 2) * HEAD_SIZE:(hd % 2 + 1) * HEAD_SIZE]


def _wkv_call(seqs, s0, gain, bias, *, nb, c, nc):
    B, T, _ = seqs[0].shape
    tok = pl.BlockSpec((nb, nc * c, RWKV_DIM), lambda b, t: (b, t, 0))
    st = pl.BlockSpec((nb, N_HEADS, HEAD_SIZE, HEAD_SIZE), lambda b, t: (b, 0, 0, 0))
    vec = pl.BlockSpec((1, RWKV_DIM), lambda b, t: (0, 0))
    return pl.pallas_call(
        functools.partial(_wkv_kernel, nb=nb, c=c, nc=nc),
        grid=(B // nb, T // (nc * c)),
        in_specs=[tok] * 8 + [st, vec, vec],
        out_specs=[tok, st],
        out_shape=[jax.ShapeDtypeStruct((B, T, RWKV_DIM), F32),
                   jax.ShapeDtypeStruct((B, N_HEADS, HEAD_SIZE, HEAD_SIZE), F32)],
        scratch_shapes=[pltpu.VMEM((nb, N_HEADS, HEAD_SIZE, 2 * HEAD_SIZE), F32)],
        compiler_params=pltpu.CompilerParams(
            dimension_semantics=("arbitrary", "arbitrary"), vmem_limit_bytes=VMEM_LIMIT),
        name="wkv",
    )(*seqs, s0, gain, bias)


def _out_kernel(x_ref, yc_ref, yr_ref, wout_ref, gpost_ref, gfpre_ref, gfpost_ref,
                wff1_ref, wff2_ref, o_ref):
    wout = wout_ref[...]
    mix = (_dg(yc_ref[...].astype(BF16), wout[:CONV_DIM])
           + _dg(yr_ref[...].astype(BF16), wout[CONV_DIM:]))
    x1 = x_ref[...] + _rmsnorm(mix, gpost_ref[...])
    h2 = _rmsnorm(x1, gfpre_ref[...]).astype(BF16)
    f1 = _dg(h2, wff1_ref[...])
    f1 = jnp.square(jnp.maximum(f1, 0.0)).astype(BF16)
    f2 = _dg(f1, wff2_ref[...])
    o_ref[...] = x1 + _rmsnorm(f2, gfpost_ref[...])


def _out_call(x, yc, yr, params, *, tm):
    n = x.shape[0]
    tok = lambda w: pl.BlockSpec((tm, w), lambda i: (i, 0))
    full = lambda arr: pl.BlockSpec(arr.shape, lambda i: (0,) * arr.ndim,
                                    pipeline_mode=pl.Buffered(1))
    return pl.pallas_call(
        _out_kernel,
        grid=(n // tm,),
        in_specs=[tok(D_MODEL), tok(CONV_DIM), tok(RWKV_DIM)] + [full(p) for p in params],
        out_specs=tok(D_MODEL),
        out_shape=jax.ShapeDtypeStruct((n, D_MODEL), F32),
        compiler_params=pltpu.CompilerParams(
            dimension_semantics=("arbitrary",), vmem_limit_bytes=VMEM_LIMIT),
        name="outffn",
    )(x, yc, yr, *params)


def _tiles(B, T):
    if T >= 256:
        return 2, 256, 2, 2, 64, 2, 512
    return B, T, 1, 4, T, 1, B * T


def _layer(x, conv_buf, shift_prev, wkv_state, proj_params, gn, out_params):
    B, T, _ = x.shape
    pnb, ptt, pns, wnb, wc, wnc, tm = _tiles(B, T)
    outs = _proj_call(x, conv_buf, shift_prev.reshape(B, 1, RWKV_PROJ), proj_params,
                      nb=pnb, tt=ptt, ns=pns)
    yconv, seqs, new_conv, new_shift = outs[0], outs[1:9], outs[9], outs[10]
    yr, new_wkv = _wkv_call(seqs, wkv_state, *gn, nb=wnb, c=wc, nc=wnc)
    y = _out_call(x.reshape(B * T, D_MODEL), yconv.reshape(B * T, CONV_DIM),
                  yr.reshape(B * T, RWKV_DIM), out_params, tm=tm)
    return y.reshape(B, T, D_MODEL), new_conv, new_shift.reshape(B, RWKV_PROJ), new_wkv


def kernel(x_prompt, x_sample, state_conv, state_shift, state_wkv, norm_mix_pre, norm_mix_post,
           norm_ffn_pre, norm_ffn_post, w_in, conv_w, shift_mu, w_decay2, decay_w0, w_a2, a0,
           w_g2, k_k, k_a, r_k, gn_gain, gn_bias, w_out, w_ff1, w_ff2):
    depth = w_in.shape[0]
    Bp = x_prompt.shape[0]
    xp, xs = x_prompt, x_sample
    hid = jnp.arange(RWKV_DIM) // HEAD_SIZE
    ones_bd = (hid[:, None] == hid[None, :]).astype(BF16)
    row = lambda t: t.reshape(1, -1).astype(F32)
    zpad = jnp.zeros((DECAY_LORA, RWKV_DIM), F32)
    res = [[] for _ in range(6)]
    for l in range(depth):
        proj_params = (
            row(norm_mix_pre[l]), w_in[l].astype(BF16), conv_w[l], row(shift_mu[l]),
            jnp.concatenate([w_decay2[l], zpad], axis=0), row(decay_w0[l]),
            jnp.concatenate([zpad, w_a2[l]], axis=0), row(a0[l]), w_g2[l],
            row(k_k[l]), row(k_a[l]), row(r_k[l]), ones_bd)
        gn = (row(gn_gain[l]), row(gn_bias[l]))
        out_params = (w_out[l].astype(BF16), row(norm_mix_post[l]), row(norm_ffn_pre[l]),
                      row(norm_ffn_post[l]), w_ff1[l].astype(BF16), w_ff2[l].astype(BF16))
        xp, c_p, s_p, w_p = _layer(
            xp, jnp.zeros((Bp, CONV_K - 1, CONV_DIM), F32), jnp.zeros((Bp, RWKV_PROJ), F32),
            jnp.zeros((Bp, N_HEADS, HEAD_SIZE, HEAD_SIZE), F32), proj_params, gn, out_params)
        xs, c_s, s_s, w_s = _layer(xs, state_conv[l], state_shift[l], state_wkv[l],
                                   proj_params, gn, out_params)
        for lst, val in zip(res, (c_p, s_p, w_p, c_s, s_s, w_s)):
            lst.append(val)
    return (xp, xs) + tuple(jnp.stack(r) for r in res)
```

```python
import functools

import jax
import jax.numpy as jnp
import numpy as np
from jax import lax
from jax.experimental import pallas as pl
from jax.experimental.pallas import tpu as pltpu

D_MODEL = 1024
CONV_DIM = 512
RWKV_DIM = 512
HEAD_SIZE = 64
N_HEADS = 8
CONV_K = 3
DECAY_LORA = 64
AAA_LORA = 64
GATE_LORA = 128
RWKV_PROJ = 3 * RWKV_DIM + DECAY_LORA + AAA_LORA + GATE_LORA
PROJ_DIM = 3 * CONV_DIM + RWKV_PROJ
D_FF = 4 * D_MODEL
NORM_EPS = 1e-6
GN_EPS = 64e-5
DECAY_SCALE = float(np.exp(-0.5))

SUBLANES = 8
INV_BASE = 16
VMEM_LIMIT = 56 * 1024 * 1024

F32 = jnp.float32
BF16 = jnp.bfloat16

_NN = (((1,), (0,)), ((), ()))
_NT = (((1,), (1,)), ((), ()))


def _dg(a, b, dims=_NN):
    return lax.dot_general(a, b, dims, preferred_element_type=F32)


def _dot1(a, b, dims=_NN):
    return _dg(a.astype(BF16), b.astype(BF16), dims)


def _bf16_terms(a, n):
    terms = []
    for _ in range(n - 1):
        t = a.astype(BF16)
        terms.append(t)
        a = a - t.astype(F32)
    terms.append(a.astype(BF16))
    return terms


def _sigmoid(x):
    return 0.5 * jnp.tanh(0.5 * x) + 0.5


def _rmsnorm(x, g):
    return x * lax.rsqrt(jnp.mean(x * x, axis=-1, keepdims=True) + NORM_EPS) * g


def _proj_stream(sq, tt, x_ref, gpre_ref, win_ref, convw_ref, mu_ref, wd_ref, w0_ref, wa_ref,
                 a0_ref, wg_ref, kk_ref, ka_ref, rk_ref, ones_ref, yconv_ref, r_ref, k_ref,
                 v_ref, kkn_ref, b_ref, lw_ref, g_ref, bg_ref, nconv_ref, nshift_ref, ubuf, zbuf):
    nbs = sq.stop - sq.start
    m = nbs * tt
    h0 = SUBLANES

    x = x_ref[sq].reshape(m, D_MODEL)
    h = _rmsnorm(x, gpre_ref[...]).astype(BF16)
    z = _dg(h, win_ref[...])
    zb = z[:, :CONV_DIM]
    zc = z[:, CONV_DIM:2 * CONV_DIM]
    zh = z[:, 2 * CONV_DIM:3 * CONV_DIM]
    zr = z[:, 3 * CONV_DIM:]
    yield

    ubuf[sq, h0:, :] = (zc * zh).reshape(nbs, tt, CONV_DIM)
    cw = convw_ref[...]
    yc = (ubuf[sq, h0 - 2:h0 - 2 + tt, :] * cw[0:1, :]
          + ubuf[sq, h0 - 1:h0 - 1 + tt, :] * cw[1:2, :]
          + ubuf[sq, h0:h0 + tt, :] * cw[2:3, :])
    yconv_ref[sq] = (zb.reshape(nbs, tt, CONV_DIM) * yc)
    last_u = ubuf[sq, h0 + tt - 2:h0 + tt, :]
    nconv_ref[sq] = last_u
    ubuf[sq, h0 - 2:h0, :] = last_u

    zbuf[sq, h0:, :] = zr.reshape(nbs, tt, RWKV_PROJ)
    zprev = zbuf[sq, h0 - 1:h0 - 1 + tt, :].reshape(m, RWKV_PROJ)
    last_z = zbuf[sq, h0 + tt - 1:h0 + tt, :]
    nshift_ref[sq] = last_z
    zbuf[sq, h0 - 1:h0, :] = last_z
    zs = zr + mu_ref[...] * (zprev - zr)

    r = zs[:, :RWKV_DIM]
    k = zs[:, RWKV_DIM:2 * RWKV_DIM]
    v = zs[:, 2 * RWKV_DIM:3 * RWKV_DIM]
    lwa = zs[:, 3 * RWKV_DIM:3 * RWKV_DIM + DECAY_LORA + AAA_LORA]
    lg = zs[:, 3 * RWKV_DIM + DECAY_LORA + AAA_LORA:]
    yield

    logit_w = w0_ref[...] + _dot1(jnp.tanh(lwa), wd_ref[...])
    logw = -DECAY_SCALE * _sigmoid(logit_w)
    a = _sigmoid(a0_ref[...] + _dot1(lwa, wa_ref[...]))
    g = _dot1(_sigmoid(lg), wg_ref[...])
    yield

    ones_bd = ones_ref[...]
    kk = k * kk_ref[...]
    ss = _dg((kk * kk).astype(BF16), ones_bd)
    kp = k * (1.0 + (a - 1.0) * ka_ref[...])
    bsum = sum(_dg(t, ones_bd) for t in _bf16_terms(r * kp * rk_ref[...], 2))
    yield

    kk = kk * lax.rsqrt(jnp.maximum(ss, 1e-24))
    shp = (nbs, tt, RWKV_DIM)
    r_ref[sq] = r.reshape(shp)
    k_ref[sq] = kp.reshape(shp)
    v_ref[sq] = v.reshape(shp)
    kkn_ref[sq] = kk.reshape(shp)
    b_ref[sq] = (kk * a).reshape(shp)
    lw_ref[sq] = logw.reshape(shp)
    g_ref[sq] = g.reshape(shp)
    bg_ref[sq] = (bsum * v * g).reshape(shp)


def _proj_kernel(x_ref, cbuf_ref, sprev_ref, *rest, nb, tt, ns):
    ubuf, zbuf = rest[-2:]

    @pl.when(pl.program_id(1) == 0)
    def _():
        ubuf[:, SUBLANES - 2:SUBLANES, :] = cbuf_ref[...]
        zbuf[:, SUBLANES - 1:SUBLANES, :] = sprev_ref[...]

    per = nb // ns
    streams = [_proj_stream(slice(i * per, (i + 1) * per), tt, x_ref, *rest) for i in range(ns)]
    while streams:
        streams = [s for s in streams if next(s, True) is None]


def _proj_call(x, conv_buf, shift_prev, params, *, nb, tt, ns):
    B, T, _ = x.shape
    grid = (B // nb, T // tt)
    tok = lambda w: pl.BlockSpec((nb, tt, w), lambda b, t: (b, t, 0))
    full = lambda arr: pl.BlockSpec(arr.shape, lambda b, t: (0,) * arr.ndim)
    in_specs = [tok(D_MODEL),
                pl.BlockSpec((nb, CONV_K - 1, CONV_DIM), lambda b, t: (b, 0, 0)),
                pl.BlockSpec((nb, 1, RWKV_PROJ), lambda b, t: (b, 0, 0))]
    in_specs += [full(p) for p in params]
    seq = lambda w: jax.ShapeDtypeStruct((B, T, w), F32)
    out_shape = [seq(CONV_DIM)] + [seq(RWKV_DIM)] * 8 + [
        jax.ShapeDtypeStruct((B, CONV_K - 1, CONV_DIM), F32),
        jax.ShapeDtypeStruct((B, 1, RWKV_PROJ), F32)]
    out_specs = [tok(CONV_DIM)] + [tok(RWKV_DIM)] * 8 + [
        pl.BlockSpec((nb, CONV_K - 1, CONV_DIM), lambda b, t: (b, 0, 0)),
        pl.BlockSpec((nb, 1, RWKV_PROJ), lambda b, t: (b, 0, 0))]
    return pl.pallas_call(
        functools.partial(_proj_kernel, nb=nb, tt=tt, ns=ns),
        grid=grid, in_specs=in_specs, out_specs=out_specs, out_shape=out_shape,
        scratch_shapes=[pltpu.VMEM((nb, tt + SUBLANES, CONV_DIM), F32),
                        pltpu.VMEM((nb, tt + SUBLANES, RWKV_PROJ), F32)],
        compiler_params=pltpu.CompilerParams(
            dimension_semantics=("arbitrary", "arbitrary"), vmem_limit_bytes=VMEM_LIMIT),
        name="proj",
    )(x, conv_buf, shift_prev, *params)


def _unit_lower_inverse(lows, c):
    row = lax.broadcasted_iota(jnp.int32, (c, c), 0)
    col = lax.broadcasted_iota(jnp.int32, (c, c), 1)
    eye = (row == col).astype(F32)
    base = min(INV_BASE, c)
    same = (row // base) == (col // base)
    ps = [jnp.where(same, low, 0.0) for low in lows]
    invs = [eye - p for p in ps]
    span = 2
    while span < base:
        ps = [_dot1(p, p) for p in ps]
        invs = [inv + _dot1(inv, p) for inv, p in zip(invs, ps)]
        span *= 2
    blk = base
    while blk < c:
        sel = ((row // (2 * blk)) == (col // (2 * blk))) & ((row // blk) != (col // blk))
        tmp = [_dot1(jnp.where(sel, low, 0.0), inv) for low, inv in zip(lows, invs)]
        invs = [inv - _dot1(inv, t) for inv, t in zip(invs, tmp)]
        blk *= 2
    return invs


def _wkv_kernel(r_ref, k_ref, v_ref, kk_ref, b_ref, lw_ref, g_ref, bg_ref, s0_ref,
                gain_ref, bias_ref, y_ref, sout_ref, s_scr, *, nb, c, nc):
    pair = 2 * HEAD_SIZE
    zpad = jnp.zeros((HEAD_SIZE, HEAD_SIZE), F32)

    @pl.when(pl.program_id(1) == 0)
    def _():
        for i in range(nb):
            for hd in range(N_HEADS):
                s0 = s0_ref[i, hd]
                s_scr[i, hd] = jnp.concatenate([s0, zpad] if hd % 2 == 0 else [zpad, s0], axis=1)

    row = lax.broadcasted_iota(jnp.int32, (c, c), 0)
    col = lax.broadcasted_iota(jnp.int32, (c, c), 1)
    strict = row > col
    incl = row >= col
    row2 = lax.broadcasted_iota(jnp.int32, (c, 2 * c), 0)
    col2 = lax.broadcasted_iota(jnp.int32, (c, 2 * c), 1)
    incl2 = row2 >= jnp.where(col2 >= c, col2 - c, col2)
    strict_k = (col2 >= c) & (row2 > col2 - c)
    tri = incl.astype(BF16)
    lane_half = lax.broadcasted_iota(jnp.int32, (c, pair), 1) // HEAD_SIZE
    own = [lane_half == 0, lane_half == 1]
    zrows = jnp.zeros((c, HEAD_SIZE), F32)

    xl, xr, xe, vh, pe = [], [], [], [], []
    for q in range(nc):
        rows = slice(q * c, (q + 1) * c)
        for i in range(nb):
            logw = lw_ref[i, rows, :]
            cum = sum(_dg(tri, t) for t in _bf16_terms(logw, 3))
            cum_end = cum[c - 1:c, :]
            k = k_ref[i, rows, :]
            b = b_ref[i, rows, :]
            e_neg = jnp.exp(-cum)
            rt = r_ref[i, rows, :] * jnp.exp(cum)
            at = kk_ref[i, rows, :] * jnp.exp(cum - logw)
            bt = b * e_neg
            kt = k * e_neg
            d_end = jnp.exp(cum_end - cum)
            bh = b * d_end
            kh = k * d_end
            p_end = jnp.exp(cum_end)
            v = v_ref[i, rows, :]
            for hd in range(N_HEADS):
                lp = slice((hd // 2) * pair, (hd // 2 + 1) * pair)
                m = own[hd % 2]
                xl.append(jnp.concatenate([jnp.where(m, at[:, lp], 0.0),
                                           jnp.where(m, rt[:, lp], 0.0)], axis=0))
                xr.append(jnp.concatenate([bt[:, lp], kt[:, lp]], axis=0))
                xe.append(jnp.concatenate([jnp.where(m, bh[:, lp], 0.0),
                                           jnp.where(m, kh[:, lp], 0.0)], axis=0).astype(BF16))
                vh.append(v[:, hd * HEAD_SIZE:(hd + 1) * HEAD_SIZE])
                pe.append(p_end[:, lp])
    n = len(xl)

    gram = [_dot1(xl[j], xr[j], _NT) for j in range(n)]
    tinv = _unit_lower_inverse([jnp.where(strict, g[:c, :c], 0.0) for g in gram], c)
    yk = [_dot1(jnp.where(strict_k, gram[j][:c], 0.0), jnp.concatenate([zrows, vh[j]], axis=0))
          for j in range(n)]
    wu = [_dot1(tinv[j], jnp.concatenate([xl[j][:c], -yk[j]], axis=1)) for j in range(n)]
    w_b = [wu[j][:, :pair].astype(BF16) for j in range(n)]
    lg = [jnp.concatenate([w_b[j], xl[j][c:].astype(BF16)], axis=0) for j in range(n)]
    u0 = [wu[j][:, pair:] for j in range(n)]
    u0t = [u.T for u in u0]
    vt = [x.T.astype(BF16) for x in vh]
    vb = [x.astype(BF16) for x in vh]
    m_r = [jnp.where(incl2, g[c:, :], 0.0).astype(BF16) for g in gram]

    nh = nb * N_HEADS
    st = [s_scr[jj // N_HEADS, jj % N_HEADS] for jj in range(nh)]
    gain = gain_ref[...]
    bias = bias_ref[...]
    for q in range(nc):
        rows = slice(q * c, (q + 1) * c)
        js = [q * nh + jj for jj in range(nh)]
        sb = [s.astype(BF16) for s in st]
        ut = [u0t[j] - _dg(sb[jj], w_b[j], _NT) for jj, j in enumerate(js)]
        gs = [_dg(lg[j], sb[jj], _NT) for jj, j in enumerate(js)]
        st = [st[jj] * pe[j]
              + _dg(jnp.concatenate([ut[jj].astype(BF16), vt[j]], axis=1), xe[j])
              for jj, j in enumerate(js)]
        uv = [jnp.concatenate([(u0[j] - gs[jj][:c]).astype(BF16), vb[j]], axis=0)
              for jj, j in enumerate(js)]
        o = [gs[jj][c:] + _dg(m_r[j], uv[jj]) for jj, j in enumerate(js)]
        mu = [jnp.mean(x, axis=-1, keepdims=True) for x in o]
        dev = [x - m for x, m in zip(o, mu)]
        var = [jnp.mean(jnp.square(d), axis=-1, keepdims=True) for d in dev]
        on = [d * lax.rsqrt(s + GN_EPS) for d, s in zip(dev, var)]
        for jj in range(0, nh, 2):
            i, hd = jj // N_HEADS, jj % N_HEADS
            lp = slice(hd * HEAD_SIZE, (hd + 2) * HEAD_SIZE)
            on_pair = jnp.concatenate([on[jj], on[jj + 1]], axis=1)
            y_ref[i, rows, lp] = ((on_pair * gain[:, lp] + bias[:, lp]) * g_ref[i, rows, lp]
                                  + bg_ref[i, rows, lp])

    for jj in range(nh):
        s_scr[jj // N_HEADS, jj % N_HEADS] = st[jj]

    @pl.when(pl.program_id(1) == pl.num_programs(1) - 1)
    def _():
        for jj in range(nh):
            hd = jj % N_HEADS
            sout_ref[jj // N_HEADS, hd] = st[jj][:, (hd % 2) * HEAD_SIZE:(hd % 2 + 1) * HEAD_SIZE]


def _wkv_call(seqs, s0, gain, bias, *, nb, c, nc):
    B, T, _ = seqs[0].shape
    tok = pl.BlockSpec((nb, nc * c, RWKV_DIM), lambda b, t: (b, t, 0))
    st = pl.BlockSpec((nb, N_HEADS, HEAD_SIZE, HEAD_SIZE), lambda b, t: (b, 0, 0, 0))
    vec = pl.BlockSpec((1, RWKV_DIM), lambda b, t: (0, 0))
    return pl.pallas_call(
        functools.partial(_wkv_kernel, nb=nb, c=c, nc=nc),
        grid=(B // nb, T // (nc * c)),
        in_specs=[tok] * 8 + [st, vec, vec],
        out_specs=[tok, st],
        out_shape=[jax.ShapeDtypeStruct((B, T, RWKV_DIM), F32),
                   jax.ShapeDtypeStruct((B, N_HEADS, HEAD_SIZE, HEAD_SIZE), F32)],
        scratch_shapes=[pltpu.VMEM((nb, N_HEADS, HEAD_SIZE, 2 * HEAD_SIZE), F32)],
        compiler_params=pltpu.CompilerParams(
            dimension_semantics=("arbitrary", "arbitrary"), vmem_limit_bytes=VMEM_LIMIT),
        name="wkv",
    )(*seqs, s0, gain, bias)


def _out_stream(rows, x_ref, yc_ref, yr_ref, wout_ref, gpost_ref, gfpre_ref, gfpost_ref,
                wff1_ref, wff2_ref, o_ref):
    wout = wout_ref[...]
    mix = (_dg(yc_ref[rows, :].astype(BF16), wout[:CONV_DIM])
           + _dg(yr_ref[rows, :].astype(BF16), wout[CONV_DIM:]))
    yield
    x1 = x_ref[rows, :] + _rmsnorm(mix, gpost_ref[...])
    h2 = _rmsnorm(x1, gfpre_ref[...]).astype(BF16)
    yield
    f1 = _dg(h2, wff1_ref[...])
    yield
    f1 = jnp.square(jnp.maximum(f1, 0.0)).astype(BF16)
    yield
    f2 = _dg(f1, wff2_ref[...])
    yield
    o_ref[rows, :] = x1 + _rmsnorm(f2, gfpost_ref[...])


def _out_kernel(*refs, tm, ns):
    per = tm // ns
    streams = [_out_stream(slice(i * per, (i + 1) * per), *refs) for i in range(ns)]
    while streams:
        streams = [s for s in streams if next(s, True) is None]


def _out_call(x, yc, yr, params, *, tm, ns):
    n = x.shape[0]
    tok = lambda w: pl.BlockSpec((tm, w), lambda i: (i, 0))
    full = lambda arr: pl.BlockSpec(arr.shape, lambda i: (0,) * arr.ndim,
                                    pipeline_mode=pl.Buffered(1))
    return pl.pallas_call(
        functools.partial(_out_kernel, tm=tm, ns=ns),
        grid=(n // tm,),
        in_specs=[tok(D_MODEL), tok(CONV_DIM), tok(RWKV_DIM)] + [full(p) for p in params],
        out_specs=tok(D_MODEL),
        out_shape=jax.ShapeDtypeStruct((n, D_MODEL), F32),
        compiler_params=pltpu.CompilerParams(
            dimension_semantics=("arbitrary",), vmem_limit_bytes=VMEM_LIMIT),
        name="outffn",
    )(x, yc, yr, *params)


def _tiles(B, T):
    if T >= 256:
        return 2, 256, 2, 2, 64, 4, 512, 2
    return B, T, 1, 4, T, 1, B * T, 1


def _layer(x, conv_buf, shift_prev, wkv_state, proj_params, gn, out_params):
    B, T, _ = x.shape
    pnb, ptt, pns, wnb, wc, wnc, tm, ons = _tiles(B, T)
    outs = _proj_call(x, conv_buf, shift_prev.reshape(B, 1, RWKV_PROJ), proj_params,
                      nb=pnb, tt=ptt, ns=pns)
    yconv, seqs, new_conv, new_shift = outs[0], outs[1:9], outs[9], outs[10]
    yr, new_wkv = _wkv_call(seqs, wkv_state, *gn, nb=wnb, c=wc, nc=wnc)
    y = _out_call(x.reshape(B * T, D_MODEL), yconv.reshape(B * T, CONV_DIM),
                  yr.reshape(B * T, RWKV_DIM), out_params, tm=tm, ns=ons)
    return y.reshape(B, T, D_MODEL), new_conv, new_shift.reshape(B, RWKV_PROJ), new_wkv


def kernel(x_prompt, x_sample, state_conv, state_shift, state_wkv, norm_mix_pre, norm_mix_post,
           norm_ffn_pre, norm_ffn_post, w_in, conv_w, shift_mu, w_decay2, decay_w0, w_a2, a0,
           w_g2, k_k, k_a, r_k, gn_gain, gn_bias, w_out, w_ff1, w_ff2):
    depth = w_in.shape[0]
    Bp = x_prompt.shape[0]
    xp, xs = x_prompt, x_sample
    hid = jnp.arange(RWKV_DIM) // HEAD_SIZE
    ones_bd = (hid[:, None] == hid[None, :]).astype(BF16)
    row = lambda t: t.reshape(1, -1).astype(F32)
    zpad = jnp.zeros((DECAY_LORA, RWKV_DIM), F32)
    res = [[] for _ in range(6)]
    for l in range(depth):
        proj_params = (
            row(norm_mix_pre[l]), w_in[l].astype(BF16), conv_w[l], row(shift_mu[l]),
            jnp.concatenate([w_decay2[l], zpad], axis=0), row(decay_w0[l]),
            jnp.concatenate([zpad, w_a2[l]], axis=0), row(a0[l]), w_g2[l],
            row(k_k[l]), row(k_a[l]), row(r_k[l]), ones_bd)
        gn = (row(gn_gain[l]), row(gn_bias[l]))
        out_params = (w_out[l].astype(BF16), row(norm_mix_post[l]), row(norm_ffn_pre[l]),
                      row(norm_ffn_post[l]), w_ff1[l].astype(BF16), w_ff2[l].astype(BF16))
        xp, c_p, s_p, w_p = _layer(
            xp, jnp.zeros((Bp, CONV_K - 1, CONV_DIM), F32), jnp.zeros((Bp, RWKV_PROJ), F32),
            jnp.zeros((Bp, N_HEADS, HEAD_SIZE, HEAD_SIZE), F32), proj_params, gn, out_params)
        xs, c_s, s_s, w_s = _layer(xs, state_conv[l], state_shift[l], state_wkv[l],
                                   proj_params, gn, out_params)
        for lst, val in zip(res, (c_p, s_p, w_p, c_s, s_s, w_s)):
            lst.append(val)
    return (xp, xs) + tuple(jnp.stack(r) for r in res)
```

```python
import functools

import jax
import jax.numpy as jnp
import numpy as np
from jax import lax
from jax.experimental import pallas as pl
from jax.experimental.pallas import tpu as pltpu

D_MODEL = 1024
CONV_DIM = 512
RWKV_DIM = 512
HEAD_SIZE = 64
N_HEADS = 8
CONV_K = 3
DECAY_LORA = 64
AAA_LORA = 64
GATE_LORA = 128
RWKV_PROJ = 3 * RWKV_DIM + DECAY_LORA + AAA_LORA + GATE_LORA
PROJ_DIM = 3 * CONV_DIM + RWKV_PROJ
D_FF = 4 * D_MODEL
NORM_EPS = 1e-6
GN_EPS = 64e-5
DECAY_SCALE = float(np.exp(-0.5))

SUBLANES = 8
INV_BASE = 16
VMEM_LIMIT = 56 * 1024 * 1024

F32 = jnp.float32
BF16 = jnp.bfloat16

_NN = (((1,), (0,)), ((), ()))
_NT = (((1,), (1,)), ((), ()))


def _dg(a, b, dims=_NN):
    return lax.dot_general(a, b, dims, preferred_element_type=F32)


def _dot1(a, b, dims=_NN):
    return _dg(a.astype(BF16), b.astype(BF16), dims)


def _bf16_terms(a, n):
    terms = []
    for _ in range(n - 1):
        t = a.astype(BF16)
        terms.append(t)
        a = a - t.astype(F32)
    terms.append(a.astype(BF16))
    return terms


def _sigmoid(x):
    return 0.5 * jnp.tanh(0.5 * x) + 0.5


def _rmsnorm(x, g):
    return x * lax.rsqrt(jnp.mean(x * x, axis=-1, keepdims=True) + NORM_EPS) * g


def _proj_stream(sq, tt, x_ref, gpre_ref, win_ref, convw_ref, mu_ref, wd_ref, w0_ref, wa_ref,
                 a0_ref, wg_ref, kk_ref, ka_ref, rk_ref, ones_ref, yconv_ref, r_ref, k_ref,
                 v_ref, kkn_ref, b_ref, lw_ref, g_ref, bg_ref, nconv_ref, nshift_ref, ubuf, zbuf):
    nbs = sq.stop - sq.start
    m = nbs * tt
    h0 = SUBLANES

    x = x_ref[sq].reshape(m, D_MODEL)
    h = _rmsnorm(x, gpre_ref[...]).astype(BF16)
    z = _dg(h, win_ref[...])
    zb = z[:, :CONV_DIM]
    zc = z[:, CONV_DIM:2 * CONV_DIM]
    zh = z[:, 2 * CONV_DIM:3 * CONV_DIM]
    zr = z[:, 3 * CONV_DIM:]
    yield

    ubuf[sq, h0:, :] = (zc * zh).reshape(nbs, tt, CONV_DIM)
    cw = convw_ref[...]
    yc = (ubuf[sq, h0 - 2:h0 - 2 + tt, :] * cw[0:1, :]
          + ubuf[sq, h0 - 1:h0 - 1 + tt, :] * cw[1:2, :]
          + ubuf[sq, h0:h0 + tt, :] * cw[2:3, :])
    yconv_ref[sq] = (zb.reshape(nbs, tt, CONV_DIM) * yc)
    last_u = ubuf[sq, h0 + tt - 2:h0 + tt, :]
    nconv_ref[sq] = last_u
    ubuf[sq, h0 - 2:h0, :] = last_u

    zbuf[sq, h0:, :] = zr.reshape(nbs, tt, RWKV_PROJ)
    zprev = zbuf[sq, h0 - 1:h0 - 1 + tt, :].reshape(m, RWKV_PROJ)
    last_z = zbuf[sq, h0 + tt - 1:h0 + tt, :]
    nshift_ref[sq] = last_z
    zbuf[sq, h0 - 1:h0, :] = last_z
    zs = zr + mu_ref[...] * (zprev - zr)

    r = zs[:, :RWKV_DIM]
    k = zs[:, RWKV_DIM:2 * RWKV_DIM]
    v = zs[:, 2 * RWKV_DIM:3 * RWKV_DIM]
    lwa = zs[:, 3 * RWKV_DIM:3 * RWKV_DIM + DECAY_LORA + AAA_LORA]
    lg = zs[:, 3 * RWKV_DIM + DECAY_LORA + AAA_LORA:]
    yield

    logit_w = w0_ref[...] + _dot1(jnp.tanh(lwa), wd_ref[...])
    logw = -DECAY_SCALE * _sigmoid(logit_w)
    a = _sigmoid(a0_ref[...] + _dot1(lwa, wa_ref[...]))
    g = _dot1(_sigmoid(lg), wg_ref[...])
    yield

    ones_bd = ones_ref[...]
    kk = k * kk_ref[...]
    ss = _dg((kk * kk).astype(BF16), ones_bd)
    kp = k * (1.0 + (a - 1.0) * ka_ref[...])
    bsum = sum(_dg(t, ones_bd) for t in _bf16_terms(r * kp * rk_ref[...], 2))
    yield

    kk = kk * lax.rsqrt(jnp.maximum(ss, 1e-24))
    shp = (nbs, tt, RWKV_DIM)
    r_ref[sq] = r.reshape(shp)
    k_ref[sq] = kp.reshape(shp)
    v_ref[sq] = v.reshape(shp)
    kkn_ref[sq] = kk.reshape(shp)
    b_ref[sq] = (kk * a).reshape(shp)
    lw_ref[sq] = logw.reshape(shp)
    g_ref[sq] = g.reshape(shp)
    bg_ref[sq] = (bsum * v * g).reshape(shp)


def _proj_kernel(x_ref, cbuf_ref, sprev_ref, *rest, nb, tt, ns):
    ubuf, zbuf = rest[-2:]

    @pl.when(pl.program_id(1) == 0)
    def _():
        ubuf[:, SUBLANES - 2:SUBLANES, :] = cbuf_ref[...]
        zbuf[:, SUBLANES - 1:SUBLANES, :] = sprev_ref[...]

    per = nb // ns
    streams = [_proj_stream(slice(i * per, (i + 1) * per), tt, x_ref, *rest) for i in range(ns)]
    while streams:
        streams = [s for s in streams if next(s, True) is None]


def _proj_call(x, conv_buf, shift_prev, params, *, nb, tt, ns):
    B, T, _ = x.shape
    grid = (B // nb, T // tt)
    tok = lambda w: pl.BlockSpec((nb, tt, w), lambda b, t: (b, t, 0))
    full = lambda arr: pl.BlockSpec(arr.shape, lambda b, t: (0,) * arr.ndim)
    in_specs = [tok(D_MODEL),
                pl.BlockSpec((nb, CONV_K - 1, CONV_DIM), lambda b, t: (b, 0, 0)),
                pl.BlockSpec((nb, 1, RWKV_PROJ), lambda b, t: (b, 0, 0))]
    in_specs += [full(p) for p in params]
    seq = lambda w: jax.ShapeDtypeStruct((B, T, w), F32)
    out_shape = [seq(CONV_DIM)] + [seq(RWKV_DIM)] * 8 + [
        jax.ShapeDtypeStruct((B, CONV_K - 1, CONV_DIM), F32),
        jax.ShapeDtypeStruct((B, 1, RWKV_PROJ), F32)]
    out_specs = [tok(CONV_DIM)] + [tok(RWKV_DIM)] * 8 + [
        pl.BlockSpec((nb, CONV_K - 1, CONV_DIM), lambda b, t: (b, 0, 0)),
        pl.BlockSpec((nb, 1, RWKV_PROJ), lambda b, t: (b, 0, 0))]
    return pl.pallas_call(
        functools.partial(_proj_kernel, nb=nb, tt=tt, ns=ns),
        grid=grid, in_specs=in_specs, out_specs=out_specs, out_shape=out_shape,
        scratch_shapes=[pltpu.VMEM((nb, tt + SUBLANES, CONV_DIM), F32),
                        pltpu.VMEM((nb, tt + SUBLANES, RWKV_PROJ), F32)],
        compiler_params=pltpu.CompilerParams(
            dimension_semantics=("arbitrary", "arbitrary"), vmem_limit_bytes=VMEM_LIMIT),
        name="proj",
    )(x, conv_buf, shift_prev, *params)


def _unit_lower_inverse(lows, c):
    base = min(INV_BASE, c)
    nblk = c // base
    prow = lax.broadcasted_iota(jnp.int32, (base, c), 0)
    pcol = lax.broadcasted_iota(jnp.int32, (base, c), 1)
    lane_blk = pcol // base
    eye_p = (pcol - lane_blk * base == prow).astype(F32)

    def pack(m):
        return sum(jnp.where(lane_blk == i, m[i * base:(i + 1) * base, :], 0.0)
                   for i in range(nblk))

    def expand(p):
        return jnp.concatenate([jnp.where(lane_blk == i, p, 0.0) for i in range(nblk)], axis=0)

    ps = [pack(low) for low in lows]
    invs = [eye_p - p for p in ps]
    pes = [expand(p) for p in ps]
    span = 2
    while span < base:
        ps = [_dot1(p, pe) for p, pe in zip(ps, pes)]
        pes = [expand(p) for p in ps]
        invs = [inv + _dot1(inv, pe) for inv, pe in zip(invs, pes)]
        span *= 2
    invs = [expand(inv) for inv in invs]

    row = lax.broadcasted_iota(jnp.int32, (c, c), 0)
    col = lax.broadcasted_iota(jnp.int32, (c, c), 1)
    blk = base
    while blk < c:
        sel = ((row // (2 * blk)) == (col // (2 * blk))) & ((row // blk) != (col // blk))
        odd = [slice((2 * b + 1) * blk, (2 * b + 2) * blk) for b in range(c // (2 * blk))]
        zero = jnp.zeros((blk, c), F32)

        def take(m):
            return jnp.concatenate([m[s] for s in odd], axis=0)

        def place(mr):
            parts = []
            for b in range(len(odd)):
                parts += [zero, mr[b * blk:(b + 1) * blk]]
            return jnp.concatenate(parts, axis=0)

        tmp = [place(_dot1(take(jnp.where(sel, low, 0.0)), inv)) for low, inv in zip(lows, invs)]
        invs = [inv - place(_dot1(take(inv), t)) for inv, t in zip(invs, tmp)]
        blk *= 2
    return invs


def _wkv_kernel(r_ref, k_ref, v_ref, kk_ref, b_ref, lw_ref, g_ref, bg_ref, s0_ref,
                gain_ref, bias_ref, y_ref, sout_ref, s_scr, *, nb, c, nc):
    pair = 2 * HEAD_SIZE
    zpad = jnp.zeros((HEAD_SIZE, HEAD_SIZE), F32)

    @pl.when(pl.program_id(1) == 0)
    def _():
        for i in range(nb):
            for hd in range(N_HEADS):
                s0 = s0_ref[i, hd]
                s_scr[i, hd] = jnp.concatenate([s0, zpad] if hd % 2 == 0 else [zpad, s0], axis=1)

    row = lax.broadcasted_iota(jnp.int32, (c, c), 0)
    col = lax.broadcasted_iota(jnp.int32, (c, c), 1)
    strict = row > col
    incl = row >= col
    row2 = lax.broadcasted_iota(jnp.int32, (c, 2 * c), 0)
    col2 = lax.broadcasted_iota(jnp.int32, (c, 2 * c), 1)
    incl2 = row2 >= jnp.where(col2 >= c, col2 - c, col2)
    tri = incl.astype(BF16)
    lane_half = lax.broadcasted_iota(jnp.int32, (c, pair), 1) // HEAD_SIZE
    own = [lane_half == 0, lane_half == 1]

    xl, xr, xe, vh, pe = [], [], [], [], []
    for q in range(nc):
        rows = slice(q * c, (q + 1) * c)
        for i in range(nb):
            logw = lw_ref[i, rows, :]
            cum = sum(_dg(tri, t) for t in _bf16_terms(logw, 2))
            cum_end = cum[c - 1:c, :]
            k = k_ref[i, rows, :]
            b = b_ref[i, rows, :]
            e_neg = jnp.exp(-cum)
            rt = r_ref[i, rows, :] * jnp.exp(cum)
            at = kk_ref[i, rows, :] * jnp.exp(cum - logw)
            bt = b * e_neg
            kt = k * e_neg
            d_end = jnp.exp(cum_end - cum)
            bh = b * d_end
            kh = k * d_end
            p_end = jnp.exp(cum_end)
            v = v_ref[i, rows, :]
            for hd in range(N_HEADS):
                lp = slice((hd // 2) * pair, (hd // 2 + 1) * pair)
                m = own[hd % 2]
                xl.append(jnp.concatenate([jnp.where(m, at[:, lp], 0.0),
                                           jnp.where(m, rt[:, lp], 0.0)], axis=0))
                xr.append(jnp.concatenate([bt[:, lp], kt[:, lp]], axis=0))
                xe.append(jnp.concatenate([jnp.where(m, bh[:, lp], 0.0),
                                           jnp.where(m, kh[:, lp], 0.0)], axis=0).astype(BF16))
                vh.append(v[:, hd * HEAD_SIZE:(hd + 1) * HEAD_SIZE])
                pe.append(p_end[:, lp])
    n = len(xl)

    gram = [_dot1(xl[j], xr[j], _NT) for j in range(n)]
    tinv = _unit_lower_inverse([jnp.where(strict, g[:c, :c], 0.0) for g in gram], c)
    yk = [_dot1(jnp.where(strict, gram[j][:c, c:], 0.0), vh[j]) for j in range(n)]
    wu = [_dot1(tinv[j], jnp.concatenate([xl[j][:c], -yk[j]], axis=1)) for j in range(n)]
    w_b = [wu[j][:, :pair].astype(BF16) for j in range(n)]
    lg = [jnp.concatenate([w_b[j], xl[j][c:].astype(BF16)], axis=0) for j in range(n)]
    u0 = [wu[j][:, pair:] for j in range(n)]
    u0t = [u.T for u in u0]
    vt = [x.T.astype(BF16) for x in vh]
    vb = [x.astype(BF16) for x in vh]
    m_r = [jnp.where(incl2, g[c:, :], 0.0).astype(BF16) for g in gram]

    nh = nb * N_HEADS
    st = [s_scr[jj // N_HEADS, jj % N_HEADS] for jj in range(nh)]
    gain = gain_ref[...]
    bias = bias_ref[...]
    for q in range(nc):
        rows = slice(q * c, (q + 1) * c)
        js = [q * nh + jj for jj in range(nh)]
        sb = [s.astype(BF16) for s in st]
        ut = [u0t[j] - _dg(sb[jj], w_b[j], _NT) for jj, j in enumerate(js)]
        gs = [_dg(lg[j], sb[jj], _NT) for jj, j in enumerate(js)]
        st = [st[jj] * pe[j]
              + _dg(jnp.concatenate([ut[jj].astype(BF16), vt[j]], axis=1), xe[j])
              for jj, j in enumerate(js)]
        uv = [jnp.concatenate([(u0[j] - gs[jj][:c]).astype(BF16), vb[j]], axis=0)
              for jj, j in enumerate(js)]
        o = [gs[jj][c:] + _dg(m_r[j], uv[jj]) for jj, j in enumerate(js)]
        mu = [jnp.mean(x, axis=-1, keepdims=True) for x in o]
        dev = [x - m for x, m in zip(o, mu)]
        var = [jnp.mean(jnp.square(d), axis=-1, keepdims=True) for d in dev]
        on = [d * lax.rsqrt(s + GN_EPS) for d, s in zip(dev, var)]
        for jj in range(0, nh, 2):
            i, hd = jj // N_HEADS, jj % N_HEADS
            lp = slice(hd * HEAD_SIZE, (hd + 2) * HEAD_SIZE)
            on_pair = jnp.concatenate([on[jj], on[jj + 1]], axis=1)
            y_ref[i, rows, lp] = ((on_pair * gain[:, lp] + bias[:, lp]) * g_ref[i, rows, lp]
                                  + bg_ref[i, rows, lp])

    for jj in range(nh):
        s_scr[jj // N_HEADS, jj % N_HEADS] = st[jj]

    @pl.when(pl.program_id(1) == pl.num_programs(1) - 1)
    def _():
        for jj in range(nh):
            hd = jj % N_HEADS
            sout_ref[jj // N_HEADS, hd] = st[jj][:, (hd % 2) * HEAD_SIZE:(hd % 2 + 1) * HEAD_SIZE]


def _wkv_call(seqs, s0, gain, bias, *, nb, c, nc):
    B, T, _ = seqs[0].shape
    tok = pl.BlockSpec((nb, nc * c, RWKV_DIM), lambda b, t: (b, t, 0))
    st = pl.BlockSpec((nb, N_HEADS, HEAD_SIZE, HEAD_SIZE), lambda b, t: (b, 0, 0, 0))
    vec = pl.BlockSpec((1, RWKV_DIM), lambda b, t: (0, 0))
    return pl.pallas_call(
        functools.partial(_wkv_kernel, nb=nb, c=c, nc=nc),
        grid=(B // nb, T // (nc * c)),
        in_specs=[tok] * 8 + [st, vec, vec],
        out_specs=[tok, st],
        out_shape=[jax.ShapeDtypeStruct((B, T, RWKV_DIM), F32),
                   jax.ShapeDtypeStruct((B, N_HEADS, HEAD_SIZE, HEAD_SIZE), F32)],
        scratch_shapes=[pltpu.VMEM((nb, N_HEADS, HEAD_SIZE, 2 * HEAD_SIZE), F32)],
        compiler_params=pltpu.CompilerParams(
            dimension_semantics=("arbitrary", "arbitrary"), vmem_limit_bytes=VMEM_LIMIT),
        name="wkv",
    )(*seqs, s0, gain, bias)


def _out_stream(rows, x_ref, yc_ref, yr_ref, wout_ref, gpost_ref, gfpre_ref, gfpost_ref,
                wff1_ref, wff2_ref, o_ref):
    wout = wout_ref[...]
    mix = (_dg(yc_ref[rows, :].astype(BF16), wout[:CONV_DIM])
           + _dg(yr_ref[rows, :].astype(BF16), wout[CONV_DIM:]))
    yield
    x1 = x_ref[rows, :] + _rmsnorm(mix, gpost_ref[...])
    h2 = _rmsnorm(x1, gfpre_ref[...]).astype(BF16)
    yield
    f1 = _dg(h2, wff1_ref[...])
    yield
    f1 = jnp.square(jnp.maximum(f1, 0.0)).astype(BF16)
    yield
    f2 = _dg(f1, wff2_ref[...])
    yield
    o_ref[rows, :] = x1 + _rmsnorm(f2, gfpost_ref[...])


def _out_kernel(*refs, tm, ns):
    per = tm // ns
    streams = [_out_stream(slice(i * per, (i + 1) * per), *refs) for i in range(ns)]
    while streams:
        streams = [s for s in streams if next(s, True) is None]


def _out_call(x, yc, yr, params, *, tm, ns):
    n = x.shape[0]
    tok = lambda w: pl.BlockSpec((tm, w), lambda i: (i, 0))
    full = lambda arr: pl.BlockSpec(arr.shape, lambda i: (0,) * arr.ndim,
                                    pipeline_mode=pl.Buffered(1))
    return pl.pallas_call(
        functools.partial(_out_kernel, tm=tm, ns=ns),
        grid=(n // tm,),
        in_specs=[tok(D_MODEL), tok(CONV_DIM), tok(RWKV_DIM)] + [full(p) for p in params],
        out_specs=tok(D_MODEL),
        out_shape=jax.ShapeDtypeStruct((n, D_MODEL), F32),
        compiler_params=pltpu.CompilerParams(
            dimension_semantics=("arbitrary",), vmem_limit_bytes=VMEM_LIMIT),
        name="outffn",
    )(x, yc, yr, *params)


def _tiles(B, T):
    if T >= 256:
        return 2, 256, 2, 2, 64, 4, 512, 2
    return B, T, 1, 4, T, 1, B * T, 1


def _layer(x, conv_buf, shift_prev, wkv_state, proj_params, gn, out_params):
    B, T, _ = x.shape
    pnb, ptt, pns, wnb, wc, wnc, tm, ons = _tiles(B, T)
    outs = _proj_call(x, conv_buf, shift_prev.reshape(B, 1, RWKV_PROJ), proj_params,
                      nb=pnb, tt=ptt, ns=pns)
    yconv, seqs, new_conv, new_shift = outs[0], outs[1:9], outs[9], outs[10]
    yr, new_wkv = _wkv_call(seqs, wkv_state, *gn, nb=wnb, c=wc, nc=wnc)
    y = _out_call(x.reshape(B * T, D_MODEL), yconv.reshape(B * T, CONV_DIM),
                  yr.reshape(B * T, RWKV_DIM), out_params, tm=tm, ns=ons)
    return y.reshape(B, T, D_MODEL), new_conv, new_shift.reshape(B, RWKV_PROJ), new_wkv


def kernel(x_prompt, x_sample, state_conv, state_shift, state_wkv, norm_mix_pre, norm_mix_post,
           norm_ffn_pre, norm_ffn_post, w_in, conv_w, shift_mu, w_decay2, decay_w0, w_a2, a0,
           w_g2, k_k, k_a, r_k, gn_gain, gn_bias, w_out, w_ff1, w_ff2):
    depth = w_in.shape[0]
    Bp = x_prompt.shape[0]
    xp, xs = x_prompt, x_sample
    hid = jnp.arange(RWKV_DIM) // HEAD_SIZE
    ones_bd = (hid[:, None] == hid[None, :]).astype(BF16)
    row = lambda t: t.reshape(1, -1).astype(F32)
    zpad = jnp.zeros((DECAY_LORA, RWKV_DIM), F32)
    res = [[] for _ in range(6)]
    for l in range(depth):
        proj_params = (
            row(norm_mix_pre[l]), w_in[l].astype(BF16), conv_w[l], row(shift_mu[l]),
            jnp.concatenate([w_decay2[l], zpad], axis=0), row(decay_w0[l]),
            jnp.concatenate([zpad, w_a2[l]], axis=0), row(a0[l]), w_g2[l],
            row(k_k[l]), row(k_a[l]), row(r_k[l]), ones_bd)
        gn = (row(gn_gain[l]), row(gn_bias[l]))
        out_params = (w_out[l].astype(BF16), row(norm_mix_post[l]), row(norm_ffn_pre[l]),
                      row(norm_ffn_post[l]), w_ff1[l].astype(BF16), w_ff2[l].astype(BF16))
        xp, c_p, s_p, w_p = _layer(
            xp, jnp.zeros((Bp, CONV_K - 1, CONV_DIM), F32), jnp.zeros((Bp, RWKV_PROJ), F32),
            jnp.zeros((Bp, N_HEADS, HEAD_SIZE, HEAD_SIZE), F32), proj_params, gn, out_params)
        xs, c_s, s_s, w_s = _layer(xs, state_conv[l], state_shift[l], state_wkv[l],
                                   proj_params, gn, out_params)
        for lst, val in zip(res, (c_p, s_p, w_p, c_s, s_s, w_s)):
            lst.append(val)
    return (xp, xs) + tuple(jnp.stack(r) for r in res)
```

```python
import functools

import jax
import jax.numpy as jnp
import numpy as np
from jax import lax
from jax.experimental import pallas as pl
from jax.experimental.pallas import tpu as pltpu

D_MODEL = 1024
CONV_DIM = 512
RWKV_DIM = 512
HEAD_SIZE = 64
N_HEADS = 8
CONV_K = 3
DECAY_LORA = 64
AAA_LORA = 64
GATE_LORA = 128
RWKV_PROJ = 3 * RWKV_DIM + DECAY_LORA + AAA_LORA + GATE_LORA
PROJ_DIM = 3 * CONV_DIM + RWKV_PROJ
D_FF = 4 * D_MODEL
NORM_EPS = 1e-6
GN_EPS = 64e-5
DECAY_SCALE = float(np.exp(-0.5))

SUBLANES = 8
INV_BASE = 16
VMEM_LIMIT = 56 * 1024 * 1024

F32 = jnp.float32
BF16 = jnp.bfloat16

_NN = (((1,), (0,)), ((), ()))
_NT = (((1,), (1,)), ((), ()))


def _dg(a, b, dims=_NN):
    return lax.dot_general(a, b, dims, preferred_element_type=F32)


def _dot1(a, b, dims=_NN):
    return _dg(a.astype(BF16), b.astype(BF16), dims)


def _bf16_terms(a, n):
    terms = []
    for _ in range(n - 1):
        t = a.astype(BF16)
        terms.append(t)
        a = a - t.astype(F32)
    terms.append(a.astype(BF16))
    return terms


def _sigmoid(x):
    return 0.5 * jnp.tanh(0.5 * x) + 0.5


def _interleave(streams):
    live = list(streams)
    while live:
        live = [s for s in live if next(s, True) is None]


def _head_sums(x):
    pair = 2 * HEAD_SIZE
    low = lax.broadcasted_iota(jnp.int32, (x.shape[0], pair), 1) < HEAD_SIZE
    tiles = []
    for p in range(RWKV_DIM // pair):
        t = x[:, p * pair:(p + 1) * pair]
        s_lo = jnp.sum(jnp.where(low, t, 0.0), axis=-1, keepdims=True)
        s_hi = jnp.sum(jnp.where(low, 0.0, t), axis=-1, keepdims=True)
        tiles.append(jnp.where(low, s_lo, s_hi))
    return jnp.concatenate(tiles, axis=1)


def _rmsnorm(x, g):
    return x * lax.rsqrt(jnp.mean(x * x, axis=-1, keepdims=True) + NORM_EPS) * g


def _proj_stream(sq, tt, x_ref, gpre_ref, win_ref, convw_ref, mu_ref, wd_ref, w0_ref, wa_ref,
                 a0_ref, wg_ref, kk_ref, ka_ref, rk_ref, yconv_ref, r_ref, k_ref,
                 v_ref, kkn_ref, b_ref, lw_ref, g_ref, bg_ref, nconv_ref, nshift_ref, ubuf, zbuf):
    nbs = sq.stop - sq.start
    m = nbs * tt
    h0 = SUBLANES

    x = x_ref[sq].reshape(m, D_MODEL)
    h = _rmsnorm(x, gpre_ref[...]).astype(BF16)
    z = _dg(h, win_ref[...])
    zb = z[:, :CONV_DIM]
    zc = z[:, CONV_DIM:2 * CONV_DIM]
    zh = z[:, 2 * CONV_DIM:3 * CONV_DIM]
    zr = z[:, 3 * CONV_DIM:]
    yield

    ubuf[sq, h0:, :] = (zc * zh).reshape(nbs, tt, CONV_DIM)
    cw = convw_ref[...]
    yc = (ubuf[sq, h0 - 2:h0 - 2 + tt, :] * cw[0:1, :]
          + ubuf[sq, h0 - 1:h0 - 1 + tt, :] * cw[1:2, :]
          + ubuf[sq, h0:h0 + tt, :] * cw[2:3, :])
    yconv_ref[sq] = (zb.reshape(nbs, tt, CONV_DIM) * yc)
    last_u = ubuf[sq, h0 + tt - 2:h0 + tt, :]
    nconv_ref[sq] = last_u
    ubuf[sq, h0 - 2:h0, :] = last_u

    zbuf[sq, h0:, :] = zr.reshape(nbs, tt, RWKV_PROJ)
    zprev = zbuf[sq, h0 - 1:h0 - 1 + tt, :].reshape(m, RWKV_PROJ)
    last_z = zbuf[sq, h0 + tt - 1:h0 + tt, :]
    nshift_ref[sq] = last_z
    zbuf[sq, h0 - 1:h0, :] = last_z
    zs = zr + mu_ref[...] * (zprev - zr)

    r = zs[:, :RWKV_DIM]
    k = zs[:, RWKV_DIM:2 * RWKV_DIM]
    v = zs[:, 2 * RWKV_DIM:3 * RWKV_DIM]
    lwa = zs[:, 3 * RWKV_DIM:3 * RWKV_DIM + DECAY_LORA + AAA_LORA]
    lg = zs[:, 3 * RWKV_DIM + DECAY_LORA + AAA_LORA:]
    yield

    logit_w = w0_ref[...] + _dot1(jnp.tanh(lwa), wd_ref[...])
    logw = -DECAY_SCALE * _sigmoid(logit_w)
    a = _sigmoid(a0_ref[...] + _dot1(lwa, wa_ref[...]))
    g = _dot1(_sigmoid(lg), wg_ref[...])
    yield

    kk = k * kk_ref[...]
    ss = _head_sums(kk * kk)
    kp = k * (1.0 + (a - 1.0) * ka_ref[...])
    bsum = _head_sums(r * kp * rk_ref[...])
    yield

    kk = kk * lax.rsqrt(jnp.maximum(ss, 1e-24))
    shp = (nbs, tt, RWKV_DIM)
    r_ref[sq] = r.reshape(shp)
    k_ref[sq] = kp.reshape(shp)
    v_ref[sq] = v.reshape(shp)
    kkn_ref[sq] = kk.reshape(shp)
    b_ref[sq] = (kk * a).reshape(shp)
    lw_ref[sq] = logw.reshape(shp)
    g_ref[sq] = g.reshape(shp)
    bg_ref[sq] = (bsum * v * g).reshape(shp)


def _proj_kernel(x_ref, cbuf_ref, sprev_ref, *rest, nb, tt, ns, ncast):
    ubuf, zbuf = rest[-2:]
    params, cast_in = rest[:12], rest[12:12 + ncast]
    outs, cast_out = rest[12 + ncast:23 + ncast], rest[23 + ncast:23 + 2 * ncast]

    @pl.when(pl.program_id(1) == 0)
    def _():
        ubuf[:, SUBLANES - 2:SUBLANES, :] = cbuf_ref[...]
        zbuf[:, SUBLANES - 1:SUBLANES, :] = sprev_ref[...]

    for src, dst in zip(cast_in, cast_out):
        dst[...] = src[...].astype(BF16)

    per = nb // ns
    _interleave([_proj_stream(slice(i * per, (i + 1) * per), tt, x_ref, *params, *outs,
                              ubuf, zbuf) for i in range(ns)])


def _proj_call(x, conv_buf, shift_prev, params, cast_srcs, *, nb, tt, ns):
    B, T, _ = x.shape
    nt = T // tt
    grid = (B // nb, nt)
    steps = grid[0] * nt
    tok = lambda w: pl.BlockSpec((nb, tt, w), lambda b, t: (b, t, 0))
    full = lambda arr: pl.BlockSpec(arr.shape, lambda b, t: (0,) * arr.ndim)
    rows = lambda arr: pl.BlockSpec((arr.shape[0] // steps, arr.shape[1]),
                                    lambda b, t: (b * nt + t, 0))
    in_specs = [tok(D_MODEL),
                pl.BlockSpec((nb, CONV_K - 1, CONV_DIM), lambda b, t: (b, 0, 0)),
                pl.BlockSpec((nb, 1, RWKV_PROJ), lambda b, t: (b, 0, 0))]
    in_specs += [full(p) for p in params] + [rows(w) for w in cast_srcs]
    seq = lambda w: jax.ShapeDtypeStruct((B, T, w), F32)
    out_shape = [seq(CONV_DIM)] + [seq(RWKV_DIM)] * 8 + [
        jax.ShapeDtypeStruct((B, CONV_K - 1, CONV_DIM), F32),
        jax.ShapeDtypeStruct((B, 1, RWKV_PROJ), F32)]
    out_shape += [jax.ShapeDtypeStruct(w.shape, BF16) for w in cast_srcs]
    out_specs = [tok(CONV_DIM)] + [tok(RWKV_DIM)] * 8 + [
        pl.BlockSpec((nb, CONV_K - 1, CONV_DIM), lambda b, t: (b, 0, 0)),
        pl.BlockSpec((nb, 1, RWKV_PROJ), lambda b, t: (b, 0, 0))]
    out_specs += [rows(w) for w in cast_srcs]
    return pl.pallas_call(
        functools.partial(_proj_kernel, nb=nb, tt=tt, ns=ns, ncast=len(cast_srcs)),
        grid=grid, in_specs=in_specs, out_specs=out_specs, out_shape=out_shape,
        scratch_shapes=[pltpu.VMEM((nb, tt + SUBLANES, CONV_DIM), F32),
                        pltpu.VMEM((nb, tt + SUBLANES, RWKV_PROJ), F32)],
        compiler_params=pltpu.CompilerParams(
            dimension_semantics=("arbitrary", "arbitrary"), vmem_limit_bytes=VMEM_LIMIT),
        name="proj",
    )(x, conv_buf, shift_prev, *params, *cast_srcs)


def _unit_lower_inverse(lows, c):
    base = min(INV_BASE, c)
    nblk = c // base
    prow = lax.broadcasted_iota(jnp.int32, (base, c), 0)
    pcol = lax.broadcasted_iota(jnp.int32, (base, c), 1)
    lane_blk = pcol // base
    eye_p = (pcol - lane_blk * base == prow).astype(F32)

    def pack(m):
        return sum(jnp.where(lane_blk == i, m[i * base:(i + 1) * base, :], 0.0)
                   for i in range(nblk))

    def expand(p):
        return jnp.concatenate([jnp.where(lane_blk == i, p, 0.0) for i in range(nblk)], axis=0)

    ps = [pack(low) for low in lows]
    invs = [eye_p - p for p in ps]
    pes = [expand(p) for p in ps]
    span = 2
    while span < base:
        ps = [_dot1(p, pe) for p, pe in zip(ps, pes)]
        pes = [expand(p) for p in ps]
        invs = [inv + _dot1(inv, pe) for inv, pe in zip(invs, pes)]
        span *= 2
    invs = [expand(inv) for inv in invs]

    row = lax.broadcasted_iota(jnp.int32, (c, c), 0)
    col = lax.broadcasted_iota(jnp.int32, (c, c), 1)
    blk = base
    while blk < c:
        sel = ((row // (2 * blk)) == (col // (2 * blk))) & ((row // blk) != (col // blk))
        odd = [slice((2 * b + 1) * blk, (2 * b + 2) * blk) for b in range(c // (2 * blk))]
        zero = jnp.zeros((blk, c), F32)

        def take(m):
            return jnp.concatenate([m[s] for s in odd], axis=0)

        def place(mr):
            parts = []
            for b in range(len(odd)):
                parts += [zero, mr[b * blk:(b + 1) * blk]]
            return jnp.concatenate(parts, axis=0)

        tmp = [place(_dot1(take(jnp.where(sel, low, 0.0)), inv)) for low, inv in zip(lows, invs)]
        invs = [inv - place(_dot1(take(inv), t)) for inv, t in zip(invs, tmp)]
        blk *= 2
    return invs


def _wkv_kernel(r_ref, k_ref, v_ref, kk_ref, b_ref, lw_ref, g_ref, bg_ref, s0_ref,
                gain_ref, bias_ref, y_ref, sout_ref, s_scr, *, nb, c, nc):
    pair = 2 * HEAD_SIZE
    zpad = jnp.zeros((HEAD_SIZE, HEAD_SIZE), F32)

    @pl.when(pl.program_id(1) == 0)
    def _():
        for i in range(nb):
            for hd in range(N_HEADS):
                s0 = s0_ref[i, hd]
                s_scr[i, hd] = jnp.concatenate([s0, zpad] if hd % 2 == 0 else [zpad, s0], axis=1)

    row = lax.broadcasted_iota(jnp.int32, (c, c), 0)
    col = lax.broadcasted_iota(jnp.int32, (c, c), 1)
    strict = row > col
    incl = row >= col
    row2 = lax.broadcasted_iota(jnp.int32, (c, 2 * c), 0)
    col2 = lax.broadcasted_iota(jnp.int32, (c, 2 * c), 1)
    incl2 = row2 >= jnp.where(col2 >= c, col2 - c, col2)
    tri = incl.astype(BF16)
    lane_half = lax.broadcasted_iota(jnp.int32, (c, pair), 1) // HEAD_SIZE
    own = [lane_half == 0, lane_half == 1]

    xl, xr, xe, vh, pe = [], [], [], [], []
    for q in range(nc):
        rows = slice(q * c, (q + 1) * c)
        for i in range(nb):
            logw = lw_ref[i, rows, :]
            cum = sum(_dg(tri, t) for t in _bf16_terms(logw, 2))
            cum_end = cum[c - 1:c, :]
            k = k_ref[i, rows, :]
            b = b_ref[i, rows, :]
            e_neg = jnp.exp(-cum)
            rt = r_ref[i, rows, :] * jnp.exp(cum)
            at = kk_ref[i, rows, :] * jnp.exp(cum - logw)
            bt = b * e_neg
            kt = k * e_neg
            d_end = jnp.exp(cum_end - cum)
            bh = b * d_end
            kh = k * d_end
            p_end = jnp.exp(cum_end)
            v = v_ref[i, rows, :]
            for hd in range(N_HEADS):
                lp = slice((hd // 2) * pair, (hd // 2 + 1) * pair)
                m = own[hd % 2]
                xl.append(jnp.concatenate([jnp.where(m, at[:, lp], 0.0),
                                           jnp.where(m, rt[:, lp], 0.0)], axis=0))
                xr.append(jnp.concatenate([bt[:, lp], kt[:, lp]], axis=0))
                xe.append(jnp.concatenate([jnp.where(m, bh[:, lp], 0.0),
                                           jnp.where(m, kh[:, lp], 0.0)], axis=0).astype(BF16))
                vh.append(v[:, hd * HEAD_SIZE:(hd + 1) * HEAD_SIZE])
                pe.append(p_end[:, lp])
    n = len(xl)

    gram = [_dot1(xl[j], xr[j], _NT) for j in range(n)]
    tinv = _unit_lower_inverse([jnp.where(strict, g[:c, :c], 0.0) for g in gram], c)
    yk = [_dot1(jnp.where(strict, gram[j][:c, c:], 0.0), vh[j]) for j in range(n)]
    wu = [_dot1(tinv[j], jnp.concatenate([xl[j][:c], -yk[j]], axis=1)) for j in range(n)]
    w_b = [wu[j][:, :pair].astype(BF16) for j in range(n)]
    lg = [jnp.concatenate([w_b[j], xl[j][c:].astype(BF16)], axis=0) for j in range(n)]
    u0 = [wu[j][:, pair:] for j in range(n)]
    u0t = [u.T for u in u0]
    vt = [x.T.astype(BF16) for x in vh]
    vb = [x.astype(BF16) for x in vh]
    m_r = [jnp.where(incl2, g[c:, :], 0.0).astype(BF16) for g in gram]

    nh = nb * N_HEADS
    st = [s_scr[jj // N_HEADS, jj % N_HEADS] for jj in range(nh)]
    gain = gain_ref[...]
    bias = bias_ref[...]
    for q in range(nc):
        rows = slice(q * c, (q + 1) * c)
        js = [q * nh + jj for jj in range(nh)]
        sb = [s.astype(BF16) for s in st]
        ut = [u0t[j] - _dg(sb[jj], w_b[j], _NT) for jj, j in enumerate(js)]
        gs = [_dg(lg[j], sb[jj], _NT) for jj, j in enumerate(js)]
        st = [st[jj] * pe[j]
              + _dg(jnp.concatenate([ut[jj].astype(BF16), vt[j]], axis=1), xe[j])
              for jj, j in enumerate(js)]
        uv = [jnp.concatenate([(u0[j] - gs[jj][:c]).astype(BF16), vb[j]], axis=0)
              for jj, j in enumerate(js)]
        o = [gs[jj][c:] + _dg(m_r[j], uv[jj]) for jj, j in enumerate(js)]
        mu = [jnp.mean(x, axis=-1, keepdims=True) for x in o]
        dev = [x - m for x, m in zip(o, mu)]
        var = [jnp.mean(jnp.square(d), axis=-1, keepdims=True) for d in dev]
        on = [d * lax.rsqrt(s + GN_EPS) for d, s in zip(dev, var)]
        for jj in range(0, nh, 2):
            i, hd = jj // N_HEADS, jj % N_HEADS
            lp = slice(hd * HEAD_SIZE, (hd + 2) * HEAD_SIZE)
            on_pair = jnp.concatenate([on[jj], on[jj + 1]], axis=1)
            y_ref[i, rows, lp] = ((on_pair * gain[:, lp] + bias[:, lp]) * g_ref[i, rows, lp]
                                  + bg_ref[i, rows, lp])

    for jj in range(nh):
        s_scr[jj // N_HEADS, jj % N_HEADS] = st[jj]

    @pl.when(pl.program_id(1) == pl.num_programs(1) - 1)
    def _():
        for jj in range(nh):
            hd = jj % N_HEADS
            sout_ref[jj // N_HEADS, hd] = st[jj][:, (hd % 2) * HEAD_SIZE:(hd % 2 + 1) * HEAD_SIZE]


def _wkv_call(seqs, s0, gain, bias, *, nb, c, nc):
    B, T, _ = seqs[0].shape
    tok = pl.BlockSpec((nb, nc * c, RWKV_DIM), lambda b, t: (b, t, 0))
    st = pl.BlockSpec((nb, N_HEADS, HEAD_SIZE, HEAD_SIZE), lambda b, t: (b, 0, 0, 0))
    vec = pl.BlockSpec((1, RWKV_DIM), lambda b, t: (0, 0))
    return pl.pallas_call(
        functools.partial(_wkv_kernel, nb=nb, c=c, nc=nc),
        grid=(B // nb, T // (nc * c)),
        in_specs=[tok] * 8 + [st, vec, vec],
        out_specs=[tok, st],
        out_shape=[jax.ShapeDtypeStruct((B, T, RWKV_DIM), F32),
                   jax.ShapeDtypeStruct((B, N_HEADS, HEAD_SIZE, HEAD_SIZE), F32)],
        scratch_shapes=[pltpu.VMEM((nb, N_HEADS, HEAD_SIZE, 2 * HEAD_SIZE), F32)],
        compiler_params=pltpu.CompilerParams(
            dimension_semantics=("arbitrary", "arbitrary"), vmem_limit_bytes=VMEM_LIMIT),
        name="wkv",
    )(*seqs, s0, gain, bias)


def _out_stream(rows, x_ref, yc_ref, yr_ref, wout_ref, gpost_ref, gfpre_ref, gfpost_ref,
                wff1_ref, wff2_ref, o_ref):
    wout = wout_ref[...]
    mix = (_dg(yc_ref[rows, :].astype(BF16), wout[:CONV_DIM])
           + _dg(yr_ref[rows, :].astype(BF16), wout[CONV_DIM:]))
    yield
    x1 = x_ref[rows, :] + _rmsnorm(mix, gpost_ref[...])
    h2 = _rmsnorm(x1, gfpre_ref[...]).astype(BF16)
    yield
    f1 = _dg(h2, wff1_ref[...])
    yield
    f1 = jnp.square(jnp.maximum(f1, 0.0)).astype(BF16)
    yield
    f2 = _dg(f1, wff2_ref[...])
    yield
    o_ref[rows, :] = x1 + _rmsnorm(f2, gfpost_ref[...])


def _out_kernel(*refs, tm, ns):
    per = tm // ns
    _interleave([_out_stream(slice(i * per, (i + 1) * per), *refs) for i in range(ns)])


def _out_call(x, yc, yr, params, *, tm, ns):
    n = x.shape[0]
    tok = lambda w: pl.BlockSpec((tm, w), lambda i: (i, 0))
    full = lambda arr: pl.BlockSpec(arr.shape, lambda i: (0,) * arr.ndim,
                                    pipeline_mode=pl.Buffered(1))
    return pl.pallas_call(
        functools.partial(_out_kernel, tm=tm, ns=ns),
        grid=(n // tm,),
        in_specs=[tok(D_MODEL), tok(CONV_DIM), tok(RWKV_DIM)] + [full(p) for p in params],
        out_specs=tok(D_MODEL),
        out_shape=jax.ShapeDtypeStruct((n, D_MODEL), F32),
        compiler_params=pltpu.CompilerParams(
            dimension_semantics=("arbitrary",), vmem_limit_bytes=VMEM_LIMIT),
        name="outffn",
    )(x, yc, yr, *params)


def _tiles(B, T):
    if T >= 256:
        return 2, 256, 2, 2, 64, 4, 512, 2
    return B, T, 1, 4, T, 1, B * T, 1


def _layer(x, conv_buf, shift_prev, wkv_state, proj_params, gn, norms, out_w):
    B, T, _ = x.shape
    pnb, ptt, pns, wnb, wc, wnc, tm, ons = _tiles(B, T)
    cast = [w for w in out_w if w.dtype != BF16]
    outs = _proj_call(x, conv_buf, shift_prev.reshape(B, 1, RWKV_PROJ), proj_params, cast,
                      nb=pnb, tt=ptt, ns=pns)
    yconv, seqs, new_conv, new_shift = outs[0], outs[1:9], outs[9], outs[10]
    if cast:
        out_w = tuple(outs[11:])
    yr, new_wkv = _wkv_call(seqs, wkv_state, *gn, nb=wnb, c=wc, nc=wnc)
    out_params = (out_w[0], *norms, out_w[1], out_w[2])
    y = _out_call(x.reshape(B * T, D_MODEL), yconv.reshape(B * T, CONV_DIM),
                  yr.reshape(B * T, RWKV_DIM), out_params, tm=tm, ns=ons)
    return (y.reshape(B, T, D_MODEL), new_conv, new_shift.reshape(B, RWKV_PROJ), new_wkv,
            out_w)


def kernel(x_prompt, x_sample, state_conv, state_shift, state_wkv, norm_mix_pre, norm_mix_post,
           norm_ffn_pre, norm_ffn_post, w_in, conv_w, shift_mu, w_decay2, decay_w0, w_a2, a0,
           w_g2, k_k, k_a, r_k, gn_gain, gn_bias, w_out, w_ff1, w_ff2):
    depth = w_in.shape[0]
    Bp = x_prompt.shape[0]
    xp, xs = x_prompt, x_sample
    row = lambda t: t.reshape(1, -1).astype(F32)
    zpad = jnp.zeros((DECAY_LORA, RWKV_DIM), F32)
    res = [[] for _ in range(6)]
    for l in range(depth):
        proj_params = (
            row(norm_mix_pre[l]), w_in[l].astype(BF16), conv_w[l], row(shift_mu[l]),
            jnp.concatenate([w_decay2[l], zpad], axis=0), row(decay_w0[l]),
            jnp.concatenate([zpad, w_a2[l]], axis=0), row(a0[l]), w_g2[l],
            row(k_k[l]), row(k_a[l]), row(r_k[l]))
        gn = (row(gn_gain[l]), row(gn_bias[l]))
        norms = (row(norm_mix_post[l]), row(norm_ffn_pre[l]), row(norm_ffn_post[l]))
        xp, c_p, s_p, w_p, out_w = _layer(
            xp, jnp.zeros((Bp, CONV_K - 1, CONV_DIM), F32), jnp.zeros((Bp, RWKV_PROJ), F32),
            jnp.zeros((Bp, N_HEADS, HEAD_SIZE, HEAD_SIZE), F32), proj_params, gn, norms,
            (w_out[l], w_ff1[l], w_ff2[l]))
        xs, c_s, s_s, w_s, _ = _layer(xs, state_conv[l], state_shift[l], state_wkv[l],
                                      proj_params, gn, norms, out_w)
        for lst, val in zip(res, (c_p, s_p, w_p, c_s, s_s, w_s)):
            lst.append(val)
    return (xp, xs) + tuple(jnp.stack(r) for r in res)
```

```python
import functools

import jax
import jax.numpy as jnp
import numpy as np
from jax import lax
from jax.experimental import pallas as pl
from jax.experimental.pallas import tpu as pltpu

D_MODEL = 1024
CONV_DIM = 512
RWKV_DIM = 512
HEAD_SIZE = 64
N_HEADS = 8
CONV_K = 3
DECAY_LORA = 64
AAA_LORA = 64
GATE_LORA = 128
RWKV_PROJ = 3 * RWKV_DIM + DECAY_LORA + AAA_LORA + GATE_LORA
PROJ_DIM = 3 * CONV_DIM + RWKV_PROJ
D_FF = 4 * D_MODEL
NORM_EPS = 1e-6
GN_EPS = 64e-5
DECAY_SCALE = float(np.exp(-0.5))

SUBLANES = 8
INV_BASE = 16
VMEM_LIMIT = 56 * 1024 * 1024

F32 = jnp.float32
BF16 = jnp.bfloat16

_NN = (((1,), (0,)), ((), ()))
_NT = (((1,), (1,)), ((), ()))


def _dg(a, b, dims=_NN):
    return lax.dot_general(a, b, dims, preferred_element_type=F32)


def _dot1(a, b, dims=_NN):
    return _dg(a.astype(BF16), b.astype(BF16), dims)


def _bf16_terms(a, n):
    terms = []
    for _ in range(n - 1):
        t = a.astype(BF16)
        terms.append(t)
        a = a - t.astype(F32)
    terms.append(a.astype(BF16))
    return terms


def _sigmoid(x):
    return 0.5 * jnp.tanh(0.5 * x) + 0.5


def _interleave(streams):
    live = list(streams)
    while live:
        live = [s for s in live if next(s, True) is None]


def _head_sums(x):
    pair = 2 * HEAD_SIZE
    low = lax.broadcasted_iota(jnp.int32, (x.shape[0], pair), 1) < HEAD_SIZE
    tiles = []
    for p in range(RWKV_DIM // pair):
        t = x[:, p * pair:(p + 1) * pair]
        s_lo = jnp.sum(jnp.where(low, t, 0.0), axis=-1, keepdims=True)
        s_hi = jnp.sum(jnp.where(low, 0.0, t), axis=-1, keepdims=True)
        tiles.append(jnp.where(low, s_lo, s_hi))
    return jnp.concatenate(tiles, axis=1)


def _rmsnorm(x, g):
    return x * lax.rsqrt(jnp.mean(x * x, axis=-1, keepdims=True) + NORM_EPS) * g


def _proj_stream(sq, rq, tt, x_ref, gpre_ref, win_ref, convw_ref, mu_ref, wd_ref, w0_ref, wa_ref,
                 a0_ref, wg_ref, kk_ref, ka_ref, rk_ref, yconv_ref, r_ref, k_ref,
                 v_ref, kkn_ref, b_ref, lw_ref, g_ref, bg_ref, nconv_ref, nshift_ref, ubuf, zbuf):
    nbs = sq.stop - sq.start
    rn = rq.stop - rq.start
    m = nbs * rn
    lo = SUBLANES + rq.start
    hi = lo + rn

    x = x_ref[sq, rq].reshape(m, D_MODEL)
    h = _rmsnorm(x, gpre_ref[...]).astype(BF16)
    z = _dg(h, win_ref[...])
    zb = z[:, :CONV_DIM]
    zc = z[:, CONV_DIM:2 * CONV_DIM]
    zh = z[:, 2 * CONV_DIM:3 * CONV_DIM]
    zr = z[:, 3 * CONV_DIM:]
    yield

    ubuf[sq, lo:hi, :] = (zc * zh).reshape(nbs, rn, CONV_DIM)
    cw = convw_ref[...]
    yc = (ubuf[sq, lo - 2:hi - 2, :] * cw[0:1, :]
          + ubuf[sq, lo - 1:hi - 1, :] * cw[1:2, :]
          + ubuf[sq, lo:hi, :] * cw[2:3, :])
    yconv_ref[sq, rq] = (zb.reshape(nbs, rn, CONV_DIM) * yc)

    zbuf[sq, lo:hi, :] = zr.reshape(nbs, rn, RWKV_PROJ)
    zprev = zbuf[sq, lo - 1:hi - 1, :].reshape(m, RWKV_PROJ)
    zs = zr + mu_ref[...] * (zprev - zr)

    if rq.stop == tt:
        last_u = ubuf[sq, hi - 2:hi, :]
        nconv_ref[sq] = last_u
        ubuf[sq, SUBLANES - 2:SUBLANES, :] = last_u
        last_z = zbuf[sq, hi - 1:hi, :]
        nshift_ref[sq] = last_z
        zbuf[sq, SUBLANES - 1:SUBLANES, :] = last_z

    r = zs[:, :RWKV_DIM]
    k = zs[:, RWKV_DIM:2 * RWKV_DIM]
    v = zs[:, 2 * RWKV_DIM:3 * RWKV_DIM]
    lwa = zs[:, 3 * RWKV_DIM:3 * RWKV_DIM + DECAY_LORA + AAA_LORA]
    lg = zs[:, 3 * RWKV_DIM + DECAY_LORA + AAA_LORA:]
    yield

    logit_w = w0_ref[...] + _dot1(jnp.tanh(lwa[:, :DECAY_LORA]), wd_ref[...])
    logw = -DECAY_SCALE * _sigmoid(logit_w)
    a = _sigmoid(a0_ref[...] + _dot1(lwa[:, DECAY_LORA:], wa_ref[...]))
    g = _dot1(_sigmoid(lg), wg_ref[...])
    yield

    kk = k * kk_ref[...]
    ss = _head_sums(kk * kk)
    kp = k * (1.0 + (a - 1.0) * ka_ref[...])
    bsum = _head_sums(r * kp * rk_ref[...])
    yield

    kk = kk * lax.rsqrt(jnp.maximum(ss, 1e-24))
    shp = (nbs, rn, RWKV_DIM)
    r_ref[sq, rq] = r.reshape(shp)
    k_ref[sq, rq] = kp.reshape(shp)
    v_ref[sq, rq] = v.reshape(shp)
    kkn_ref[sq, rq] = kk.reshape(shp)
    b_ref[sq, rq] = (kk * a).reshape(shp)
    lw_ref[sq, rq] = logw.reshape(shp)
    g_ref[sq, rq] = g.reshape(shp)
    bg_ref[sq, rq] = (bsum * v * g).reshape(shp)


def _proj_kernel(x_ref, cbuf_ref, sprev_ref, *rest, nb, tt, ns, ncast):
    ubuf, zbuf = rest[-2:]
    params, cast_in = rest[:12], rest[12:12 + ncast]
    outs, cast_out = rest[12 + ncast:23 + ncast], rest[23 + ncast:23 + 2 * ncast]

    @pl.when(pl.program_id(1) == 0)
    def _():
        ubuf[:, SUBLANES - 2:SUBLANES, :] = cbuf_ref[...]
        zbuf[:, SUBLANES - 1:SUBLANES, :] = sprev_ref[...]

    for src, dst in zip(cast_in, cast_out):
        dst[...] = src[...].astype(BF16)

    nseq = min(ns, nb)
    per, rn = nb // nseq, tt // (ns // nseq)
    _interleave([_proj_stream(slice(i * per, (i + 1) * per), slice(r0, r0 + rn), tt, x_ref,
                              *params, *outs, ubuf, zbuf)
                 for i in range(nseq) for r0 in range(0, tt, rn)])


def _proj_call(x, conv_buf, shift_prev, params, cast_srcs, *, nb, tt, ns):
    B, T, _ = x.shape
    nt = T // tt
    grid = (B // nb, nt)
    steps = grid[0] * nt
    tok = lambda w: pl.BlockSpec((nb, tt, w), lambda b, t: (b, t, 0))
    full = lambda arr: pl.BlockSpec(arr.shape, lambda b, t: (0,) * arr.ndim)
    rows = lambda arr: pl.BlockSpec((arr.shape[0] // steps, arr.shape[1]),
                                    lambda b, t: (b * nt + t, 0))
    in_specs = [tok(D_MODEL),
                pl.BlockSpec((nb, CONV_K - 1, CONV_DIM), lambda b, t: (b, 0, 0)),
                pl.BlockSpec((nb, 1, RWKV_PROJ), lambda b, t: (b, 0, 0))]
    in_specs += [full(p) for p in params] + [rows(w) for w in cast_srcs]
    seq = lambda w: jax.ShapeDtypeStruct((B, T, w), F32)
    out_shape = [seq(CONV_DIM)] + [seq(RWKV_DIM)] * 8 + [
        jax.ShapeDtypeStruct((B, CONV_K - 1, CONV_DIM), F32),
        jax.ShapeDtypeStruct((B, 1, RWKV_PROJ), F32)]
    out_shape += [jax.ShapeDtypeStruct(w.shape, BF16) for w in cast_srcs]
    out_specs = [tok(CONV_DIM)] + [tok(RWKV_DIM)] * 8 + [
        pl.BlockSpec((nb, CONV_K - 1, CONV_DIM), lambda b, t: (b, 0, 0)),
        pl.BlockSpec((nb, 1, RWKV_PROJ), lambda b, t: (b, 0, 0))]
    out_specs += [rows(w) for w in cast_srcs]
    return pl.pallas_call(
        functools.partial(_proj_kernel, nb=nb, tt=tt, ns=ns, ncast=len(cast_srcs)),
        grid=grid, in_specs=in_specs, out_specs=out_specs, out_shape=out_shape,
        scratch_shapes=[pltpu.VMEM((nb, tt + SUBLANES, CONV_DIM), F32),
                        pltpu.VMEM((nb, tt + SUBLANES, RWKV_PROJ), F32)],
        compiler_params=pltpu.CompilerParams(
            dimension_semantics=("arbitrary", "arbitrary"), vmem_limit_bytes=VMEM_LIMIT),
        name="proj",
    )(x, conv_buf, shift_prev, *params, *cast_srcs)


def _unit_lower_inverse(lows, c):
    base = min(INV_BASE, c)
    nblk = c // base
    prow = lax.broadcasted_iota(jnp.int32, (base, c), 0)
    pcol = lax.broadcasted_iota(jnp.int32, (base, c), 1)
    lane_blk = pcol // base
    eye_p = (pcol - lane_blk * base == prow).astype(F32)

    def pack(m):
        return sum(jnp.where(lane_blk == i, m[i * base:(i + 1) * base, :], 0.0)
                   for i in range(nblk))

    def expand(p):
        return jnp.concatenate([jnp.where(lane_blk == i, p, 0.0) for i in range(nblk)], axis=0)

    ps = [pack(low) for low in lows]
    invs = [eye_p - p for p in ps]
    pes = [expand(p) for p in ps]
    span = 2
    while span < base:
        ps = [_dot1(p, pe) for p, pe in zip(ps, pes)]
        pes = [expand(p) for p in ps]
        invs = [inv + _dot1(inv, pe) for inv, pe in zip(invs, pes)]
        span *= 2
    invs = [expand(inv) for inv in invs]

    row = lax.broadcasted_iota(jnp.int32, (c, c), 0)
    col = lax.broadcasted_iota(jnp.int32, (c, c), 1)
    blk = base
    while blk < c:
        sel = ((row // (2 * blk)) == (col // (2 * blk))) & ((row // blk) != (col // blk))
        odd = [slice((2 * b + 1) * blk, (2 * b + 2) * blk) for b in range(c // (2 * blk))]
        zero = jnp.zeros((blk, c), F32)

        def take(m):
            return jnp.concatenate([m[s] for s in odd], axis=0)

        def place(mr):
            parts = []
            for b in range(len(odd)):
                parts += [zero, mr[b * blk:(b + 1) * blk]]
            return jnp.concatenate(parts, axis=0)

        tmp = [place(_dot1(take(jnp.where(sel, low, 0.0)), inv)) for low, inv in zip(lows, invs)]
        invs = [inv - place(_dot1(take(inv), t)) for inv, t in zip(invs, tmp)]
        blk *= 2
    return invs


def _wkv_kernel(r_ref, k_ref, v_ref, kk_ref, b_ref, lw_ref, g_ref, bg_ref, s0_ref,
                gain_ref, bias_ref, y_ref, sout_ref, s_scr, *, nb, c, nc):
    pair = 2 * HEAD_SIZE
    zpad = jnp.zeros((HEAD_SIZE, HEAD_SIZE), F32)

    @pl.when(pl.program_id(1) == 0)
    def _():
        for i in range(nb):
            for hd in range(N_HEADS):
                s0 = s0_ref[i, hd]
                s_scr[i, hd] = jnp.concatenate([s0, zpad] if hd % 2 == 0 else [zpad, s0], axis=1)

    row = lax.broadcasted_iota(jnp.int32, (c, c), 0)
    col = lax.broadcasted_iota(jnp.int32, (c, c), 1)
    strict = row > col
    incl = row >= col
    row2 = lax.broadcasted_iota(jnp.int32, (c, 2 * c), 0)
    col2 = lax.broadcasted_iota(jnp.int32, (c, 2 * c), 1)
    incl2 = row2 >= jnp.where(col2 >= c, col2 - c, col2)
    tri = incl.astype(BF16)
    lane_half = lax.broadcasted_iota(jnp.int32, (c, pair), 1) // HEAD_SIZE
    own = [lane_half == 0, lane_half == 1]

    xl, xr, xe, vh, pe = [], [], [], [], []
    for q in range(nc):
        rows = slice(q * c, (q + 1) * c)
        for i in range(nb):
            logw = lw_ref[i, rows, :]
            cum = sum(_dg(tri, t) for t in _bf16_terms(logw, 2))
            cum_end = cum[c - 1:c, :]
            k = k_ref[i, rows, :]
            b = b_ref[i, rows, :]
            e_neg = jnp.exp(-cum)
            rt = r_ref[i, rows, :] * jnp.exp(cum)
            at = kk_ref[i, rows, :] * jnp.exp(cum - logw)
            bt = b * e_neg
            kt = k * e_neg
            d_end = jnp.exp(cum_end - cum)
            bh = b * d_end
            kh = k * d_end
            p_end = jnp.exp(cum_end)
            v = v_ref[i, rows, :]
            for hd in range(N_HEADS):
                lp = slice((hd // 2) * pair, (hd // 2 + 1) * pair)
                m = own[hd % 2]
                xl.append(jnp.concatenate([jnp.where(m, at[:, lp], 0.0),
                                           jnp.where(m, rt[:, lp], 0.0)], axis=0))
                xr.append(jnp.concatenate([bt[:, lp], kt[:, lp]], axis=0))
                xe.append(jnp.concatenate([jnp.where(m, bh[:, lp], 0.0),
                                           jnp.where(m, kh[:, lp], 0.0)], axis=0).astype(BF16))
                vh.append(v[:, hd * HEAD_SIZE:(hd + 1) * HEAD_SIZE])
                pe.append(p_end[:, lp])
    n = len(xl)

    gram = [_dot1(xl[j], xr[j], _NT) for j in range(n)]
    tinv = _unit_lower_inverse([jnp.where(strict, g[:c, :c], 0.0) for g in gram], c)
    yk = [_dot1(jnp.where(strict, gram[j][:c, c:], 0.0), vh[j]) for j in range(n)]
    wu = [_dot1(tinv[j], jnp.concatenate([xl[j][:c], -yk[j]], axis=1)) for j in range(n)]
    w_b = [wu[j][:, :pair].astype(BF16) for j in range(n)]
    lg = [jnp.concatenate([w_b[j], xl[j][c:].astype(BF16)], axis=0) for j in range(n)]
    u0 = [wu[j][:, pair:] for j in range(n)]
    u0t = [u.T for u in u0]
    vt = [x.T.astype(BF16) for x in vh]
    vb = [x.astype(BF16) for x in vh]
    m_r = [jnp.where(incl2, g[c:, :], 0.0).astype(BF16) for g in gram]

    nh = nb * N_HEADS
    st = [s_scr[jj // N_HEADS, jj % N_HEADS] for jj in range(nh)]
    gain = gain_ref[...]
    bias = bias_ref[...]
    for q in range(nc):
        rows = slice(q * c, (q + 1) * c)
        js = [q * nh + jj for jj in range(nh)]
        sb = [s.astype(BF16) for s in st]
        ut = [u0t[j] - _dg(sb[jj], w_b[j], _NT) for jj, j in enumerate(js)]
        gs = [_dg(lg[j], sb[jj], _NT) for jj, j in enumerate(js)]
        st = [st[jj] * pe[j]
              + _dg(jnp.concatenate([ut[jj].astype(BF16), vt[j]], axis=1), xe[j])
              for jj, j in enumerate(js)]
        uv = [jnp.concatenate([(u0[j] - gs[jj][:c]).astype(BF16), vb[j]], axis=0)
              for jj, j in enumerate(js)]
        o = [gs[jj][c:] + _dg(m_r[j], uv[jj]) for jj, j in enumerate(js)]
        mu = [jnp.mean(x, axis=-1, keepdims=True) for x in o]
        dev = [x - m for x, m in zip(o, mu)]
        var = [jnp.mean(jnp.square(d), axis=-1, keepdims=True) for d in dev]
        on = [d * lax.rsqrt(s + GN_EPS) for d, s in zip(dev, var)]
        for jj in range(0, nh, 2):
            i, hd = jj // N_HEADS, jj % N_HEADS
            lp = slice(hd * HEAD_SIZE, (hd + 2) * HEAD_SIZE)
            on_pair = jnp.concatenate([on[jj], on[jj + 1]], axis=1)
            y_ref[i, rows, lp] = ((on_pair * gain[:, lp] + bias[:, lp]) * g_ref[i, rows, lp]
                                  + bg_ref[i, rows, lp])

    for jj in range(nh):
        s_scr[jj // N_HEADS, jj % N_HEADS] = st[jj]

    @pl.when(pl.program_id(1) == pl.num_programs(1) - 1)
    def _():
        for jj in range(nh):
            hd = jj % N_HEADS
            sout_ref[jj // N_HEADS, hd] = st[jj][:, (hd % 2) * HEAD_SIZE:(hd % 2 + 1) * HEAD_SIZE]


def _wkv_call(seqs, s0, gain, bias, *, nb, c, nc):
    B, T, _ = seqs[0].shape
    tok = pl.BlockSpec((nb, nc * c, RWKV_DIM), lambda b, t: (b, t, 0))
    st = pl.BlockSpec((nb, N_HEADS, HEAD_SIZE, HEAD_SIZE), lambda b, t: (b, 0, 0, 0))
    vec = pl.BlockSpec((1, RWKV_DIM), lambda b, t: (0, 0))
    return pl.pallas_call(
        functools.partial(_wkv_kernel, nb=nb, c=c, nc=nc),
        grid=(B // nb, T // (nc * c)),
        in_specs=[tok] * 8 + [st, vec, vec],
        out_specs=[tok, st],
        out_shape=[jax.ShapeDtypeStruct((B, T, RWKV_DIM), F32),
                   jax.ShapeDtypeStruct((B, N_HEADS, HEAD_SIZE, HEAD_SIZE), F32)],
        scratch_shapes=[pltpu.VMEM((nb, N_HEADS, HEAD_SIZE, 2 * HEAD_SIZE), F32)],
        compiler_params=pltpu.CompilerParams(
            dimension_semantics=("arbitrary", "arbitrary"), vmem_limit_bytes=VMEM_LIMIT),
        name="wkv",
    )(*seqs, s0, gain, bias)


def _out_stream(rows, x_ref, yc_ref, yr_ref, wout_ref, gpost_ref, gfpre_ref, gfpost_ref,
                wff1_ref, wff2_ref, o_ref):
    wout = wout_ref[...]
    mix = (_dg(yc_ref[rows, :].astype(BF16), wout[:CONV_DIM])
           + _dg(yr_ref[rows, :].astype(BF16), wout[CONV_DIM:]))
    yield
    x1 = x_ref[rows, :] + _rmsnorm(mix, gpost_ref[...])
    h2 = _rmsnorm(x1, gfpre_ref[...]).astype(BF16)
    yield
    f1 = _dg(h2, wff1_ref[...])
    yield
    f1 = jnp.square(jnp.maximum(f1, 0.0)).astype(BF16)
    yield
    f2 = _dg(f1, wff2_ref[...])
    yield
    o_ref[rows, :] = x1 + _rmsnorm(f2, gfpost_ref[...])


def _out_kernel(*refs, tm, ns):
    per = tm // ns
    _interleave([_out_stream(slice(i * per, (i + 1) * per), *refs) for i in range(ns)])


def _out_call(x, yc, yr, params, *, tm, ns):
    n = x.shape[0]
    tok = lambda w: pl.BlockSpec((tm, w), lambda i: (i, 0))
    full = lambda arr: pl.BlockSpec(arr.shape, lambda i: (0,) * arr.ndim,
                                    pipeline_mode=pl.Buffered(1))
    return pl.pallas_call(
        functools.partial(_out_kernel, tm=tm, ns=ns),
        grid=(n // tm,),
        in_specs=[tok(D_MODEL), tok(CONV_DIM), tok(RWKV_DIM)] + [full(p) for p in params],
        out_specs=tok(D_MODEL),
        out_shape=jax.ShapeDtypeStruct((n, D_MODEL), F32),
        compiler_params=pltpu.CompilerParams(
            dimension_semantics=("arbitrary",), vmem_limit_bytes=VMEM_LIMIT),
        name="outffn",
    )(x, yc, yr, *params)


def _tiles(B, T):
    if T >= 256:
        return 2, 256, 4, 2, 64, 4, 512, 2
    return B, T, 1, 4, T, 1, B * T, 1


def _layer(x, conv_buf, shift_prev, wkv_state, proj_params, gn, norms, out_w):
    B, T, _ = x.shape
    pnb, ptt, pns, wnb, wc, wnc, tm, ons = _tiles(B, T)
    cast = [w for w in out_w if w.dtype != BF16]
    outs = _proj_call(x, conv_buf, shift_prev.reshape(B, 1, RWKV_PROJ), proj_params, cast,
                      nb=pnb, tt=ptt, ns=pns)
    yconv, seqs, new_conv, new_shift = outs[0], outs[1:9], outs[9], outs[10]
    if cast:
        out_w = tuple(outs[11:])
    yr, new_wkv = _wkv_call(seqs, wkv_state, *gn, nb=wnb, c=wc, nc=wnc)
    out_params = (out_w[0], *norms, out_w[1], out_w[2])
    y = _out_call(x.reshape(B * T, D_MODEL), yconv.reshape(B * T, CONV_DIM),
                  yr.reshape(B * T, RWKV_DIM), out_params, tm=tm, ns=ons)
    return (y.reshape(B, T, D_MODEL), new_conv, new_shift.reshape(B, RWKV_PROJ), new_wkv,
            out_w)


def kernel(x_prompt, x_sample, state_conv, state_shift, state_wkv, norm_mix_pre, norm_mix_post,
           norm_ffn_pre, norm_ffn_post, w_in, conv_w, shift_mu, w_decay2, decay_w0, w_a2, a0,
           w_g2, k_k, k_a, r_k, gn_gain, gn_bias, w_out, w_ff1, w_ff2):
    depth = w_in.shape[0]
    Bp = x_prompt.shape[0]
    xp, xs = x_prompt, x_sample
    row = lambda t: t.reshape(1, -1).astype(F32)
    res = [[] for _ in range(6)]
    for l in range(depth):
        proj_params = (
            row(norm_mix_pre[l]), w_in[l].astype(BF16), conv_w[l], row(shift_mu[l]),
            w_decay2[l], row(decay_w0[l]), w_a2[l], row(a0[l]), w_g2[l],
            row(k_k[l]), row(k_a[l]), row(r_k[l]))
        gn = (row(gn_gain[l]), row(gn_bias[l]))
        norms = (row(norm_mix_post[l]), row(norm_ffn_pre[l]), row(norm_ffn_post[l]))
        xp, c_p, s_p, w_p, out_w = _layer(
            xp, jnp.zeros((Bp, CONV_K - 1, CONV_DIM), F32), jnp.zeros((Bp, RWKV_PROJ), F32),
            jnp.zeros((Bp, N_HEADS, HEAD_SIZE, HEAD_SIZE), F32), proj_params, gn, norms,
            (w_out[l], w_ff1[l], w_ff2[l]))
        xs, c_s, s_s, w_s, _ = _layer(xs, state_conv[l], state_shift[l], state_wkv[l],
                                      proj_params, gn, norms, out_w)
        for lst, val in zip(res, (c_p, s_p, w_p, c_s, s_s, w_s)):
            lst.append(val)
    return (xp, xs) + tuple(jnp.stack(r) for r in res)
```

```python
import functools

import jax
import jax.numpy as jnp
import numpy as np
from jax import lax
from jax.experimental import pallas as pl
from jax.experimental.pallas import tpu as pltpu

D_MODEL = 1024
CONV_DIM = 512
RWKV_DIM = 512
HEAD_SIZE = 64
N_HEADS = 8
CONV_K = 3
DECAY_LORA = 64
AAA_LORA = 64
GATE_LORA = 128
RWKV_PROJ = 3 * RWKV_DIM + DECAY_LORA + AAA_LORA + GATE_LORA
PROJ_DIM = 3 * CONV_DIM + RWKV_PROJ
D_FF = 4 * D_MODEL
NORM_EPS = 1e-6
GN_EPS = 64e-5
DECAY_SCALE = float(np.exp(-0.5))

SUBLANES = 8
INV_BASE = 16
VMEM_LIMIT = 56 * 1024 * 1024

F32 = jnp.float32
BF16 = jnp.bfloat16

_NN = (((1,), (0,)), ((), ()))
_NT = (((1,), (1,)), ((), ()))


def _dg(a, b, dims=_NN):
    return lax.dot_general(a, b, dims, preferred_element_type=F32)


def _dot1(a, b, dims=_NN):
    return _dg(a.astype(BF16), b.astype(BF16), dims)


def _bf16_terms(a, n):
    terms = []
    for _ in range(n - 1):
        t = a.astype(BF16)
        terms.append(t)
        a = a - t.astype(F32)
    terms.append(a.astype(BF16))
    return terms


def _sigmoid(x):
    return 0.5 * jnp.tanh(0.5 * x) + 0.5


def _interleave(streams):
    live = list(streams)
    while live:
        live = [s for s in live if next(s, True) is None]


def _head_sums(x):
    pair = 2 * HEAD_SIZE
    low = lax.broadcasted_iota(jnp.int32, (x.shape[0], pair), 1) < HEAD_SIZE
    tiles = []
    for p in range(RWKV_DIM // pair):
        t = x[:, p * pair:(p + 1) * pair]
        s_lo = jnp.sum(jnp.where(low, t, 0.0), axis=-1, keepdims=True)
        s_hi = jnp.sum(jnp.where(low, 0.0, t), axis=-1, keepdims=True)
        tiles.append(jnp.where(low, s_lo, s_hi))
    return jnp.concatenate(tiles, axis=1)


def _rmsnorm(x, g):
    return x * lax.rsqrt(jnp.mean(x * x, axis=-1, keepdims=True) + NORM_EPS) * g


def _proj_stream(sq, rq, tt, x_ref, gpre_ref, win_ref, convw_ref, mu_ref, wd_ref, w0_ref, wa_ref,
                 a0_ref, wg_ref, kk_ref, ka_ref, rk_ref, yconv_ref, r_ref, k_ref,
                 v_ref, kkn_ref, b_ref, lw_ref, g_ref, bg_ref, nconv_ref, nshift_ref, ubuf, zbuf):
    nbs = sq.stop - sq.start
    rn = rq.stop - rq.start
    m = nbs * rn
    lo = SUBLANES + rq.start
    hi = lo + rn

    x = x_ref[sq, rq].reshape(m, D_MODEL)
    h = _rmsnorm(x, gpre_ref[...]).astype(BF16)
    zr = _dg(h, win_ref[:, 3 * CONV_DIM:])
    yield

    zbuf[sq, lo:hi, :] = zr.reshape(nbs, rn, RWKV_PROJ)
    zprev = zbuf[sq, lo - 1:hi - 1, :].reshape(m, RWKV_PROJ)
    zs = zr + mu_ref[...] * (zprev - zr)
    if rq.stop == tt:
        for i in range(sq.start, sq.stop):
            last_z = zbuf[i, hi - 1:hi, :]
            nshift_ref[i:i + 1, :] = last_z
            zbuf[i, SUBLANES - 1:SUBLANES, :] = last_z

    r = zs[:, :RWKV_DIM]
    k = zs[:, RWKV_DIM:2 * RWKV_DIM]
    v = zs[:, 2 * RWKV_DIM:3 * RWKV_DIM]
    lwa = zs[:, 3 * RWKV_DIM:3 * RWKV_DIM + DECAY_LORA + AAA_LORA]
    lg = zs[:, 3 * RWKV_DIM + DECAY_LORA + AAA_LORA:]
    logit_w = w0_ref[...] + _dot1(jnp.tanh(lwa[:, :DECAY_LORA]), wd_ref[...])
    a_pre = a0_ref[...] + _dot1(lwa[:, DECAY_LORA:], wa_ref[...])
    g = _dot1(_sigmoid(lg), wg_ref[...])
    zcv = _dg(h, win_ref[:, :3 * CONV_DIM])
    yield

    logw = -DECAY_SCALE * _sigmoid(logit_w)
    a = _sigmoid(a_pre)
    zb = zcv[:, :CONV_DIM]
    ubuf[sq, lo:hi, :] = (zcv[:, CONV_DIM:2 * CONV_DIM] * zcv[:, 2 * CONV_DIM:]).reshape(
        nbs, rn, CONV_DIM)
    cw = convw_ref[...]
    yc = (ubuf[sq, lo - 2:hi - 2, :] * cw[0:1, :]
          + ubuf[sq, lo - 1:hi - 1, :] * cw[1:2, :]
          + ubuf[sq, lo:hi, :] * cw[2:3, :])
    yconv_ref[sq, rq] = (zb.reshape(nbs, rn, CONV_DIM) * yc)
    if rq.stop == tt:
        last_u = ubuf[sq, hi - 2:hi, :]
        nconv_ref[sq] = last_u
        ubuf[sq, SUBLANES - 2:SUBLANES, :] = last_u
    yield

    kk = k * kk_ref[...]
    ss = _head_sums(kk * kk)
    kp = k * (1.0 + (a - 1.0) * ka_ref[...])
    bsum = _head_sums(r * kp * rk_ref[...])
    yield

    kk = kk * lax.rsqrt(jnp.maximum(ss, 1e-24))
    shp = (nbs, rn, RWKV_DIM)
    r_ref[sq, rq] = r.reshape(shp)
    k_ref[sq, rq] = kp.reshape(shp)
    v_ref[sq, rq] = v.reshape(shp)
    kkn_ref[sq, rq] = kk.reshape(shp)
    b_ref[sq, rq] = (kk * a).reshape(shp)
    lw_ref[sq, rq] = logw.reshape(shp)
    g_ref[sq, rq] = g.reshape(shp)
    bg_ref[sq, rq] = (bsum * v * g).reshape(shp)


def _proj_kernel(x_ref, cbuf_ref, sprev_ref, *rest, nb, tt, ns, ncast):
    ubuf, zbuf = rest[-2:]
    params, cast_in = rest[:12], rest[12:12 + ncast]
    outs, cast_out = rest[12 + ncast:23 + ncast], rest[23 + ncast:23 + 2 * ncast]

    @pl.when(pl.program_id(1) == 0)
    def _():
        ubuf[:, SUBLANES - 2:SUBLANES, :] = cbuf_ref[...]
        for i in range(nb):
            zbuf[i, SUBLANES - 1:SUBLANES, :] = sprev_ref[i:i + 1, :]

    for src, dst in zip(cast_in, cast_out):
        dst[...] = src[...].astype(BF16)

    nseq = min(ns, nb)
    per, rn = nb // nseq, tt // (ns // nseq)
    _interleave([_proj_stream(slice(i * per, (i + 1) * per), slice(r0, r0 + rn), tt, x_ref,
                              *params, *outs, ubuf, zbuf)
                 for i in range(nseq) for r0 in range(0, tt, rn)])


def _proj_call(x, conv_buf, shift_prev, params, cast_srcs, *, nb, tt, ns):
    B, T, _ = x.shape
    nt = T // tt
    grid = (B // nb, nt)
    steps = grid[0] * nt
    tok = lambda w: pl.BlockSpec((nb, tt, w), lambda b, t: (b, t, 0))
    full = lambda arr: pl.BlockSpec(arr.shape, lambda b, t: (0,) * arr.ndim)
    rows = lambda arr: pl.BlockSpec((arr.shape[0] // steps, arr.shape[1]),
                                    lambda b, t: (b * nt + t, 0))
    in_specs = [tok(D_MODEL),
                pl.BlockSpec((nb, CONV_K - 1, CONV_DIM), lambda b, t: (b, 0, 0)),
                pl.BlockSpec((nb, RWKV_PROJ), lambda b, t: (b, 0))]
    in_specs += [full(p) for p in params] + [rows(w) for w in cast_srcs]
    seq = lambda w: jax.ShapeDtypeStruct((B, T, w), F32)
    out_shape = [seq(CONV_DIM)] + [seq(RWKV_DIM)] * 8 + [
        jax.ShapeDtypeStruct((B, CONV_K - 1, CONV_DIM), F32),
        jax.ShapeDtypeStruct((B, RWKV_PROJ), F32)]
    out_shape += [jax.ShapeDtypeStruct(w.shape, BF16) for w in cast_srcs]
    out_specs = [tok(CONV_DIM)] + [tok(RWKV_DIM)] * 8 + [
        pl.BlockSpec((nb, CONV_K - 1, CONV_DIM), lambda b, t: (b, 0, 0)),
        pl.BlockSpec((nb, RWKV_PROJ), lambda b, t: (b, 0))]
    out_specs += [rows(w) for w in cast_srcs]
    return pl.pallas_call(
        functools.partial(_proj_kernel, nb=nb, tt=tt, ns=ns, ncast=len(cast_srcs)),
        grid=grid, in_specs=in_specs, out_specs=out_specs, out_shape=out_shape,
        scratch_shapes=[pltpu.VMEM((nb, tt + SUBLANES, CONV_DIM), F32),
                        pltpu.VMEM((nb, tt + SUBLANES, RWKV_PROJ), F32)],
        compiler_params=pltpu.CompilerParams(
            dimension_semantics=("arbitrary", "arbitrary"), vmem_limit_bytes=VMEM_LIMIT),
        name="proj",
    )(x, conv_buf, shift_prev, *params, *cast_srcs)


def _unit_lower_inverse(lows, c):
    base = min(INV_BASE, c)
    nblk = c // base
    prow = lax.broadcasted_iota(jnp.int32, (base, c), 0)
    pcol = lax.broadcasted_iota(jnp.int32, (base, c), 1)
    lane_blk = pcol // base
    eye_p = (pcol - lane_blk * base == prow).astype(F32)

    def pack(m):
        return sum(jnp.where(lane_blk == i, m[i * base:(i + 1) * base, :], 0.0)
                   for i in range(nblk))

    def expand(p):
        return jnp.concatenate([jnp.where(lane_blk == i, p, 0.0) for i in range(nblk)], axis=0)

    ps = [pack(low) for low in lows]
    invs = [eye_p - p for p in ps]
    pes = [expand(p) for p in ps]
    span = 2
    while span < base:
        ps = [_dot1(p, pe) for p, pe in zip(ps, pes)]
        pes = [expand(p) for p in ps]
        invs = [inv + _dot1(inv, pe) for inv, pe in zip(invs, pes)]
        span *= 2
    invs = [expand(inv) for inv in invs]

    row = lax.broadcasted_iota(jnp.int32, (c, c), 0)
    col = lax.broadcasted_iota(jnp.int32, (c, c), 1)
    blk = base
    while blk < c:
        sel = ((row // (2 * blk)) == (col // (2 * blk))) & ((row // blk) != (col // blk))
        odd = [slice((2 * b + 1) * blk, (2 * b + 2) * blk) for b in range(c // (2 * blk))]
        zero = jnp.zeros((blk, c), F32)

        def take(m):
            return jnp.concatenate([m[s] for s in odd], axis=0)

        def place(mr):
            parts = []
            for b in range(len(odd)):
                parts += [zero, mr[b * blk:(b + 1) * blk]]
            return jnp.concatenate(parts, axis=0)

        tmp = [place(_dot1(take(jnp.where(sel, low, 0.0)), inv)) for low, inv in zip(lows, invs)]
        invs = [inv - place(_dot1(take(inv), t)) for inv, t in zip(invs, tmp)]
        blk *= 2
    return invs


def _wkv_kernel(r_ref, k_ref, v_ref, kk_ref, b_ref, lw_ref, g_ref, bg_ref, s0_ref,
                gain_ref, bias_ref, y_ref, sout_ref, s_scr, *, nb, c, nc):
    pair = 2 * HEAD_SIZE
    zpad = jnp.zeros((HEAD_SIZE, HEAD_SIZE), F32)

    @pl.when(pl.program_id(1) == 0)
    def _():
        for i in range(nb):
            for hd in range(N_HEADS):
                s0 = s0_ref[i, hd]
                s_scr[i, hd] = jnp.concatenate([s0, zpad] if hd % 2 == 0 else [zpad, s0], axis=1)

    row = lax.broadcasted_iota(jnp.int32, (c, c), 0)
    col = lax.broadcasted_iota(jnp.int32, (c, c), 1)
    strict = row > col
    incl = row >= col
    row2 = lax.broadcasted_iota(jnp.int32, (c, 2 * c), 0)
    col2 = lax.broadcasted_iota(jnp.int32, (c, 2 * c), 1)
    incl2 = row2 >= jnp.where(col2 >= c, col2 - c, col2)
    tri = incl.astype(BF16)
    lane_half = lax.broadcasted_iota(jnp.int32, (c, pair), 1) // HEAD_SIZE
    own = [lane_half == 0, lane_half == 1]

    xl, xr, xe, vh, pe = [], [], [], [], []
    for q in range(nc):
        rows = slice(q * c, (q + 1) * c)
        for i in range(nb):
            logw = lw_ref[i, rows, :]
            cum = sum(_dg(tri, t) for t in _bf16_terms(logw, 2))
            cum_end = cum[c - 1:c, :]
            k = k_ref[i, rows, :]
            b = b_ref[i, rows, :]
            e_neg = jnp.exp(-cum)
            rt = r_ref[i, rows, :] * jnp.exp(cum)
            at = kk_ref[i, rows, :] * jnp.exp(cum - logw)
            bt = b * e_neg
            kt = k * e_neg
            d_end = jnp.exp(cum_end - cum)
            bh = b * d_end
            kh = k * d_end
            p_end = jnp.exp(cum_end)
            v = v_ref[i, rows, :]
            for hd in range(N_HEADS):
                lp = slice((hd // 2) * pair, (hd // 2 + 1) * pair)
                m = own[hd % 2]
                xl.append(jnp.concatenate([jnp.where(m, at[:, lp], 0.0),
                                           jnp.where(m, rt[:, lp], 0.0)], axis=0))
                xr.append(jnp.concatenate([bt[:, lp], kt[:, lp]], axis=0))
                xe.append(jnp.concatenate([jnp.where(m, bh[:, lp], 0.0),
                                           jnp.where(m, kh[:, lp], 0.0)], axis=0).astype(BF16))
                vh.append(v[:, hd * HEAD_SIZE:(hd + 1) * HEAD_SIZE])
                pe.append(p_end[:, lp])
    n = len(xl)

    gram = [_dot1(xl[j], xr[j], _NT) for j in range(n)]
    tinv = _unit_lower_inverse([jnp.where(strict, g[:c, :c], 0.0) for g in gram], c)
    yk = [_dot1(jnp.where(strict, gram[j][:c, c:], 0.0), vh[j]) for j in range(n)]
    wu = [_dot1(tinv[j], jnp.concatenate([xl[j][:c], -yk[j]], axis=1)) for j in range(n)]
    w_b = [wu[j][:, :pair].astype(BF16) for j in range(n)]
    lg = [jnp.concatenate([w_b[j], xl[j][c:].astype(BF16)], axis=0) for j in range(n)]
    u0 = [wu[j][:, pair:] for j in range(n)]
    u0t = [u.T for u in u0]
    vt = [x.T.astype(BF16) for x in vh]
    vb = [x.astype(BF16) for x in vh]
    m_r = [jnp.where(incl2, g[c:, :], 0.0).astype(BF16) for g in gram]

    nh = nb * N_HEADS
    st = [s_scr[jj // N_HEADS, jj % N_HEADS] for jj in range(nh)]
    gain = gain_ref[...]
    bias = bias_ref[...]
    for q in range(nc):
        rows = slice(q * c, (q + 1) * c)
        js = [q * nh + jj for jj in range(nh)]
        sb = [s.astype(BF16) for s in st]
        ut = [u0t[j] - _dg(sb[jj], w_b[j], _NT) for jj, j in enumerate(js)]
        gs = [_dg(lg[j], sb[jj], _NT) for jj, j in enumerate(js)]
        st = [st[jj] * pe[j]
              + _dg(jnp.concatenate([ut[jj].astype(BF16), vt[j]], axis=1), xe[j])
              for jj, j in enumerate(js)]
        uv = [jnp.concatenate([(u0[j] - gs[jj][:c]).astype(BF16), vb[j]], axis=0)
              for jj, j in enumerate(js)]
        o = [gs[jj][c:] + _dg(m_r[j], uv[jj]) for jj, j in enumerate(js)]
        mu = [jnp.mean(x, axis=-1, keepdims=True) for x in o]
        dev = [x - m for x, m in zip(o, mu)]
        var = [jnp.mean(jnp.square(d), axis=-1, keepdims=True) for d in dev]
        on = [d * lax.rsqrt(s + GN_EPS) for d, s in zip(dev, var)]
        for jj in range(0, nh, 2):
            i, hd = jj // N_HEADS, jj % N_HEADS
            lp = slice(hd * HEAD_SIZE, (hd + 2) * HEAD_SIZE)
            on_pair = jnp.concatenate([on[jj], on[jj + 1]], axis=1)
            y_ref[i, rows, lp] = ((on_pair * gain[:, lp] + bias[:, lp]) * g_ref[i, rows, lp]
                                  + bg_ref[i, rows, lp])

    for jj in range(nh):
        s_scr[jj // N_HEADS, jj % N_HEADS] = st[jj]

    @pl.when(pl.program_id(1) == pl.num_programs(1) - 1)
    def _():
        for jj in range(nh):
            hd = jj % N_HEADS
            sout_ref[jj // N_HEADS, hd] = st[jj][:, (hd % 2) * HEAD_SIZE:(hd % 2 + 1) * HEAD_SIZE]


def _wkv_call(seqs, s0, gain, bias, *, nb, c, nc):
    B, T, _ = seqs[0].shape
    tok = pl.BlockSpec((nb, nc * c, RWKV_DIM), lambda b, t: (b, t, 0))
    st = pl.BlockSpec((nb, N_HEADS, HEAD_SIZE, HEAD_SIZE), lambda b, t: (b, 0, 0, 0))
    vec = pl.BlockSpec((1, RWKV_DIM), lambda b, t: (0, 0))
    return pl.pallas_call(
        functools.partial(_wkv_kernel, nb=nb, c=c, nc=nc),
        grid=(B // nb, T // (nc * c)),
        in_specs=[tok] * 8 + [st, vec, vec],
        out_specs=[tok, st],
        out_shape=[jax.ShapeDtypeStruct((B, T, RWKV_DIM), F32),
                   jax.ShapeDtypeStruct((B, N_HEADS, HEAD_SIZE, HEAD_SIZE), F32)],
        scratch_shapes=[pltpu.VMEM((nb, N_HEADS, HEAD_SIZE, 2 * HEAD_SIZE), F32)],
        compiler_params=pltpu.CompilerParams(
            dimension_semantics=("arbitrary", "arbitrary"), vmem_limit_bytes=VMEM_LIMIT),
        name="wkv",
    )(*seqs, s0, gain, bias)


def _out_stream(rows, x_ref, yc_ref, yr_ref, wout_ref, gpost_ref, gfpre_ref, gfpost_ref,
                wff1_ref, wff2_ref, o_ref):
    wout = wout_ref[...]
    mix = (_dg(yc_ref[rows, :].astype(BF16), wout[:CONV_DIM])
           + _dg(yr_ref[rows, :].astype(BF16), wout[CONV_DIM:]))
    yield
    x1 = x_ref[rows, :] + _rmsnorm(mix, gpost_ref[...])
    h2 = _rmsnorm(x1, gfpre_ref[...]).astype(BF16)
    yield
    f1 = _dg(h2, wff1_ref[...])
    yield
    f1 = jnp.square(jnp.maximum(f1, 0.0)).astype(BF16)
    yield
    f2 = _dg(f1, wff2_ref[...])
    yield
    o_ref[rows, :] = x1 + _rmsnorm(f2, gfpost_ref[...])


def _out_kernel(*refs, tm, ns):
    per = tm // ns
    _interleave([_out_stream(slice(i * per, (i + 1) * per), *refs) for i in range(ns)])


def _out_call(x, yc, yr, params, *, tm, ns):
    n = x.shape[0]
    tok = lambda w: pl.BlockSpec((tm, w), lambda i: (i, 0))
    full = lambda arr: pl.BlockSpec(arr.shape, lambda i: (0,) * arr.ndim,
                                    pipeline_mode=pl.Buffered(1))
    return pl.pallas_call(
        functools.partial(_out_kernel, tm=tm, ns=ns),
        grid=(n // tm,),
        in_specs=[tok(D_MODEL), tok(CONV_DIM), tok(RWKV_DIM)] + [full(p) for p in params],
        out_specs=tok(D_MODEL),
        out_shape=jax.ShapeDtypeStruct((n, D_MODEL), F32),
        compiler_params=pltpu.CompilerParams(
            dimension_semantics=("arbitrary",), vmem_limit_bytes=VMEM_LIMIT),
        name="outffn",
    )(x, yc, yr, *params)


def _tiles(B, T):
    if T >= 256:
        return 2, 256, 2, 2, 64, 4, 512, 2
    return B, T, 1, 4, T, 1, B * T, 1


def _layer(x, conv_buf, shift_prev, wkv_state, proj_params, gn, norms, out_w):
    B, T, _ = x.shape
    pnb, ptt, pns, wnb, wc, wnc, tm, ons = _tiles(B, T)
    cast = [w for w in out_w if w.dtype != BF16]
    outs = _proj_call(x, conv_buf, shift_prev, proj_params, cast,
                      nb=pnb, tt=ptt, ns=pns)
    yconv, seqs, new_conv, new_shift = outs[0], outs[1:9], outs[9], outs[10]
    if cast:
        out_w = tuple(outs[11:])
    yr, new_wkv = _wkv_call(seqs, wkv_state, *gn, nb=wnb, c=wc, nc=wnc)
    out_params = (out_w[0], *norms, out_w[1], out_w[2])
    y = _out_call(x.reshape(B * T, D_MODEL), yconv.reshape(B * T, CONV_DIM),
                  yr.reshape(B * T, RWKV_DIM), out_params, tm=tm, ns=ons)
    return (y.reshape(B, T, D_MODEL), new_conv, new_shift, new_wkv,
            out_w)


def kernel(x_prompt, x_sample, state_conv, state_shift, state_wkv, norm_mix_pre, norm_mix_post,
           norm_ffn_pre, norm_ffn_post, w_in, conv_w, shift_mu, w_decay2, decay_w0, w_a2, a0,
           w_g2, k_k, k_a, r_k, gn_gain, gn_bias, w_out, w_ff1, w_ff2):
    depth = w_in.shape[0]
    Bp = x_prompt.shape[0]
    xp, xs = x_prompt, x_sample
    row = lambda t: t.reshape(1, -1).astype(F32)
    res = [[] for _ in range(6)]
    for l in range(depth):
        proj_params = (
            row(norm_mix_pre[l]), w_in[l].astype(BF16), conv_w[l], row(shift_mu[l]),
            w_decay2[l], row(decay_w0[l]), w_a2[l], row(a0[l]), w_g2[l],
            row(k_k[l]), row(k_a[l]), row(r_k[l]))
        gn = (row(gn_gain[l]), row(gn_bias[l]))
        norms = (row(norm_mix_post[l]), row(norm_ffn_pre[l]), row(norm_ffn_post[l]))
        xp, c_p, s_p, w_p, out_w = _layer(
            xp, jnp.zeros((Bp, CONV_K - 1, CONV_DIM), F32), jnp.zeros((Bp, RWKV_PROJ), F32),
            jnp.zeros((Bp, N_HEADS, HEAD_SIZE, HEAD_SIZE), F32), proj_params, gn, norms,
            (w_out[l], w_ff1[l], w_ff2[l]))
        xs, c_s, s_s, w_s, _ = _layer(xs, state_conv[l], state_shift[l], state_wkv[l],
                                      proj_params, gn, norms, out_w)
        for lst, val in zip(res, (c_p, s_p, w_p, c_s, s_s, w_s)):
            lst.append(val)
    return (xp, xs) + tuple(jnp.stack(r) for r in res)
```

```python
import functools

import jax
import jax.numpy as jnp
import numpy as np
from jax import lax
from jax.experimental import pallas as pl
from jax.experimental.pallas import tpu as pltpu

D_MODEL = 1024
CONV_DIM = 512
RWKV_DIM = 512
HEAD_SIZE = 64
N_HEADS = 8
CONV_K = 3
DECAY_LORA = 64
AAA_LORA = 64
GATE_LORA = 128
RWKV_PROJ = 3 * RWKV_DIM + DECAY_LORA + AAA_LORA + GATE_LORA
PROJ_DIM = 3 * CONV_DIM + RWKV_PROJ
D_FF = 4 * D_MODEL
NORM_EPS = 1e-6
GN_EPS = 64e-5
DECAY_SCALE = float(np.exp(-0.5))

SUBLANES = 8
INV_BASE = 16
VMEM_LIMIT = 56 * 1024 * 1024

F32 = jnp.float32
BF16 = jnp.bfloat16

_NN = (((1,), (0,)), ((), ()))
_NT = (((1,), (1,)), ((), ()))


def _dg(a, b, dims=_NN):
    return lax.dot_general(a, b, dims, preferred_element_type=F32)


def _dot1(a, b, dims=_NN):
    return _dg(a.astype(BF16), b.astype(BF16), dims)


def _bf16_terms(a, n):
    terms = []
    for _ in range(n - 1):
        t = a.astype(BF16)
        terms.append(t)
        a = a - t.astype(F32)
    terms.append(a.astype(BF16))
    return terms


def _sigmoid(x):
    return 0.5 * jnp.tanh(0.5 * x) + 0.5


def _interleave(streams):
    live = list(streams)
    while live:
        live = [s for s in live if next(s, True) is None]


def _head_sums(x):
    pair = 2 * HEAD_SIZE
    low = lax.broadcasted_iota(jnp.int32, (x.shape[0], pair), 1) < HEAD_SIZE
    tiles = []
    for p in range(RWKV_DIM // pair):
        t = x[:, p * pair:(p + 1) * pair]
        s_lo = jnp.sum(jnp.where(low, t, 0.0), axis=-1, keepdims=True)
        s_hi = jnp.sum(jnp.where(low, 0.0, t), axis=-1, keepdims=True)
        tiles.append(jnp.where(low, s_lo, s_hi))
    return jnp.concatenate(tiles, axis=1)


def _rmsnorm(x, g):
    return x * lax.rsqrt(jnp.mean(x * x, axis=-1, keepdims=True) + NORM_EPS) * g


def _proj_stream(sq, rq, tt, x_ref, gpre_ref, win_ref, convw_ref, mu_ref, wd_ref, w0_ref, wa_ref,
                 a0_ref, wg_ref, kk_ref, ka_ref, rk_ref, yconv_ref, r_ref, k_ref,
                 v_ref, kkn_ref, b_ref, lw_ref, g_ref, bg_ref, nconv_ref, nshift_ref, ubuf, zbuf):
    nbs = sq.stop - sq.start
    rn = rq.stop - rq.start
    m = nbs * rn
    lo = SUBLANES + rq.start
    hi = lo + rn

    x = x_ref[sq, rq].reshape(m, D_MODEL)
    h = _rmsnorm(x, gpre_ref[...]).astype(BF16)
    zr = _dg(h, win_ref[:, 3 * CONV_DIM:])
    yield

    zbuf[sq, lo:hi, :] = zr.reshape(nbs, rn, RWKV_PROJ)
    zprev = zbuf[sq, lo - 1:hi - 1, :].reshape(m, RWKV_PROJ)
    zs = zr + mu_ref[...] * (zprev - zr)
    if rq.stop == tt:
        for i in range(sq.start, sq.stop):
            last_z = zbuf[i, hi - 1:hi, :]
            nshift_ref[i:i + 1, :] = last_z
            zbuf[i, SUBLANES - 1:SUBLANES, :] = last_z

    r = zs[:, :RWKV_DIM]
    k = zs[:, RWKV_DIM:2 * RWKV_DIM]
    v = zs[:, 2 * RWKV_DIM:3 * RWKV_DIM]
    lwa = zs[:, 3 * RWKV_DIM:3 * RWKV_DIM + DECAY_LORA + AAA_LORA]
    lg = zs[:, 3 * RWKV_DIM + DECAY_LORA + AAA_LORA:]
    logit_w = w0_ref[...] + _dot1(jnp.tanh(lwa[:, :DECAY_LORA]), wd_ref[...])
    a_pre = a0_ref[...] + _dot1(lwa[:, DECAY_LORA:], wa_ref[...])
    g = _dot1(_sigmoid(lg), wg_ref[...])
    zcv = _dg(h, win_ref[:, :3 * CONV_DIM])
    yield

    logw = -DECAY_SCALE * _sigmoid(logit_w)
    a = _sigmoid(a_pre)
    zb = zcv[:, :CONV_DIM]
    ubuf[sq, lo:hi, :] = (zcv[:, CONV_DIM:2 * CONV_DIM] * zcv[:, 2 * CONV_DIM:]).reshape(
        nbs, rn, CONV_DIM)
    cw = convw_ref[...]
    yc = (ubuf[sq, lo - 2:hi - 2, :] * cw[0:1, :]
          + ubuf[sq, lo - 1:hi - 1, :] * cw[1:2, :]
          + ubuf[sq, lo:hi, :] * cw[2:3, :])
    yconv_ref[sq, rq] = (zb.reshape(nbs, rn, CONV_DIM) * yc)
    if rq.stop == tt:
        last_u = ubuf[sq, hi - 2:hi, :]
        nconv_ref[sq] = last_u
        ubuf[sq, SUBLANES - 2:SUBLANES, :] = last_u
    yield

    kk = k * kk_ref[...]
    ss = _head_sums(kk * kk)
    kp = k * (1.0 + (a - 1.0) * ka_ref[...])
    bsum = _head_sums(r * kp * rk_ref[...])
    yield

    kk = kk * lax.rsqrt(jnp.maximum(ss, 1e-24))
    shp = (nbs, rn, RWKV_DIM)
    r_ref[sq, rq] = r.reshape(shp)
    k_ref[sq, rq] = kp.reshape(shp)
    v_ref[sq, rq] = v.reshape(shp)
    kkn_ref[sq, rq] = kk.reshape(shp)
    b_ref[sq, rq] = (kk * a).reshape(shp)
    lw_ref[sq, rq] = logw.reshape(shp)
    g_ref[sq, rq] = g.reshape(shp)
    bg_ref[sq, rq] = (bsum * v * g).reshape(shp)


def _proj_kernel(x_ref, cbuf_ref, sprev_ref, *rest, nb, tt, ns, ncast):
    ubuf, zbuf = rest[-2:]
    params, cast_in = rest[:12], rest[12:12 + ncast]
    outs, cast_out = rest[12 + ncast:23 + ncast], rest[23 + ncast:23 + 2 * ncast]

    @pl.when(pl.program_id(1) == 0)
    def _():
        ubuf[:, SUBLANES - 2:SUBLANES, :] = cbuf_ref[...]
        for i in range(nb):
            zbuf[i, SUBLANES - 1:SUBLANES, :] = sprev_ref[i:i + 1, :]

    for src, dst in zip(cast_in, cast_out):
        dst[...] = src[...].astype(BF16)

    nseq = min(ns, nb)
    per, rn = nb // nseq, tt // (ns // nseq)
    _interleave([_proj_stream(slice(i * per, (i + 1) * per), slice(r0, r0 + rn), tt, x_ref,
                              *params, *outs, ubuf, zbuf)
                 for i in range(nseq) for r0 in range(0, tt, rn)])


def _proj_call(x, conv_buf, shift_prev, params, cast_srcs, *, nb, tt, ns):
    B, T, _ = x.shape
    nt = T // tt
    grid = (B // nb, nt)
    steps = grid[0] * nt
    tok = lambda w: pl.BlockSpec((nb, tt, w), lambda b, t: (b, t, 0))
    full = lambda arr: pl.BlockSpec(arr.shape, lambda b, t: (0,) * arr.ndim)
    rows = lambda arr: pl.BlockSpec((arr.shape[0] // steps, arr.shape[1]),
                                    lambda b, t: (b * nt + t, 0))
    in_specs = [tok(D_MODEL),
                pl.BlockSpec((nb, CONV_K - 1, CONV_DIM), lambda b, t: (b, 0, 0)),
                pl.BlockSpec((nb, RWKV_PROJ), lambda b, t: (b, 0))]
    in_specs += [full(p) for p in params] + [rows(w) for w in cast_srcs]
    seq = lambda w: jax.ShapeDtypeStruct((B, T, w), F32)
    out_shape = [seq(CONV_DIM)] + [seq(RWKV_DIM)] * 8 + [
        jax.ShapeDtypeStruct((B, CONV_K - 1, CONV_DIM), F32),
        jax.ShapeDtypeStruct((B, RWKV_PROJ), F32)]
    out_shape += [jax.ShapeDtypeStruct(w.shape, BF16) for w in cast_srcs]
    out_specs = [tok(CONV_DIM)] + [tok(RWKV_DIM)] * 8 + [
        pl.BlockSpec((nb, CONV_K - 1, CONV_DIM), lambda b, t: (b, 0, 0)),
        pl.BlockSpec((nb, RWKV_PROJ), lambda b, t: (b, 0))]
    out_specs += [rows(w) for w in cast_srcs]
    return pl.pallas_call(
        functools.partial(_proj_kernel, nb=nb, tt=tt, ns=ns, ncast=len(cast_srcs)),
        grid=grid, in_specs=in_specs, out_specs=out_specs, out_shape=out_shape,
        scratch_shapes=[pltpu.VMEM((nb, tt + SUBLANES, CONV_DIM), F32),
                        pltpu.VMEM((nb, tt + SUBLANES, RWKV_PROJ), F32)],
        compiler_params=pltpu.CompilerParams(
            dimension_semantics=("arbitrary", "arbitrary"), vmem_limit_bytes=VMEM_LIMIT),
        name="proj",
    )(x, conv_buf, shift_prev, *params, *cast_srcs)


def _unit_lower_inverse(lows, c):
    base = min(INV_BASE, c)
    nblk = c // base
    prow = lax.broadcasted_iota(jnp.int32, (base, c), 0)
    pcol = lax.broadcasted_iota(jnp.int32, (base, c), 1)
    lane_blk = pcol // base
    eye_p = (pcol - lane_blk * base == prow).astype(F32)

    def pack(m):
        return sum(jnp.where(lane_blk == i, m[i * base:(i + 1) * base, :], 0.0)
                   for i in range(nblk))

    def expand(p):
        return jnp.concatenate([jnp.where(lane_blk == i, p, 0.0) for i in range(nblk)], axis=0)

    ps = [pack(low) for low in lows]
    invs = [eye_p - p for p in ps]
    ps = [_dot1(p, expand(p)) for p in ps]
    span = 4
    while span < base:
        both = [_dot1(jnp.concatenate([inv, p], axis=0), expand(p)) for inv, p in zip(invs, ps)]
        invs = [inv + b[:base] for inv, b in zip(invs, both)]
        ps = [b[base:] for b in both]
        span *= 2
    invs = [inv + _dot1(inv, expand(p)) for inv, p in zip(invs, ps)]
    invs = [expand(inv) for inv in invs]

    row = lax.broadcasted_iota(jnp.int32, (c, c), 0)
    col = lax.broadcasted_iota(jnp.int32, (c, c), 1)
    blk = base
    while blk < c:
        sel = ((row // (2 * blk)) == (col // (2 * blk))) & ((row // blk) != (col // blk))
        odd = [slice((2 * b + 1) * blk, (2 * b + 2) * blk) for b in range(c // (2 * blk))]
        zero = jnp.zeros((blk, c), F32)

        def take(m):
            return jnp.concatenate([m[s] for s in odd], axis=0)

        def place(mr):
            parts = []
            for b in range(len(odd)):
                parts += [zero, mr[b * blk:(b + 1) * blk]]
            return jnp.concatenate(parts, axis=0)

        tmp = [place(_dot1(take(jnp.where(sel, low, 0.0)), inv)) for low, inv in zip(lows, invs)]
        invs = [inv - place(_dot1(take(inv), t)) for inv, t in zip(invs, tmp)]
        blk *= 2
    return invs


def _wkv_kernel(r_ref, k_ref, v_ref, kk_ref, b_ref, lw_ref, g_ref, bg_ref, s0_ref,
                gain_ref, bias_ref, y_ref, sout_ref, s_scr, *, nb, c, nc):
    pair = 2 * HEAD_SIZE
    zpad = jnp.zeros((HEAD_SIZE, HEAD_SIZE), F32)

    @pl.when(pl.program_id(1) == 0)
    def _():
        for i in range(nb):
            for hd in range(N_HEADS):
                s0 = s0_ref[i, hd]
                s_scr[i, hd] = jnp.concatenate([s0, zpad] if hd % 2 == 0 else [zpad, s0], axis=1)

    row = lax.broadcasted_iota(jnp.int32, (c, c), 0)
    col = lax.broadcasted_iota(jnp.int32, (c, c), 1)
    strict = row > col
    incl = row >= col
    row2 = lax.broadcasted_iota(jnp.int32, (c, 2 * c), 0)
    col2 = lax.broadcasted_iota(jnp.int32, (c, 2 * c), 1)
    incl2 = row2 >= jnp.where(col2 >= c, col2 - c, col2)
    tri = incl.astype(BF16)
    lane_half = lax.broadcasted_iota(jnp.int32, (c, pair), 1) // HEAD_SIZE
    own = [lane_half == 0, lane_half == 1]

    xl, xr, xe, vh, pe = [], [], [], [], []
    for q in range(nc):
        rows = slice(q * c, (q + 1) * c)
        for i in range(nb):
            logw = lw_ref[i, rows, :]
            cum = sum(_dg(tri, t) for t in _bf16_terms(logw, 2))
            cum_end = cum[c - 1:c, :]
            k = k_ref[i, rows, :]
            b = b_ref[i, rows, :]
            e_neg = jnp.exp(-cum)
            rt = r_ref[i, rows, :] * jnp.exp(cum)
            at = kk_ref[i, rows, :] * jnp.exp(cum - logw)
            bt = b * e_neg
            kt = k * e_neg
            d_end = jnp.exp(cum_end - cum)
            bh = b * d_end
            kh = k * d_end
            p_end = jnp.exp(cum_end)
            v = v_ref[i, rows, :]
            for hd in range(N_HEADS):
                lp = slice((hd // 2) * pair, (hd // 2 + 1) * pair)
                m = own[hd % 2]
                xl.append(jnp.concatenate([jnp.where(m, at[:, lp], 0.0),
                                           jnp.where(m, rt[:, lp], 0.0)], axis=0))
                xr.append(jnp.concatenate([bt[:, lp], kt[:, lp]], axis=0))
                xe.append(jnp.concatenate([jnp.where(m, bh[:, lp], 0.0),
                                           jnp.where(m, kh[:, lp], 0.0)], axis=0).astype(BF16))
                vh.append(v[:, hd * HEAD_SIZE:(hd + 1) * HEAD_SIZE])
                pe.append(p_end[:, lp])
    n = len(xl)

    gram = [_dot1(xl[j], xr[j], _NT) for j in range(n)]
    tinv = _unit_lower_inverse([jnp.where(strict, g[:c, :c], 0.0) for g in gram], c)
    yk = [_dot1(jnp.where(strict, gram[j][:c, c:], 0.0), vh[j]) for j in range(n)]
    wu = [_dot1(tinv[j], jnp.concatenate([xl[j][:c], -yk[j]], axis=1)) for j in range(n)]
    w_b = [wu[j][:, :pair].astype(BF16) for j in range(n)]
    r_b = [xl[j][c:].astype(BF16) for j in range(n)]
    u0t = [wu[j][:, pair:].T for j in range(n)]
    vt = [x.T.astype(BF16) for x in vh]
    m_r = [jnp.where(incl2, g[c:, :], 0.0).astype(BF16) for g in gram]

    nh = nb * N_HEADS
    st = [s_scr[jj // N_HEADS, jj % N_HEADS] for jj in range(nh)]
    gain = gain_ref[...]
    bias = bias_ref[...]
    for q in range(nc):
        rows = slice(q * c, (q + 1) * c)
        js = [q * nh + jj for jj in range(nh)]
        sb = [s.astype(BF16) for s in st]
        ut = [u0t[j] - _dg(sb[jj], w_b[j], _NT) for jj, j in enumerate(js)]
        rst = [_dg(sb[jj], r_b[j], _NT) for jj, j in enumerate(js)]
        uvt = [jnp.concatenate([ut[jj].astype(BF16), vt[j]], axis=1)
               for jj, j in enumerate(js)]
        st = [st[jj] * pe[j] + _dg(uvt[jj], xe[j]) for jj, j in enumerate(js)]
        ot = [rst[jj] + _dg(uvt[jj], m_r[j], _NT) for jj, j in enumerate(js)]
        mu = [jnp.mean(x, axis=0, keepdims=True) for x in ot]
        dev = [x - m for x, m in zip(ot, mu)]
        var = [jnp.mean(jnp.square(d), axis=0, keepdims=True) for d in dev]
        on = [(d * lax.rsqrt(s + GN_EPS)).T for d, s in zip(dev, var)]
        for jj in range(0, nh, 2):
            i, hd = jj // N_HEADS, jj % N_HEADS
            lp = slice(hd * HEAD_SIZE, (hd + 2) * HEAD_SIZE)
            on_pair = jnp.concatenate([on[jj], on[jj + 1]], axis=1)
            y_ref[i, rows, lp] = ((on_pair * gain[:, lp] + bias[:, lp]) * g_ref[i, rows, lp]
                                  + bg_ref[i, rows, lp])

    for jj in range(nh):
        s_scr[jj // N_HEADS, jj % N_HEADS] = st[jj]

    @pl.when(pl.program_id(1) == pl.num_programs(1) - 1)
    def _():
        for jj in range(nh):
            hd = jj % N_HEADS
            sout_ref[jj // N_HEADS, hd] = st[jj][:, (hd % 2) * HEAD_SIZE:(hd % 2 + 1) * HEAD_SIZE]


def _wkv_call(seqs, s0, gain, bias, *, nb, c, nc):
    B, T, _ = seqs[0].shape
    tok = pl.BlockSpec((nb, nc * c, RWKV_DIM), lambda b, t: (b, t, 0))
    st = pl.BlockSpec((nb, N_HEADS, HEAD_SIZE, HEAD_SIZE), lambda b, t: (b, 0, 0, 0))
    vec = pl.BlockSpec((1, RWKV_DIM), lambda b, t: (0, 0))
    return pl.pallas_call(
        functools.partial(_wkv_kernel, nb=nb, c=c, nc=nc),
        grid=(B // nb, T // (nc * c)),
        in_specs=[tok] * 8 + [st, vec, vec],
        out_specs=[tok, st],
        out_shape=[jax.ShapeDtypeStruct((B, T, RWKV_DIM), F32),
                   jax.ShapeDtypeStruct((B, N_HEADS, HEAD_SIZE, HEAD_SIZE), F32)],
        scratch_shapes=[pltpu.VMEM((nb, N_HEADS, HEAD_SIZE, 2 * HEAD_SIZE), F32)],
        compiler_params=pltpu.CompilerParams(
            dimension_semantics=("arbitrary", "arbitrary"), vmem_limit_bytes=VMEM_LIMIT),
        name="wkv",
    )(*seqs, s0, gain, bias)


def _out_stream(rows, x_ref, yc_ref, yr_ref, wout_ref, gpost_ref, gfpre_ref, gfpost_ref,
                wff1_ref, wff2_ref, o_ref):
    wout = wout_ref[...]
    mix = (_dg(yc_ref[rows, :].astype(BF16), wout[:CONV_DIM])
           + _dg(yr_ref[rows, :].astype(BF16), wout[CONV_DIM:]))
    yield
    x1 = x_ref[rows, :] + _rmsnorm(mix, gpost_ref[...])
    h2 = _rmsnorm(x1, gfpre_ref[...]).astype(BF16)
    yield
    f1 = _dg(h2, wff1_ref[...])
    yield
    f1 = jnp.square(jnp.maximum(f1, 0.0)).astype(BF16)
    yield
    f2 = _dg(f1, wff2_ref[...])
    yield
    o_ref[rows, :] = x1 + _rmsnorm(f2, gfpost_ref[...])


def _out_kernel(*refs, tm, ns):
    per = tm // ns
    _interleave([_out_stream(slice(i * per, (i + 1) * per), *refs) for i in range(ns)])


def _out_call(x, yc, yr, params, *, tm, ns):
    n = x.shape[0]
    tok = lambda w: pl.BlockSpec((tm, w), lambda i: (i, 0))
    full = lambda arr: pl.BlockSpec(arr.shape, lambda i: (0,) * arr.ndim,
                                    pipeline_mode=pl.Buffered(1))
    return pl.pallas_call(
        functools.partial(_out_kernel, tm=tm, ns=ns),
        grid=(n // tm,),
        in_specs=[tok(D_MODEL), tok(CONV_DIM), tok(RWKV_DIM)] + [full(p) for p in params],
        out_specs=tok(D_MODEL),
        out_shape=jax.ShapeDtypeStruct((n, D_MODEL), F32),
        compiler_params=pltpu.CompilerParams(
            dimension_semantics=("arbitrary",), vmem_limit_bytes=VMEM_LIMIT),
        name="outffn",
    )(x, yc, yr, *params)


def _tiles(B, T):
    if T >= 256:
        return 2, 256, 2, 2, 64, 4, 512, 2
    return B, T, 1, 4, T, 1, B * T, 1


def _layer(x, conv_buf, shift_prev, wkv_state, proj_params, gn, norms, out_w):
    B, T, _ = x.shape
    pnb, ptt, pns, wnb, wc, wnc, tm, ons = _tiles(B, T)
    cast = [w for w in out_w if w.dtype != BF16]
    outs = _proj_call(x, conv_buf, shift_prev, proj_params, cast,
                      nb=pnb, tt=ptt, ns=pns)
    yconv, seqs, new_conv, new_shift = outs[0], outs[1:9], outs[9], outs[10]
    if cast:
        out_w = tuple(outs[11:])
    yr, new_wkv = _wkv_call(seqs, wkv_state, *gn, nb=wnb, c=wc, nc=wnc)
    out_params = (out_w[0], *norms, out_w[1], out_w[2])
    y = _out_call(x.reshape(B * T, D_MODEL), yconv.reshape(B * T, CONV_DIM),
                  yr.reshape(B * T, RWKV_DIM), out_params, tm=tm, ns=ons)
    return (y.reshape(B, T, D_MODEL), new_conv, new_shift, new_wkv,
            out_w)


def kernel(x_prompt, x_sample, state_conv, state_shift, state_wkv, norm_mix_pre, norm_mix_post,
           norm_ffn_pre, norm_ffn_post, w_in, conv_w, shift_mu, w_decay2, decay_w0, w_a2, a0,
           w_g2, k_k, k_a, r_k, gn_gain, gn_bias, w_out, w_ff1, w_ff2):
    depth = w_in.shape[0]
    Bp = x_prompt.shape[0]
    xp, xs = x_prompt, x_sample
    row = lambda t: t.reshape(1, -1).astype(F32)
    res = [[] for _ in range(6)]
    for l in range(depth):
        proj_params = (
            row(norm_mix_pre[l]), w_in[l].astype(BF16), conv_w[l], row(shift_mu[l]),
            w_decay2[l], row(decay_w0[l]), w_a2[l], row(a0[l]), w_g2[l],
            row(k_k[l]), row(k_a[l]), row(r_k[l]))
        gn = (row(gn_gain[l]), row(gn_bias[l]))
        norms = (row(norm_mix_post[l]), row(norm_ffn_pre[l]), row(norm_ffn_post[l]))
        xp, c_p, s_p, w_p, out_w = _layer(
            xp, jnp.zeros((Bp, CONV_K - 1, CONV_DIM), F32), jnp.zeros((Bp, RWKV_PROJ), F32),
            jnp.zeros((Bp, N_HEADS, HEAD_SIZE, HEAD_SIZE), F32), proj_params, gn, norms,
            (w_out[l], w_ff1[l], w_ff2[l]))
        xs, c_s, s_s, w_s, _ = _layer(xs, state_conv[l], state_shift[l], state_wkv[l],
                                      proj_params, gn, norms, out_w)
        for lst, val in zip(res, (c_p, s_p, w_p, c_s, s_s, w_s)):
            lst.append(val)
    return (xp, xs) + tuple(jnp.stack(r) for r in res)
```

```python
import functools

import jax
import jax.numpy as jnp
import numpy as np
from jax import lax
from jax.experimental import pallas as pl
from jax.experimental.pallas import tpu as pltpu

D_MODEL = 1024
CONV_DIM = 512
RWKV_DIM = 512
HEAD_SIZE = 64
N_HEADS = 8
CONV_K = 3
DECAY_LORA = 64
AAA_LORA = 64
GATE_LORA = 128
RWKV_PROJ = 3 * RWKV_DIM + DECAY_LORA + AAA_LORA + GATE_LORA
PROJ_DIM = 3 * CONV_DIM + RWKV_PROJ
D_FF = 4 * D_MODEL
NORM_EPS = 1e-6
GN_EPS = 64e-5
DECAY_SCALE = float(np.exp(-0.5))

SUBLANES = 8
INV_BASE = 16
VMEM_LIMIT = 56 * 1024 * 1024

F32 = jnp.float32
BF16 = jnp.bfloat16

_NN = (((1,), (0,)), ((), ()))
_NT = (((1,), (1,)), ((), ()))


def _dg(a, b, dims=_NN):
    return lax.dot_general(a, b, dims, preferred_element_type=F32)


def _dot1(a, b, dims=_NN):
    return _dg(a.astype(BF16), b.astype(BF16), dims)


def _bf16_terms(a, n):
    terms = []
    for _ in range(n - 1):
        t = a.astype(BF16)
        terms.append(t)
        a = a - t.astype(F32)
    terms.append(a.astype(BF16))
    return terms


def _sigmoid(x):
    return 0.5 * jnp.tanh(0.5 * x) + 0.5


def _interleave(streams):
    live = list(streams)
    while live:
        live = [s for s in live if next(s, True) is None]


def _head_sums(x):
    pair = 2 * HEAD_SIZE
    low = lax.broadcasted_iota(jnp.int32, (x.shape[0], pair), 1) < HEAD_SIZE
    tiles = []
    for p in range(RWKV_DIM // pair):
        t = x[:, p * pair:(p + 1) * pair]
        s_lo = jnp.sum(jnp.where(low, t, 0.0), axis=-1, keepdims=True)
        s_hi = jnp.sum(jnp.where(low, 0.0, t), axis=-1, keepdims=True)
        tiles.append(jnp.where(low, s_lo, s_hi))
    return jnp.concatenate(tiles, axis=1)


def _rmsnorm(x, g):
    return x * lax.rsqrt(jnp.mean(x * x, axis=-1, keepdims=True) + NORM_EPS) * g


def _proj_stream(sq, rq, tt, x_ref, gpre_ref, win_ref, convw_ref, mu_ref, wd_ref, w0_ref, wa_ref,
                 a0_ref, wg_ref, kk_ref, ka_ref, rk_ref, yconv_ref, r_ref, k_ref,
                 v_ref, kkn_ref, b_ref, lw_ref, g_ref, bg_ref, nconv_ref, nshift_ref, ubuf, zbuf):
    nbs = sq.stop - sq.start
    rn = rq.stop - rq.start
    m = nbs * rn
    lo = SUBLANES + rq.start
    hi = lo + rn

    x = x_ref[sq, rq].reshape(m, D_MODEL)
    h = _rmsnorm(x, gpre_ref[...]).astype(BF16)
    zr = _dg(h, win_ref[:, 3 * CONV_DIM:])
    yield

    zbuf[sq, lo:hi, :] = zr.reshape(nbs, rn, RWKV_PROJ)
    zprev = zbuf[sq, lo - 1:hi - 1, :].reshape(m, RWKV_PROJ)
    zs = zr + mu_ref[...] * (zprev - zr)
    if rq.stop == tt:
        for i in range(sq.start, sq.stop):
            last_z = zbuf[i, hi - 1:hi, :]
            nshift_ref[i:i + 1, :] = last_z
            zbuf[i, SUBLANES - 1:SUBLANES, :] = last_z

    r = zs[:, :RWKV_DIM]
    k = zs[:, RWKV_DIM:2 * RWKV_DIM]
    v = zs[:, 2 * RWKV_DIM:3 * RWKV_DIM]
    lwa = zs[:, 3 * RWKV_DIM:3 * RWKV_DIM + DECAY_LORA + AAA_LORA]
    lg = zs[:, 3 * RWKV_DIM + DECAY_LORA + AAA_LORA:]
    logit_w = w0_ref[...] + _dot1(jnp.tanh(lwa[:, :DECAY_LORA]), wd_ref[...])
    a_pre = a0_ref[...] + _dot1(lwa[:, DECAY_LORA:], wa_ref[...])
    g = _dot1(_sigmoid(lg), wg_ref[...])
    zcv = _dg(h, win_ref[:, :3 * CONV_DIM])
    yield

    logw = -DECAY_SCALE * _sigmoid(logit_w)
    a = _sigmoid(a_pre)
    zb = zcv[:, :CONV_DIM]
    ubuf[sq, lo:hi, :] = (zcv[:, CONV_DIM:2 * CONV_DIM] * zcv[:, 2 * CONV_DIM:]).reshape(
        nbs, rn, CONV_DIM)
    cw = convw_ref[...]
    yc = (ubuf[sq, lo - 2:hi - 2, :] * cw[0:1, :]
          + ubuf[sq, lo - 1:hi - 1, :] * cw[1:2, :]
          + ubuf[sq, lo:hi, :] * cw[2:3, :])
    yconv_ref[sq, rq] = (zb.reshape(nbs, rn, CONV_DIM) * yc)
    if rq.stop == tt:
        last_u = ubuf[sq, hi - 2:hi, :]
        nconv_ref[sq] = last_u
        ubuf[sq, SUBLANES - 2:SUBLANES, :] = last_u
    yield

    kk = k * kk_ref[...]
    ss = _head_sums(kk * kk)
    kp = k * (1.0 + (a - 1.0) * ka_ref[...])
    bsum = _head_sums(r * kp * rk_ref[...])
    yield

    kk = kk * lax.rsqrt(jnp.maximum(ss, 1e-24))
    shp = (nbs, rn, RWKV_DIM)
    r_ref[sq, rq] = r.reshape(shp)
    k_ref[sq, rq] = kp.reshape(shp)
    v_ref[sq, rq] = v.reshape(shp)
    kkn_ref[sq, rq] = kk.reshape(shp)
    b_ref[sq, rq] = (kk * a).reshape(shp)
    lw_ref[sq, rq] = logw.reshape(shp)
    g_ref[sq, rq] = g.reshape(shp)
    bg_ref[sq, rq] = (bsum * v * g).reshape(shp)


def _proj_kernel(x_ref, cbuf_ref, sprev_ref, *rest, nb, tt, ns, ncast):
    ubuf, zbuf = rest[-2:]
    params, cast_in = rest[:12], rest[12:12 + ncast]
    outs, cast_out = rest[12 + ncast:23 + ncast], rest[23 + ncast:23 + 2 * ncast]

    @pl.when(pl.program_id(1) == 0)
    def _():
        ubuf[:, SUBLANES - 2:SUBLANES, :] = cbuf_ref[...]
        for i in range(nb):
            zbuf[i, SUBLANES - 1:SUBLANES, :] = sprev_ref[i:i + 1, :]

    for src, dst in zip(cast_in, cast_out):
        dst[...] = src[...].astype(BF16)

    nseq = min(ns, nb)
    per, rn = nb // nseq, tt // (ns // nseq)
    _interleave([_proj_stream(slice(i * per, (i + 1) * per), slice(r0, r0 + rn), tt, x_ref,
                              *params, *outs, ubuf, zbuf)
                 for i in range(nseq) for r0 in range(0, tt, rn)])


def _proj_call(x, conv_buf, shift_prev, params, cast_srcs, *, nb, tt, ns):
    B, T, _ = x.shape
    nt = T // tt
    grid = (B // nb, nt)
    steps = grid[0] * nt
    tok = lambda w: pl.BlockSpec((nb, tt, w), lambda b, t: (b, t, 0))
    full = lambda arr: pl.BlockSpec(arr.shape, lambda b, t: (0,) * arr.ndim)
    rows = lambda arr: pl.BlockSpec((arr.shape[0] // steps, arr.shape[1]),
                                    lambda b, t: (b * nt + t, 0))
    in_specs = [tok(D_MODEL),
                pl.BlockSpec((nb, CONV_K - 1, CONV_DIM), lambda b, t: (b, 0, 0)),
                pl.BlockSpec((nb, RWKV_PROJ), lambda b, t: (b, 0))]
    in_specs += [full(p) for p in params] + [rows(w) for w in cast_srcs]
    seq = lambda w: jax.ShapeDtypeStruct((B, T, w), F32)
    out_shape = [seq(CONV_DIM)] + [seq(RWKV_DIM)] * 8 + [
        jax.ShapeDtypeStruct((B, CONV_K - 1, CONV_DIM), F32),
        jax.ShapeDtypeStruct((B, RWKV_PROJ), F32)]
    out_shape += [jax.ShapeDtypeStruct(w.shape, BF16) for w in cast_srcs]
    out_specs = [tok(CONV_DIM)] + [tok(RWKV_DIM)] * 8 + [
        pl.BlockSpec((nb, CONV_K - 1, CONV_DIM), lambda b, t: (b, 0, 0)),
        pl.BlockSpec((nb, RWKV_PROJ), lambda b, t: (b, 0))]
    out_specs += [rows(w) for w in cast_srcs]
    return pl.pallas_call(
        functools.partial(_proj_kernel, nb=nb, tt=tt, ns=ns, ncast=len(cast_srcs)),
        grid=grid, in_specs=in_specs, out_specs=out_specs, out_shape=out_shape,
        scratch_shapes=[pltpu.VMEM((nb, tt + SUBLANES, CONV_DIM), F32),
                        pltpu.VMEM((nb, tt + SUBLANES, RWKV_PROJ), F32)],
        compiler_params=pltpu.CompilerParams(
            dimension_semantics=("arbitrary", "arbitrary"), vmem_limit_bytes=VMEM_LIMIT),
        name="proj",
    )(x, conv_buf, shift_prev, *params, *cast_srcs)


def _unit_lower_inverse(lows, c):
    base = min(INV_BASE, c)
    nblk = c // base
    prow = lax.broadcasted_iota(jnp.int32, (base, c), 0)
    pcol = lax.broadcasted_iota(jnp.int32, (base, c), 1)
    lane_blk = pcol // base
    eye_p = (pcol - lane_blk * base == prow).astype(F32)

    def pack(m):
        return sum(jnp.where(lane_blk == i, m[i * base:(i + 1) * base, :], 0.0)
                   for i in range(nblk))

    def expand(p):
        return jnp.concatenate([jnp.where(lane_blk == i, p, 0.0) for i in range(nblk)], axis=0)

    ps = [pack(low) for low in lows]
    invs = [eye_p - p for p in ps]
    ps = [_dot1(p, expand(p)) for p in ps]
    span = 4
    while span < base:
        both = [_dot1(jnp.concatenate([inv, p], axis=0), expand(p)) for inv, p in zip(invs, ps)]
        invs = [inv + b[:base] for inv, b in zip(invs, both)]
        ps = [b[base:] for b in both]
        span *= 2
    invs = [inv + _dot1(inv, expand(p)) for inv, p in zip(invs, ps)]
    invs = [expand(inv) for inv in invs]

    row = lax.broadcasted_iota(jnp.int32, (c, c), 0)
    col = lax.broadcasted_iota(jnp.int32, (c, c), 1)
    blk = base
    while blk < c:
        sel = ((row // (2 * blk)) == (col // (2 * blk))) & ((row // blk) != (col // blk))
        odd = [slice((2 * b + 1) * blk, (2 * b + 2) * blk) for b in range(c // (2 * blk))]
        zero = jnp.zeros((blk, c), F32)

        def take(m):
            return jnp.concatenate([m[s] for s in odd], axis=0)

        def place(mr):
            parts = []
            for b in range(len(odd)):
                parts += [zero, mr[b * blk:(b + 1) * blk]]
            return jnp.concatenate(parts, axis=0)

        tmp = [place(_dot1(take(jnp.where(sel, low, 0.0)), inv)) for low, inv in zip(lows, invs)]
        invs = [inv - place(_dot1(take(inv), t)) for inv, t in zip(invs, tmp)]
        blk *= 2
    return invs


def _wkv_kernel(r_ref, k_ref, v_ref, kk_ref, b_ref, lw_ref, g_ref, bg_ref, s0_ref,
                gain_ref, bias_ref, y_ref, sout_ref, s_scr, *, nb, c, nc):
    pair = 2 * HEAD_SIZE
    zpad = jnp.zeros((HEAD_SIZE, HEAD_SIZE), F32)

    @pl.when(pl.program_id(1) == 0)
    def _():
        for i in range(nb):
            for hd in range(N_HEADS):
                s0 = s0_ref[i, hd]
                s_scr[i, hd] = jnp.concatenate([s0, zpad] if hd % 2 == 0 else [zpad, s0], axis=1)

    row = lax.broadcasted_iota(jnp.int32, (c, c), 0)
    col = lax.broadcasted_iota(jnp.int32, (c, c), 1)
    strict = row > col
    incl = row >= col
    row2 = lax.broadcasted_iota(jnp.int32, (c, 2 * c), 0)
    col2 = lax.broadcasted_iota(jnp.int32, (c, 2 * c), 1)
    incl2 = row2 >= jnp.where(col2 >= c, col2 - c, col2)
    tri = incl.astype(BF16)
    lane_half = lax.broadcasted_iota(jnp.int32, (c, pair), 1) // HEAD_SIZE
    own = [lane_half == 0, lane_half == 1]
    zero_b = jnp.zeros((c, pair), BF16)

    def precompute(chunks):
        xl, xr, xe, vh, pe = [], [], [], [], []
        for q in chunks:
            rows = slice(q * c, (q + 1) * c)
            for i in range(nb):
                logw = lw_ref[i, rows, :]
                cum = sum(_dg(tri, t) for t in _bf16_terms(logw, 2))
                cum_end = cum[c - 1:c, :]
                k = k_ref[i, rows, :]
                b = b_ref[i, rows, :]
                e_neg = jnp.exp(-cum)
                rt = (r_ref[i, rows, :] * jnp.exp(cum)).astype(BF16)
                at = (kk_ref[i, rows, :] * jnp.exp(cum - logw)).astype(BF16)
                bt = (b * e_neg).astype(BF16)
                kt = (k * e_neg).astype(BF16)
                d_end = jnp.exp(cum_end - cum)
                bh = (b * d_end).astype(BF16)
                kh = (k * d_end).astype(BF16)
                p_end = jnp.exp(cum_end)
                v = v_ref[i, rows, :]
                for hd in range(N_HEADS):
                    lp = slice((hd // 2) * pair, (hd // 2 + 1) * pair)
                    m = own[hd % 2]
                    if hd % 2 == 0:
                        xr_pair = jnp.concatenate([bt[:, lp], kt[:, lp]], axis=0)
                    xr.append(xr_pair)
                    xl.append(jnp.concatenate([jnp.where(m, at[:, lp], zero_b),
                                               jnp.where(m, rt[:, lp], zero_b)], axis=0))
                    xe.append(jnp.concatenate([jnp.where(m, bh[:, lp], zero_b),
                                               jnp.where(m, kh[:, lp], zero_b)], axis=0))
                    vh.append(v[:, hd * HEAD_SIZE:(hd + 1) * HEAD_SIZE])
                    pe.append(p_end[:, lp])
        n = len(xl)

        gram = [_dg(xl[j], xr[j], _NT) for j in range(n)]
        tinv = _unit_lower_inverse([jnp.where(strict, g[:c, :c], 0.0) for g in gram], c)
        yk = [_dot1(jnp.where(strict, gram[j][:c, c:], 0.0), vh[j]) for j in range(n)]
        wu = [_dg(tinv[j].astype(BF16),
                  jnp.concatenate([xl[j][:c], (-yk[j]).astype(BF16)], axis=1)) for j in range(n)]
        w_b = [wu[j][:, :pair].astype(BF16) for j in range(n)]
        r_b = [xl[j][c:] for j in range(n)]
        u0t = [wu[j][:, pair:].T for j in range(n)]
        vt = [x.T.astype(BF16) for x in vh]
        m_r = [jnp.where(incl2, g[c:, :], 0.0).astype(BF16) for g in gram]
        return w_b, r_b, u0t, vt, m_r, xe, pe

    w_b, r_b, u0t, vt, m_r, xe, pe = precompute(range(nc))

    nh = nb * N_HEADS
    st = [s_scr[jj // N_HEADS, jj % N_HEADS] for jj in range(nh)]
    gain = gain_ref[...]
    bias = bias_ref[...]
    for q in range(nc):
        rows = slice(q * c, (q + 1) * c)
        js = [q * nh + jj for jj in range(nh)]
        sb = [s.astype(BF16) for s in st]
        ut = [u0t[j] - _dg(sb[jj], w_b[j], _NT) for jj, j in enumerate(js)]
        rst = [_dg(sb[jj], r_b[j], _NT) for jj, j in enumerate(js)]
        uvt = [jnp.concatenate([ut[jj].astype(BF16), vt[j]], axis=1)
               for jj, j in enumerate(js)]
        st = [st[jj] * pe[j] + _dg(uvt[jj], xe[j]) for jj, j in enumerate(js)]
        ot = [rst[jj] + _dg(uvt[jj], m_r[j], _NT) for jj, j in enumerate(js)]
        mu = [jnp.mean(x, axis=0, keepdims=True) for x in ot]
        dev = [x - m for x, m in zip(ot, mu)]
        var = [jnp.mean(jnp.square(d), axis=0, keepdims=True) for d in dev]
        on = [(d * lax.rsqrt(s + GN_EPS)).T for d, s in zip(dev, var)]
        for jj in range(0, nh, 2):
            i, hd = jj // N_HEADS, jj % N_HEADS
            lp = slice(hd * HEAD_SIZE, (hd + 2) * HEAD_SIZE)
            on_pair = jnp.concatenate([on[jj], on[jj + 1]], axis=1)
            y_ref[i, rows, lp] = ((on_pair * gain[:, lp] + bias[:, lp]) * g_ref[i, rows, lp]
                                  + bg_ref[i, rows, lp])

    for jj in range(nh):
        s_scr[jj // N_HEADS, jj % N_HEADS] = st[jj]

    @pl.when(pl.program_id(1) == pl.num_programs(1) - 1)
    def _():
        for jj in range(nh):
            hd = jj % N_HEADS
            sout_ref[jj // N_HEADS, hd] = st[jj][:, (hd % 2) * HEAD_SIZE:(hd % 2 + 1) * HEAD_SIZE]


def _wkv_call(seqs, s0, gain, bias, *, nb, c, nc):
    B, T, _ = seqs[0].shape
    tok = pl.BlockSpec((nb, nc * c, RWKV_DIM), lambda b, t: (b, t, 0))
    st = pl.BlockSpec((nb, N_HEADS, HEAD_SIZE, HEAD_SIZE), lambda b, t: (b, 0, 0, 0))
    vec = pl.BlockSpec((1, RWKV_DIM), lambda b, t: (0, 0))
    return pl.pallas_call(
        functools.partial(_wkv_kernel, nb=nb, c=c, nc=nc),
        grid=(B // nb, T // (nc * c)),
        in_specs=[tok] * 8 + [st, vec, vec],
        out_specs=[tok, st],
        out_shape=[jax.ShapeDtypeStruct((B, T, RWKV_DIM), F32),
                   jax.ShapeDtypeStruct((B, N_HEADS, HEAD_SIZE, HEAD_SIZE), F32)],
        scratch_shapes=[pltpu.VMEM((nb, N_HEADS, HEAD_SIZE, 2 * HEAD_SIZE), F32)],
        compiler_params=pltpu.CompilerParams(
            dimension_semantics=("arbitrary", "arbitrary"), vmem_limit_bytes=VMEM_LIMIT),
        name="wkv",
    )(*seqs, s0, gain, bias)


def _out_stream(rows, x_ref, yc_ref, yr_ref, wout_ref, gpost_ref, gfpre_ref, gfpost_ref,
                wff1_ref, wff2_ref, o_ref):
    wout = wout_ref[...]
    mix = (_dg(yc_ref[rows, :].astype(BF16), wout[:CONV_DIM])
           + _dg(yr_ref[rows, :].astype(BF16), wout[CONV_DIM:]))
    yield
    x1 = x_ref[rows, :] + _rmsnorm(mix, gpost_ref[...])
    h2 = _rmsnorm(x1, gfpre_ref[...]).astype(BF16)
    yield
    f1 = _dg(h2, wff1_ref[...])
    yield
    f1 = jnp.square(jnp.maximum(f1, 0.0)).astype(BF16)
    yield
    f2 = _dg(f1, wff2_ref[...])
    yield
    o_ref[rows, :] = x1 + _rmsnorm(f2, gfpost_ref[...])


def _out_kernel(*refs, tm, ns):
    per = tm // ns
    _interleave([_out_stream(slice(i * per, (i + 1) * per), *refs) for i in range(ns)])


def _out_call(x, yc, yr, params, *, tm, ns):
    n = x.shape[0]
    tok = lambda w: pl.BlockSpec((tm, w), lambda i: (i, 0))
    full = lambda arr: pl.BlockSpec(arr.shape, lambda i: (0,) * arr.ndim,
                                    pipeline_mode=pl.Buffered(1))
    return pl.pallas_call(
        functools.partial(_out_kernel, tm=tm, ns=ns),
        grid=(n // tm,),
        in_specs=[tok(D_MODEL), tok(CONV_DIM), tok(RWKV_DIM)] + [full(p) for p in params],
        out_specs=tok(D_MODEL),
        out_shape=jax.ShapeDtypeStruct((n, D_MODEL), F32),
        compiler_params=pltpu.CompilerParams(
            dimension_semantics=("arbitrary",), vmem_limit_bytes=VMEM_LIMIT),
        name="outffn",
    )(x, yc, yr, *params)


def _tiles(B, T):
    if T >= 256:
        return 2, 256, 4, 2, 64, 4, 512, 2
    return B, T, 1, 4, T, 1, B * T, 1


def _layer(x, conv_buf, shift_prev, wkv_state, proj_params, gn, norms, out_w):
    B, T, _ = x.shape
    pnb, ptt, pns, wnb, wc, wnc, tm, ons = _tiles(B, T)
    cast = [w for w in out_w if w.dtype != BF16]
    outs = _proj_call(x, conv_buf, shift_prev, proj_params, cast,
                      nb=pnb, tt=ptt, ns=pns)
    yconv, seqs, new_conv, new_shift = outs[0], outs[1:9], outs[9], outs[10]
    if cast:
        out_w = tuple(outs[11:])
    yr, new_wkv = _wkv_call(seqs, wkv_state, *gn, nb=wnb, c=wc, nc=wnc)
    out_params = (out_w[0], *norms, out_w[1], out_w[2])
    y = _out_call(x.reshape(B * T, D_MODEL), yconv.reshape(B * T, CONV_DIM),
                  yr.reshape(B * T, RWKV_DIM), out_params, tm=tm, ns=ons)
    return (y.reshape(B, T, D_MODEL), new_conv, new_shift, new_wkv,
            out_w)


def kernel(x_prompt, x_sample, state_conv, state_shift, state_wkv, norm_mix_pre, norm_mix_post,
           norm_ffn_pre, norm_ffn_post, w_in, conv_w, shift_mu, w_decay2, decay_w0, w_a2, a0,
           w_g2, k_k, k_a, r_k, gn_gain, gn_bias, w_out, w_ff1, w_ff2):
    depth = w_in.shape[0]
    Bp = x_prompt.shape[0]
    xp, xs = x_prompt, x_sample
    row = lambda t: t.reshape(1, -1).astype(F32)
    res = [[] for _ in range(6)]
    for l in range(depth):
        proj_params = (
            row(norm_mix_pre[l]), w_in[l].astype(BF16), conv_w[l], row(shift_mu[l]),
            w_decay2[l], row(decay_w0[l]), w_a2[l], row(a0[l]), w_g2[l],
            row(k_k[l]), row(k_a[l]), row(r_k[l]))
        gn = (row(gn_gain[l]), row(gn_bias[l]))
        norms = (row(norm_mix_post[l]), row(norm_ffn_pre[l]), row(norm_ffn_post[l]))
        xp, c_p, s_p, w_p, out_w = _layer(
            xp, jnp.zeros((Bp, CONV_K - 1, CONV_DIM), F32), jnp.zeros((Bp, RWKV_PROJ), F32),
            jnp.zeros((Bp, N_HEADS, HEAD_SIZE, HEAD_SIZE), F32), proj_params, gn, norms,
            (w_out[l], w_ff1[l], w_ff2[l]))
        xs, c_s, s_s, w_s, _ = _layer(xs, state_conv[l], state_shift[l], state_wkv[l],
                                      proj_params, gn, norms, out_w)
        for lst, val in zip(res, (c_p, s_p, w_p, c_s, s_s, w_s)):
            lst.append(val)
    return (xp, xs) + tuple(jnp.stack(r) for r in res)
```

```python
import functools

import jax
import jax.numpy as jnp
import numpy as np
from jax import lax
from jax.experimental import pallas as pl
from jax.experimental.pallas import tpu as pltpu

D_MODEL = 1024
CONV_DIM = 512
RWKV_DIM = 512
HEAD_SIZE = 64
N_HEADS = 8
CONV_K = 3
DECAY_LORA = 64
AAA_LORA = 64
GATE_LORA = 128
RWKV_PROJ = 3 * RWKV_DIM + DECAY_LORA + AAA_LORA + GATE_LORA
PROJ_DIM = 3 * CONV_DIM + RWKV_PROJ
D_FF = 4 * D_MODEL
NORM_EPS = 1e-6
GN_EPS = 64e-5
DECAY_SCALE = float(np.exp(-0.5))

SUBLANES = 8
INV_BASE = 16
VMEM_LIMIT = 56 * 1024 * 1024

F32 = jnp.float32
BF16 = jnp.bfloat16

_NN = (((1,), (0,)), ((), ()))
_NT = (((1,), (1,)), ((), ()))


def _dg(a, b, dims=_NN):
    return lax.dot_general(a, b, dims, preferred_element_type=F32)


def _dot1(a, b, dims=_NN):
    return _dg(a.astype(BF16), b.astype(BF16), dims)


def _bf16_terms(a, n):
    terms = []
    for _ in range(n - 1):
        t = a.astype(BF16)
        terms.append(t)
        a = a - t.astype(F32)
    terms.append(a.astype(BF16))
    return terms


def _sigmoid(x):
    return 0.5 * jnp.tanh(0.5 * x) + 0.5


def _interleave(streams):
    live = list(streams)
    while live:
        live = [s for s in live if next(s, True) is None]


def _head_sums(x):
    pair = 2 * HEAD_SIZE
    low = lax.broadcasted_iota(jnp.int32, (x.shape[0], pair), 1) < HEAD_SIZE
    tiles = []
    for p in range(RWKV_DIM // pair):
        t = x[:, p * pair:(p + 1) * pair]
        s_lo = jnp.sum(jnp.where(low, t, 0.0), axis=-1, keepdims=True)
        s_hi = jnp.sum(jnp.where(low, 0.0, t), axis=-1, keepdims=True)
        tiles.append(jnp.where(low, s_lo, s_hi))
    return jnp.concatenate(tiles, axis=1)


def _rmsnorm(x, g):
    return x * lax.rsqrt(jnp.mean(x * x, axis=-1, keepdims=True) + NORM_EPS) * g


def _proj_stream(sq, rq, tt, x_ref, gpre_ref, win_ref, convw_ref, mu_ref, wd_ref, w0_ref, wa_ref,
                 a0_ref, wg_ref, kk_ref, ka_ref, rk_ref, yconv_ref, r_ref, k_ref,
                 v_ref, kkn_ref, b_ref, lw_ref, g_ref, bg_ref, nconv_ref, nshift_ref, ubuf, zbuf):
    nbs = sq.stop - sq.start
    rn = rq.stop - rq.start
    m = nbs * rn
    lo = SUBLANES + rq.start
    hi = lo + rn

    x = x_ref[sq, rq].reshape(m, D_MODEL)
    h = _rmsnorm(x, gpre_ref[...]).astype(BF16)
    z = _dg(h, win_ref[...])
    zb = z[:, :CONV_DIM]
    zc = z[:, CONV_DIM:2 * CONV_DIM]
    zh = z[:, 2 * CONV_DIM:3 * CONV_DIM]
    zr = z[:, 3 * CONV_DIM:]
    yield

    ubuf[sq, lo:hi, :] = (zc * zh).reshape(nbs, rn, CONV_DIM)
    cw = convw_ref[...]
    yc = (ubuf[sq, lo - 2:hi - 2, :] * cw[0:1, :]
          + ubuf[sq, lo - 1:hi - 1, :] * cw[1:2, :]
          + ubuf[sq, lo:hi, :] * cw[2:3, :])
    yconv_ref[sq, rq] = (zb.reshape(nbs, rn, CONV_DIM) * yc)

    zbuf[sq, lo:hi, :] = zr.reshape(nbs, rn, RWKV_PROJ)
    zprev = zbuf[sq, lo - 1:hi - 1, :].reshape(m, RWKV_PROJ)
    zs = zr + mu_ref[...] * (zprev - zr)

    if rq.stop == tt:
        last_u = ubuf[sq, hi - 2:hi, :]
        nconv_ref[sq] = last_u
        ubuf[sq, SUBLANES - 2:SUBLANES, :] = last_u
        for i in range(sq.start, sq.stop):
            last_z = zbuf[i, hi - 1:hi, :]
            nshift_ref[i:i + 1, :] = last_z
            zbuf[i, SUBLANES - 1:SUBLANES, :] = last_z

    r = zs[:, :RWKV_DIM]
    k = zs[:, RWKV_DIM:2 * RWKV_DIM]
    v = zs[:, 2 * RWKV_DIM:3 * RWKV_DIM]
    lwa = zs[:, 3 * RWKV_DIM:3 * RWKV_DIM + DECAY_LORA + AAA_LORA]
    lg = zs[:, 3 * RWKV_DIM + DECAY_LORA + AAA_LORA:]
    yield

    logit_w = w0_ref[...] + _dot1(jnp.tanh(lwa[:, :DECAY_LORA]), wd_ref[...])
    logw = -DECAY_SCALE * _sigmoid(logit_w)
    a = _sigmoid(a0_ref[...] + _dot1(lwa[:, DECAY_LORA:], wa_ref[...]))
    g = _dot1(_sigmoid(lg), wg_ref[...])
    yield

    kk = k * kk_ref[...]
    ss = _head_sums(kk * kk)
    kp = k * (1.0 + (a - 1.0) * ka_ref[...])
    bsum = _head_sums(r * kp * rk_ref[...])
    yield

    kk = kk * lax.rsqrt(jnp.maximum(ss, 1e-24))
    shp = (nbs, rn, RWKV_DIM)
    r_ref[sq, rq] = r.reshape(shp)
    k_ref[sq, rq] = kp.reshape(shp)
    v_ref[sq, rq] = v.reshape(shp)
    kkn_ref[sq, rq] = kk.reshape(shp)
    b_ref[sq, rq] = (kk * a).reshape(shp)
    lw_ref[sq, rq] = logw.reshape(shp)
    g_ref[sq, rq] = g.reshape(shp)
    bg_ref[sq, rq] = (bsum * v * g).reshape(shp)


def _proj_kernel(x_ref, cbuf_ref, sprev_ref, *rest, nb, tt, ns, ncast):
    ubuf, zbuf = rest[-2:]
    params, cast_in = rest[:12], rest[12:12 + ncast]
    outs, cast_out = rest[12 + ncast:23 + ncast], rest[23 + ncast:23 + 2 * ncast]

    @pl.when(pl.program_id(1) == 0)
    def _():
        ubuf[:, SUBLANES - 2:SUBLANES, :] = cbuf_ref[...]
        for i in range(nb):
            zbuf[i, SUBLANES - 1:SUBLANES, :] = sprev_ref[i:i + 1, :]

    for src, dst in zip(cast_in, cast_out):
        dst[...] = src[...].astype(BF16)

    nseq = min(ns, nb)
    per, rn = nb // nseq, tt // (ns // nseq)
    _interleave([_proj_stream(slice(i * per, (i + 1) * per), slice(r0, r0 + rn), tt, x_ref,
                              *params, *outs, ubuf, zbuf)
                 for i in range(nseq) for r0 in range(0, tt, rn)])


def _proj_call(x, conv_buf, shift_prev, params, cast_srcs, *, nb, tt, ns):
    B, T, _ = x.shape
    nt = T // tt
    grid = (B // nb, nt)
    steps = grid[0] * nt
    tok = lambda w: pl.BlockSpec((nb, tt, w), lambda b, t: (b, t, 0))
    full = lambda arr: pl.BlockSpec(arr.shape, lambda b, t: (0,) * arr.ndim)
    rows = lambda arr: pl.BlockSpec((arr.shape[0] // steps, arr.shape[1]),
                                    lambda b, t: (b * nt + t, 0))
    in_specs = [tok(D_MODEL),
                pl.BlockSpec((nb, CONV_K - 1, CONV_DIM), lambda b, t: (b, 0, 0)),
                pl.BlockSpec((nb, RWKV_PROJ), lambda b, t: (b, 0))]
    in_specs += [full(p) for p in params] + [rows(w) for w in cast_srcs]
    seq = lambda w: jax.ShapeDtypeStruct((B, T, w), F32)
    out_shape = [seq(CONV_DIM)] + [seq(RWKV_DIM)] * 8 + [
        jax.ShapeDtypeStruct((B, CONV_K - 1, CONV_DIM), F32),
        jax.ShapeDtypeStruct((B, RWKV_PROJ), F32)]
    out_shape += [jax.ShapeDtypeStruct(w.shape, BF16) for w in cast_srcs]
    out_specs = [tok(CONV_DIM)] + [tok(RWKV_DIM)] * 8 + [
        pl.BlockSpec((nb, CONV_K - 1, CONV_DIM), lambda b, t: (b, 0, 0)),
        pl.BlockSpec((nb, RWKV_PROJ), lambda b, t: (b, 0))]
    out_specs += [rows(w) for w in cast_srcs]
    return pl.pallas_call(
        functools.partial(_proj_kernel, nb=nb, tt=tt, ns=ns, ncast=len(cast_srcs)),
        grid=grid, in_specs=in_specs, out_specs=out_specs, out_shape=out_shape,
        scratch_shapes=[pltpu.VMEM((nb, tt + SUBLANES, CONV_DIM), F32),
                        pltpu.VMEM((nb, tt + SUBLANES, RWKV_PROJ), F32)],
        compiler_params=pltpu.CompilerParams(
            dimension_semantics=("arbitrary", "arbitrary"), vmem_limit_bytes=VMEM_LIMIT),
        name="proj",
    )(x, conv_buf, shift_prev, *params, *cast_srcs)


def _unit_lower_inverse(lows, c):
    base = min(INV_BASE, c)
    nblk = c // base
    prow = lax.broadcasted_iota(jnp.int32, (base, c), 0)
    pcol = lax.broadcasted_iota(jnp.int32, (base, c), 1)
    lane_blk = pcol // base
    eye_p = (pcol - lane_blk * base == prow).astype(F32)

    def pack(m):
        return sum(jnp.where(lane_blk == i, m[i * base:(i + 1) * base, :], 0.0)
                   for i in range(nblk))

    def expand(p):
        return jnp.concatenate([jnp.where(lane_blk == i, p, 0.0) for i in range(nblk)], axis=0)

    ps = [pack(low) for low in lows]
    invs = [eye_p - p for p in ps]
    ps = [_dot1(p, expand(p)) for p in ps]
    span = 4
    while span < base:
        both = [_dot1(jnp.concatenate([inv, p], axis=0), expand(p)) for inv, p in zip(invs, ps)]
        invs = [inv + b[:base] for inv, b in zip(invs, both)]
        ps = [b[base:] for b in both]
        span *= 2
    invs = [inv + _dot1(inv, expand(p)) for inv, p in zip(invs, ps)]
    invs = [expand(inv) for inv in invs]

    row = lax.broadcasted_iota(jnp.int32, (c, c), 0)
    col = lax.broadcasted_iota(jnp.int32, (c, c), 1)
    blk = base
    while blk < c:
        sel = ((row // (2 * blk)) == (col // (2 * blk))) & ((row // blk) != (col // blk))
        odd = [slice((2 * b + 1) * blk, (2 * b + 2) * blk) for b in range(c // (2 * blk))]
        zero = jnp.zeros((blk, c), F32)

        def take(m):
            return jnp.concatenate([m[s] for s in odd], axis=0)

        def place(mr):
            parts = []
            for b in range(len(odd)):
                parts += [zero, mr[b * blk:(b + 1) * blk]]
            return jnp.concatenate(parts, axis=0)

        tmp = [place(_dot1(take(jnp.where(sel, low, 0.0)), inv)) for low, inv in zip(lows, invs)]
        invs = [inv - place(_dot1(take(inv), t)) for inv, t in zip(invs, tmp)]
        blk *= 2
    return invs


def _wkv_kernel(r_ref, k_ref, v_ref, kk_ref, b_ref, lw_ref, g_ref, bg_ref, s0_ref,
                gain_ref, bias_ref, y_ref, sout_ref, s_scr, *, nb, c, nc):
    pair = 2 * HEAD_SIZE
    zpad = jnp.zeros((HEAD_SIZE, HEAD_SIZE), F32)

    @pl.when(pl.program_id(1) == 0)
    def _():
        for i in range(nb):
            for hd in range(N_HEADS):
                s0 = s0_ref[i, hd]
                s_scr[i, hd] = jnp.concatenate([s0, zpad] if hd % 2 == 0 else [zpad, s0], axis=1)

    row = lax.broadcasted_iota(jnp.int32, (c, c), 0)
    col = lax.broadcasted_iota(jnp.int32, (c, c), 1)
    strict = row > col
    incl = row >= col
    row2 = lax.broadcasted_iota(jnp.int32, (c, 2 * c), 0)
    col2 = lax.broadcasted_iota(jnp.int32, (c, 2 * c), 1)
    incl2 = row2 >= jnp.where(col2 >= c, col2 - c, col2)
    tri = incl.astype(BF16)
    lane_half = lax.broadcasted_iota(jnp.int32, (c, pair), 1) // HEAD_SIZE
    own = [lane_half == 0, lane_half == 1]
    zero_b = jnp.zeros((c, pair), BF16)

    def precompute(chunks):
        xl, xr, xe, vh, pe = [], [], [], [], []
        for q in chunks:
            rows = slice(q * c, (q + 1) * c)
            for i in range(nb):
                logw = lw_ref[i, rows, :]
                cum = sum(_dg(tri, t) for t in _bf16_terms(logw, 2))
                cum_end = cum[c - 1:c, :]
                k = k_ref[i, rows, :]
                b = b_ref[i, rows, :]
                e_neg = jnp.exp(-cum)
                rt = (r_ref[i, rows, :] * jnp.exp(cum)).astype(BF16)
                at = (kk_ref[i, rows, :] * jnp.exp(cum - logw)).astype(BF16)
                bt = (b * e_neg).astype(BF16)
                kt = (k * e_neg).astype(BF16)
                d_end = jnp.exp(cum_end - cum)
                bh = (b * d_end).astype(BF16)
                kh = (k * d_end).astype(BF16)
                p_end = jnp.exp(cum_end)
                v = v_ref[i, rows, :]
                for hd in range(N_HEADS):
                    lp = slice((hd // 2) * pair, (hd // 2 + 1) * pair)
                    m = own[hd % 2]
                    if hd % 2 == 0:
                        xr_pair = jnp.concatenate([bt[:, lp], kt[:, lp]], axis=0)
                    xr.append(xr_pair)
                    xl.append(jnp.concatenate([jnp.where(m, at[:, lp], zero_b),
                                               jnp.where(m, rt[:, lp], zero_b)], axis=0))
                    xe.append(jnp.concatenate([jnp.where(m, bh[:, lp], zero_b),
                                               jnp.where(m, kh[:, lp], zero_b)], axis=0))
                    vh.append(v[:, hd * HEAD_SIZE:(hd + 1) * HEAD_SIZE])
                    pe.append(p_end[:, lp])
        n = len(xl)

        gram = [_dg(xl[j], xr[j], _NT) for j in range(n)]
        tinv = _unit_lower_inverse([jnp.where(strict, g[:c, :c], 0.0) for g in gram], c)
        yk = [_dot1(jnp.where(strict, gram[j][:c, c:], 0.0), vh[j]) for j in range(n)]
        wu = [_dg(tinv[j].astype(BF16),
                  jnp.concatenate([xl[j][:c], (-yk[j]).astype(BF16)], axis=1)) for j in range(n)]
        w_b = [wu[j][:, :pair].astype(BF16) for j in range(n)]
        r_b = [xl[j][c:] for j in range(n)]
        u0t = [wu[j][:, pair:].T for j in range(n)]
        vt = [x.T.astype(BF16) for x in vh]
        m_r = [jnp.where(incl2, g[c:, :], 0.0).astype(BF16) for g in gram]
        return w_b, r_b, u0t, vt, m_r, xe, pe

    w_b, r_b, u0t, vt, m_r, xe, pe = precompute(range(nc))

    nh = nb * N_HEADS
    st = [s_scr[jj // N_HEADS, jj % N_HEADS] for jj in range(nh)]
    gain = gain_ref[...]
    bias = bias_ref[...]
    for q in range(nc):
        rows = slice(q * c, (q + 1) * c)
        js = [q * nh + jj for jj in range(nh)]
        sb = [s.astype(BF16) for s in st]
        ut = [u0t[j] - _dg(sb[jj], w_b[j], _NT) for jj, j in enumerate(js)]
        rst = [_dg(sb[jj], r_b[j], _NT) for jj, j in enumerate(js)]
        uvt = [jnp.concatenate([ut[jj].astype(BF16), vt[j]], axis=1)
               for jj, j in enumerate(js)]
        st = [st[jj] * pe[j] + _dg(uvt[jj], xe[j]) for jj, j in enumerate(js)]
        ot = [rst[jj] + _dg(uvt[jj], m_r[j], _NT) for jj, j in enumerate(js)]
        mu = [jnp.mean(x, axis=0, keepdims=True) for x in ot]
        dev = [x - m for x, m in zip(ot, mu)]
        var = [jnp.mean(jnp.square(d), axis=0, keepdims=True) for d in dev]
        on = [(d * lax.rsqrt(s + GN_EPS)).T for d, s in zip(dev, var)]
        for jj in range(0, nh, 2):
            i, hd = jj // N_HEADS, jj % N_HEADS
            lp = slice(hd * HEAD_SIZE, (hd + 2) * HEAD_SIZE)
            on_pair = jnp.concatenate([on[jj], on[jj + 1]], axis=1)
            y_ref[i, rows, lp] = ((on_pair * gain[:, lp] + bias[:, lp]) * g_ref[i, rows, lp]
                                  + bg_ref[i, rows, lp])

    for jj in range(nh):
        s_scr[jj // N_HEADS, jj % N_HEADS] = st[jj]

    @pl.when(pl.program_id(1) == pl.num_programs(1) - 1)
    def _():
        for jj in range(nh):
            hd = jj % N_HEADS
            sout_ref[jj // N_HEADS, hd] = st[jj][:, (hd % 2) * HEAD_SIZE:(hd % 2 + 1) * HEAD_SIZE]


def _wkv_call(seqs, s0, gain, bias, *, nb, c, nc):
    B, T, _ = seqs[0].shape
    tok = pl.BlockSpec((nb, nc * c, RWKV_DIM), lambda b, t: (b, t, 0))
    st = pl.BlockSpec((nb, N_HEADS, HEAD_SIZE, HEAD_SIZE), lambda b, t: (b, 0, 0, 0))
    vec = pl.BlockSpec((1, RWKV_DIM), lambda b, t: (0, 0))
    return pl.pallas_call(
        functools.partial(_wkv_kernel, nb=nb, c=c, nc=nc),
        grid=(B // nb, T // (nc * c)),
        in_specs=[tok] * 8 + [st, vec, vec],
        out_specs=[tok, st],
        out_shape=[jax.ShapeDtypeStruct((B, T, RWKV_DIM), F32),
                   jax.ShapeDtypeStruct((B, N_HEADS, HEAD_SIZE, HEAD_SIZE), F32)],
        scratch_shapes=[pltpu.VMEM((nb, N_HEADS, HEAD_SIZE, 2 * HEAD_SIZE), F32)],
        compiler_params=pltpu.CompilerParams(
            dimension_semantics=("arbitrary", "arbitrary"), vmem_limit_bytes=VMEM_LIMIT),
        name="wkv",
    )(*seqs, s0, gain, bias)


def _out_stream(rows, x_ref, yc_ref, yr_ref, wout_ref, gpost_ref, gfpre_ref, gfpost_ref,
                wff1_ref, wff2_ref, o_ref):
    wout = wout_ref[...]
    mix = (_dg(yc_ref[rows, :].astype(BF16), wout[:CONV_DIM])
           + _dg(yr_ref[rows, :].astype(BF16), wout[CONV_DIM:]))
    yield
    x1 = x_ref[rows, :] + _rmsnorm(mix, gpost_ref[...])
    h2 = _rmsnorm(x1, gfpre_ref[...]).astype(BF16)
    yield
    f1 = _dg(h2, wff1_ref[...])
    yield
    f1 = jnp.square(jnp.maximum(f1, 0.0)).astype(BF16)
    yield
    f2 = _dg(f1, wff2_ref[...])
    yield
    o_ref[rows, :] = x1 + _rmsnorm(f2, gfpost_ref[...])


def _out_kernel(*refs, tm, ns):
    per = tm // ns
    _interleave([_out_stream(slice(i * per, (i + 1) * per), *refs) for i in range(ns)])


def _out_call(x, yc, yr, params, *, tm, ns):
    n = x.shape[0]
    tok = lambda w: pl.BlockSpec((tm, w), lambda i: (i, 0))
    full = lambda arr: pl.BlockSpec(arr.shape, lambda i: (0,) * arr.ndim,
                                    pipeline_mode=pl.Buffered(1))
    return pl.pallas_call(
        functools.partial(_out_kernel, tm=tm, ns=ns),
        grid=(n // tm,),
        in_specs=[tok(D_MODEL), tok(CONV_DIM), tok(RWKV_DIM)] + [full(p) for p in params],
        out_specs=tok(D_MODEL),
        out_shape=jax.ShapeDtypeStruct((n, D_MODEL), F32),
        compiler_params=pltpu.CompilerParams(
            dimension_semantics=("arbitrary",), vmem_limit_bytes=VMEM_LIMIT),
        name="outffn",
    )(x, yc, yr, *params)


def _tiles(B, T):
    if T >= 256:
        return 2, 256, 4, 2, 64, 4, 512, 2
    return B, T, 1, B, T, 1, B * T, 1


def _layer(x, conv_buf, shift_prev, wkv_state, proj_params, gn, norms, out_w):
    B, T, _ = x.shape
    pnb, ptt, pns, wnb, wc, wnc, tm, ons = _tiles(B, T)
    cast = [w for w in out_w if w.dtype != BF16]
    outs = _proj_call(x, conv_buf, shift_prev, proj_params, cast,
                      nb=pnb, tt=ptt, ns=pns)
    yconv, seqs, new_conv, new_shift = outs[0], outs[1:9], outs[9], outs[10]
    if cast:
        out_w = tuple(outs[11:])
    yr, new_wkv = _wkv_call(seqs, wkv_state, *gn, nb=wnb, c=wc, nc=wnc)
    out_params = (out_w[0], *norms, out_w[1], out_w[2])
    y = _out_call(x.reshape(B * T, D_MODEL), yconv.reshape(B * T, CONV_DIM),
                  yr.reshape(B * T, RWKV_DIM), out_params, tm=tm, ns=ons)
    return (y.reshape(B, T, D_MODEL), new_conv, new_shift, new_wkv,
            out_w)


def kernel(x_prompt, x_sample, state_conv, state_shift, state_wkv, norm_mix_pre, norm_mix_post,
           norm_ffn_pre, norm_ffn_post, w_in, conv_w, shift_mu, w_decay2, decay_w0, w_a2, a0,
           w_g2, k_k, k_a, r_k, gn_gain, gn_bias, w_out, w_ff1, w_ff2):
    depth = w_in.shape[0]
    Bp = x_prompt.shape[0]
    xp, xs = x_prompt, x_sample
    row = lambda t: t.reshape(1, -1).astype(F32)
    res = [[] for _ in range(6)]
    for l in range(depth):
        proj_params = (
            row(norm_mix_pre[l]), w_in[l].astype(BF16), conv_w[l], row(shift_mu[l]),
            w_decay2[l], row(decay_w0[l]), w_a2[l], row(a0[l]), w_g2[l],
            row(k_k[l]), row(k_a[l]), row(r_k[l]))
        gn = (row(gn_gain[l]), row(gn_bias[l]))
        norms = (row(norm_mix_post[l]), row(norm_ffn_pre[l]), row(norm_ffn_post[l]))
        xp, c_p, s_p, w_p, out_w = _layer(
            xp, jnp.zeros((Bp, CONV_K - 1, CONV_DIM), F32), jnp.zeros((Bp, RWKV_PROJ), F32),
            jnp.zeros((Bp, N_HEADS, HEAD_SIZE, HEAD_SIZE), F32), proj_params, gn, norms,
            (w_out[l], w_ff1[l], w_ff2[l]))
        xs, c_s, s_s, w_s, _ = _layer(xs, state_conv[l], state_shift[l], state_wkv[l],
                                      proj_params, gn, norms, out_w)
        for lst, val in zip(res, (c_p, s_p, w_p, c_s, s_s, w_s)):
            lst.append(val)
    return (xp, xs) + tuple(jnp.stack(r) for r in res)
```

```python
import functools

import jax
import jax.numpy as jnp
import numpy as np
from jax import lax
from jax.experimental import pallas as pl
from jax.experimental.pallas import tpu as pltpu

D_MODEL = 1024
CONV_DIM = 512
RWKV_DIM = 512
HEAD_SIZE = 64
N_HEADS = 8
CONV_K = 3
DECAY_LORA = 64
AAA_LORA = 64
GATE_LORA = 128
RWKV_PROJ = 3 * RWKV_DIM + DECAY_LORA + AAA_LORA + GATE_LORA
PROJ_DIM = 3 * CONV_DIM + RWKV_PROJ
D_FF = 4 * D_MODEL
NORM_EPS = 1e-6
GN_EPS = 64e-5
DECAY_SCALE = float(np.exp(-0.5))

SUBLANES = 8
INV_BASE = 8
VMEM_LIMIT = 56 * 1024 * 1024

F32 = jnp.float32
BF16 = jnp.bfloat16

_NN = (((1,), (0,)), ((), ()))
_NT = (((1,), (1,)), ((), ()))


def _dg(a, b, dims=_NN):
    return lax.dot_general(a, b, dims, preferred_element_type=F32)


def _dot1(a, b, dims=_NN):
    return _dg(a.astype(BF16), b.astype(BF16), dims)


def _bf16_terms(a, n):
    terms = []
    for _ in range(n - 1):
        t = a.astype(BF16)
        terms.append(t)
        a = a - t.astype(F32)
    terms.append(a.astype(BF16))
    return terms


def _sigmoid(x):
    return 0.5 * jnp.tanh(0.5 * x) + 0.5


def _interleave(streams):
    live = list(streams)
    while live:
        live = [s for s in live if next(s, True) is None]


def _head_sums(x):
    pair = 2 * HEAD_SIZE
    low = lax.broadcasted_iota(jnp.int32, (x.shape[0], pair), 1) < HEAD_SIZE
    tiles = []
    for p in range(RWKV_DIM // pair):
        t = x[:, p * pair:(p + 1) * pair]
        s_lo = jnp.sum(jnp.where(low, t, 0.0), axis=-1, keepdims=True)
        s_hi = jnp.sum(jnp.where(low, 0.0, t), axis=-1, keepdims=True)
        tiles.append(jnp.where(low, s_lo, s_hi))
    return jnp.concatenate(tiles, axis=1)


def _rmsnorm(x, g):
    return x * lax.rsqrt(jnp.mean(x * x, axis=-1, keepdims=True) + NORM_EPS) * g


def _proj_stream(sq, rq, tt, x_ref, gpre_ref, win_ref, convw_ref, mu_ref, wd_ref, w0_ref, wa_ref,
                 a0_ref, wg_ref, kk_ref, ka_ref, rk_ref, yconv_ref, r_ref, k_ref,
                 v_ref, kkn_ref, b_ref, lw_ref, g_ref, bg_ref, nconv_ref, nshift_ref, ubuf, zbuf):
    nbs = sq.stop - sq.start
    rn = rq.stop - rq.start
    m = nbs * rn
    lo = SUBLANES + rq.start
    hi = lo + rn

    x = x_ref[sq, rq].reshape(m, D_MODEL)
    h = _rmsnorm(x, gpre_ref[...]).astype(BF16)
    z = _dg(h, win_ref[...])
    zb = z[:, :CONV_DIM]
    zc = z[:, CONV_DIM:2 * CONV_DIM]
    zh = z[:, 2 * CONV_DIM:3 * CONV_DIM]
    zr = z[:, 3 * CONV_DIM:]
    yield

    ubuf[sq, lo:hi, :] = (zc * zh).reshape(nbs, rn, CONV_DIM)
    cw = convw_ref[...]
    yc = (ubuf[sq, lo - 2:hi - 2, :] * cw[0:1, :]
          + ubuf[sq, lo - 1:hi - 1, :] * cw[1:2, :]
          + ubuf[sq, lo:hi, :] * cw[2:3, :])
    yconv_ref[sq, rq] = (zb.reshape(nbs, rn, CONV_DIM) * yc)

    zbuf[sq, lo:hi, :] = zr.reshape(nbs, rn, RWKV_PROJ)
    zprev = zbuf[sq, lo - 1:hi - 1, :].reshape(m, RWKV_PROJ)
    zs = zr + mu_ref[...] * (zprev - zr)

    if rq.stop == tt:
        last_u = ubuf[sq, hi - 2:hi, :]
        nconv_ref[sq] = last_u
        ubuf[sq, SUBLANES - 2:SUBLANES, :] = last_u
        for i in range(sq.start, sq.stop):
            last_z = zbuf[i, hi - 1:hi, :]
            nshift_ref[i:i + 1, :] = last_z
            zbuf[i, SUBLANES - 1:SUBLANES, :] = last_z

    r = zs[:, :RWKV_DIM]
    k = zs[:, RWKV_DIM:2 * RWKV_DIM]
    v = zs[:, 2 * RWKV_DIM:3 * RWKV_DIM]
    lwa = zs[:, 3 * RWKV_DIM:3 * RWKV_DIM + DECAY_LORA + AAA_LORA]
    lg = zs[:, 3 * RWKV_DIM + DECAY_LORA + AAA_LORA:]
    yield

    logit_w = w0_ref[...] + _dot1(jnp.tanh(lwa[:, :DECAY_LORA]), wd_ref[...])
    logw = -DECAY_SCALE * _sigmoid(logit_w)
    a = _sigmoid(a0_ref[...] + _dot1(lwa[:, DECAY_LORA:], wa_ref[...]))
    g = _dot1(_sigmoid(lg), wg_ref[...])
    yield

    kk = k * kk_ref[...]
    ss = _head_sums(kk * kk)
    kp = k * (1.0 + (a - 1.0) * ka_ref[...])
    bsum = _head_sums(r * kp * rk_ref[...])
    yield

    kk = kk * lax.rsqrt(jnp.maximum(ss, 1e-24))
    shp = (nbs, rn, RWKV_DIM)
    r_ref[sq, rq] = r.reshape(shp)
    k_ref[sq, rq] = kp.reshape(shp)
    v_ref[sq, rq] = v.reshape(shp)
    kkn_ref[sq, rq] = kk.reshape(shp)
    b_ref[sq, rq] = (kk * a).reshape(shp)
    lw_ref[sq, rq] = logw.reshape(shp)
    g_ref[sq, rq] = g.reshape(shp)
    bg_ref[sq, rq] = (bsum * v * g).reshape(shp)


def _proj_kernel(x_ref, cbuf_ref, sprev_ref, *rest, nb, tt, ns, ncast):
    ubuf, zbuf = rest[-2:]
    params, cast_in = rest[:12], rest[12:12 + ncast]
    outs, cast_out = rest[12 + ncast:23 + ncast], rest[23 + ncast:23 + 2 * ncast]

    @pl.when(pl.program_id(1) == 0)
    def _():
        ubuf[:, SUBLANES - 2:SUBLANES, :] = cbuf_ref[...]
        for i in range(nb):
            zbuf[i, SUBLANES - 1:SUBLANES, :] = sprev_ref[i:i + 1, :]

    for src, dst in zip(cast_in, cast_out):
        dst[...] = src[...].astype(BF16)

    nseq = min(ns, nb)
    per, rn = nb // nseq, tt // (ns // nseq)
    _interleave([_proj_stream(slice(i * per, (i + 1) * per), slice(r0, r0 + rn), tt, x_ref,
                              *params, *outs, ubuf, zbuf)
                 for i in range(nseq) for r0 in range(0, tt, rn)])


def _proj_call(x, conv_buf, shift_prev, params, cast_srcs, *, nb, tt, ns):
    B, T, _ = x.shape
    nt = T // tt
    grid = (B // nb, nt)
    steps = grid[0] * nt
    tok = lambda w: pl.BlockSpec((nb, tt, w), lambda b, t: (b, t, 0))
    full = lambda arr: pl.BlockSpec(arr.shape, lambda b, t: (0,) * arr.ndim)
    rows = lambda arr: pl.BlockSpec((arr.shape[0] // steps, arr.shape[1]),
                                    lambda b, t: (b * nt + t, 0))
    in_specs = [tok(D_MODEL),
                pl.BlockSpec((nb, CONV_K - 1, CONV_DIM), lambda b, t: (b, 0, 0)),
                pl.BlockSpec((nb, RWKV_PROJ), lambda b, t: (b, 0))]
    in_specs += [full(p) for p in params] + [rows(w) for w in cast_srcs]
    seq = lambda w: jax.ShapeDtypeStruct((B, T, w), F32)
    out_shape = [seq(CONV_DIM)] + [seq(RWKV_DIM)] * 8 + [
        jax.ShapeDtypeStruct((B, CONV_K - 1, CONV_DIM), F32),
        jax.ShapeDtypeStruct((B, RWKV_PROJ), F32)]
    out_shape += [jax.ShapeDtypeStruct(w.shape, BF16) for w in cast_srcs]
    out_specs = [tok(CONV_DIM)] + [tok(RWKV_DIM)] * 8 + [
        pl.BlockSpec((nb, CONV_K - 1, CONV_DIM), lambda b, t: (b, 0, 0)),
        pl.BlockSpec((nb, RWKV_PROJ), lambda b, t: (b, 0))]
    out_specs += [rows(w) for w in cast_srcs]
    return pl.pallas_call(
        functools.partial(_proj_kernel, nb=nb, tt=tt, ns=ns, ncast=len(cast_srcs)),
        grid=grid, in_specs=in_specs, out_specs=out_specs, out_shape=out_shape,
        scratch_shapes=[pltpu.VMEM((nb, tt + SUBLANES, CONV_DIM), F32),
                        pltpu.VMEM((nb, tt + SUBLANES, RWKV_PROJ), F32)],
        compiler_params=pltpu.CompilerParams(
            dimension_semantics=("arbitrary", "arbitrary"), vmem_limit_bytes=VMEM_LIMIT),
        name="proj",
    )(x, conv_buf, shift_prev, *params, *cast_srcs)


def _unit_lower_inverse(lows, c):
    base = min(INV_BASE, c)
    nblk = c // base
    prow = lax.broadcasted_iota(jnp.int32, (base, c), 0)
    pcol = lax.broadcasted_iota(jnp.int32, (base, c), 1)
    lane_blk = pcol // base
    eye_p = (pcol - lane_blk * base == prow).astype(F32)

    def pack(m):
        return sum(jnp.where(lane_blk == i, m[i * base:(i + 1) * base, :], 0.0)
                   for i in range(nblk))

    def expand(p):
        return jnp.concatenate([jnp.where(lane_blk == i, p, 0.0) for i in range(nblk)], axis=0)

    ps = [pack(low) for low in lows]
    invs = [eye_p - p for p in ps]
    ps = [_dot1(p, expand(p)) for p in ps]
    span = 4
    while span < base:
        both = [_dot1(jnp.concatenate([inv, p], axis=0), expand(p)) for inv, p in zip(invs, ps)]
        invs = [inv + b[:base] for inv, b in zip(invs, both)]
        ps = [b[base:] for b in both]
        span *= 2
    invs = [inv + _dot1(inv, expand(p)) for inv, p in zip(invs, ps)]
    invs = [expand(inv) for inv in invs]

    row = lax.broadcasted_iota(jnp.int32, (c, c), 0)
    col = lax.broadcasted_iota(jnp.int32, (c, c), 1)
    blk = base
    while blk < c:
        sel = ((row // (2 * blk)) == (col // (2 * blk))) & ((row // blk) != (col // blk))
        odd = [slice((2 * b + 1) * blk, (2 * b + 2) * blk) for b in range(c // (2 * blk))]
        zero = jnp.zeros((blk, c), F32)

        def take(m):
            return jnp.concatenate([m[s] for s in odd], axis=0)

        def place(mr):
            parts = []
            for b in range(len(odd)):
                parts += [zero, mr[b * blk:(b + 1) * blk]]
            return jnp.concatenate(parts, axis=0)

        tmp = [place(_dot1(take(jnp.where(sel, low, 0.0)), inv)) for low, inv in zip(lows, invs)]
        invs = [inv - place(_dot1(take(inv), t)) for inv, t in zip(invs, tmp)]
        blk *= 2
    return invs


def _wkv_kernel(r_ref, k_ref, v_ref, kk_ref, b_ref, lw_ref, g_ref, bg_ref, s0_ref,
                gain_ref, bias_ref, y_ref, sout_ref, s_scr, *, nb, c, nc):
    pair = 2 * HEAD_SIZE
    zpad = jnp.zeros((HEAD_SIZE, HEAD_SIZE), F32)

    @pl.when(pl.program_id(1) == 0)
    def _():
        for i in range(nb):
            for hd in range(N_HEADS):
                s0 = s0_ref[i, hd]
                s_scr[i, hd] = jnp.concatenate([s0, zpad] if hd % 2 == 0 else [zpad, s0], axis=1)

    row = lax.broadcasted_iota(jnp.int32, (c, c), 0)
    col = lax.broadcasted_iota(jnp.int32, (c, c), 1)
    strict = row > col
    incl = row >= col
    row2 = lax.broadcasted_iota(jnp.int32, (c, 2 * c), 0)
    col2 = lax.broadcasted_iota(jnp.int32, (c, 2 * c), 1)
    incl2 = row2 >= jnp.where(col2 >= c, col2 - c, col2)
    tri = incl.astype(BF16)
    lane_half = lax.broadcasted_iota(jnp.int32, (c, pair), 1) // HEAD_SIZE
    own = [lane_half == 0, lane_half == 1]
    zero_b = jnp.zeros((c, pair), BF16)

    def precompute(chunks):
        xl, xr, xe, vh, pe = [], [], [], [], []
        for q in chunks:
            rows = slice(q * c, (q + 1) * c)
            for i in range(nb):
                logw = lw_ref[i, rows, :]
                cum = sum(_dg(tri, t) for t in _bf16_terms(logw, 2))
                cum_end = cum[c - 1:c, :]
                k = k_ref[i, rows, :]
                b = b_ref[i, rows, :]
                e_neg = jnp.exp(-cum)
                rt = (r_ref[i, rows, :] * jnp.exp(cum)).astype(BF16)
                at = (kk_ref[i, rows, :] * jnp.exp(cum - logw)).astype(BF16)
                bt = (b * e_neg).astype(BF16)
                kt = (k * e_neg).astype(BF16)
                d_end = jnp.exp(cum_end - cum)
                bh = (b * d_end).astype(BF16)
                kh = (k * d_end).astype(BF16)
                p_end = jnp.exp(cum_end)
                v = v_ref[i, rows, :]
                for hd in range(N_HEADS):
                    lp = slice((hd // 2) * pair, (hd // 2 + 1) * pair)
                    m = own[hd % 2]
                    if hd % 2 == 0:
                        xr_pair = jnp.concatenate([bt[:, lp], kt[:, lp]], axis=0)
                    xr.append(xr_pair)
                    xl.append(jnp.concatenate([jnp.where(m, at[:, lp], zero_b),
                                               jnp.where(m, rt[:, lp], zero_b)], axis=0))
                    xe.append(jnp.concatenate([jnp.where(m, bh[:, lp], zero_b),
                                               jnp.where(m, kh[:, lp], zero_b)], axis=0))
                    vh.append(v[:, hd * HEAD_SIZE:(hd + 1) * HEAD_SIZE])
                    pe.append(p_end[:, lp])
        n = len(xl)

        gram = [_dg(xl[j], xr[j], _NT) for j in range(n)]
        tinv = _unit_lower_inverse([jnp.where(strict, g[:c, :c], 0.0) for g in gram], c)
        yk = [_dot1(jnp.where(strict, gram[j][:c, c:], 0.0), vh[j]) for j in range(n)]
        wu = [_dg(tinv[j].astype(BF16),
                  jnp.concatenate([xl[j][:c], (-yk[j]).astype(BF16)], axis=1)) for j in range(n)]
        w_b = [wu[j][:, :pair].astype(BF16) for j in range(n)]
        r_b = [xl[j][c:] for j in range(n)]
        u0t = [wu[j][:, pair:].T for j in range(n)]
        vt = [x.T.astype(BF16) for x in vh]
        m_r = [jnp.where(incl2, g[c:, :], 0.0).astype(BF16) for g in gram]
        return w_b, r_b, u0t, vt, m_r, xe, pe

    w_b, r_b, u0t, vt, m_r, xe, pe = precompute(range(nc))

    nh = nb * N_HEADS
    st = [s_scr[jj // N_HEADS, jj % N_HEADS] for jj in range(nh)]
    gain = gain_ref[...]
    bias = bias_ref[...]
    for q in range(nc):
        rows = slice(q * c, (q + 1) * c)
        js = [q * nh + jj for jj in range(nh)]
        sb = [s.astype(BF16) for s in st]
        ut = [u0t[j] - _dg(sb[jj], w_b[j], _NT) for jj, j in enumerate(js)]
        rst = [_dg(sb[jj], r_b[j], _NT) for jj, j in enumerate(js)]
        uvt = [jnp.concatenate([ut[jj].astype(BF16), vt[j]], axis=1)
               for jj, j in enumerate(js)]
        st = [st[jj] * pe[j] + _dg(uvt[jj], xe[j]) for jj, j in enumerate(js)]
        ot = [rst[jj] + _dg(uvt[jj], m_r[j], _NT) for jj, j in enumerate(js)]
        mu = [jnp.mean(x, axis=0, keepdims=True) for x in ot]
        dev = [x - m for x, m in zip(ot, mu)]
        var = [jnp.mean(jnp.square(d), axis=0, keepdims=True) for d in dev]
        on = [(d * lax.rsqrt(s + GN_EPS)).T for d, s in zip(dev, var)]
        for jj in range(0, nh, 2):
            i, hd = jj // N_HEADS, jj % N_HEADS
            lp = slice(hd * HEAD_SIZE, (hd + 2) * HEAD_SIZE)
            on_pair = jnp.concatenate([on[jj], on[jj + 1]], axis=1)
            y_ref[i, rows, lp] = ((on_pair * gain[:, lp] + bias[:, lp]) * g_ref[i, rows, lp]
                                  + bg_ref[i, rows, lp])

    for jj in range(nh):
        s_scr[jj // N_HEADS, jj % N_HEADS] = st[jj]

    @pl.when(pl.program_id(1) == pl.num_programs(1) - 1)
    def _():
        for jj in range(nh):
            hd = jj % N_HEADS
            sout_ref[jj // N_HEADS, hd] = st[jj][:, (hd % 2) * HEAD_SIZE:(hd % 2 + 1) * HEAD_SIZE]


def _wkv_call(seqs, s0, gain, bias, *, nb, c, nc):
    B, T, _ = seqs[0].shape
    tok = pl.BlockSpec((nb, nc * c, RWKV_DIM), lambda b, t: (b, t, 0))
    st = pl.BlockSpec((nb, N_HEADS, HEAD_SIZE, HEAD_SIZE), lambda b, t: (b, 0, 0, 0))
    vec = pl.BlockSpec((1, RWKV_DIM), lambda b, t: (0, 0))
    return pl.pallas_call(
        functools.partial(_wkv_kernel, nb=nb, c=c, nc=nc),
        grid=(B // nb, T // (nc * c)),
        in_specs=[tok] * 8 + [st, vec, vec],
        out_specs=[tok, st],
        out_shape=[jax.ShapeDtypeStruct((B, T, RWKV_DIM), F32),
                   jax.ShapeDtypeStruct((B, N_HEADS, HEAD_SIZE, HEAD_SIZE), F32)],
        scratch_shapes=[pltpu.VMEM((nb, N_HEADS, HEAD_SIZE, 2 * HEAD_SIZE), F32)],
        compiler_params=pltpu.CompilerParams(
            dimension_semantics=("arbitrary", "arbitrary"), vmem_limit_bytes=VMEM_LIMIT),
        name="wkv",
    )(*seqs, s0, gain, bias)


def _out_stream(rows, x_ref, yc_ref, yr_ref, wout_ref, gpost_ref, gfpre_ref, gfpost_ref,
                wff1_ref, wff2_ref, o_ref):
    wout = wout_ref[...]
    mix = (_dg(yc_ref[rows, :].astype(BF16), wout[:CONV_DIM])
           + _dg(yr_ref[rows, :].astype(BF16), wout[CONV_DIM:]))
    yield
    x1 = x_ref[rows, :] + _rmsnorm(mix, gpost_ref[...])
    h2 = _rmsnorm(x1, gfpre_ref[...]).astype(BF16)
    yield
    f1 = _dg(h2, wff1_ref[...])
    yield
    f1 = jnp.square(jnp.maximum(f1, 0.0)).astype(BF16)
    yield
    f2 = _dg(f1, wff2_ref[...])
    yield
    o_ref[rows, :] = x1 + _rmsnorm(f2, gfpost_ref[...])


def _out_kernel(*refs, tm, ns):
    per = tm // ns
    _interleave([_out_stream(slice(i * per, (i + 1) * per), *refs) for i in range(ns)])


def _out_call(x, yc, yr, params, *, tm, ns):
    n = x.shape[0]
    tok = lambda w: pl.BlockSpec((tm, w), lambda i: (i, 0))
    full = lambda arr: pl.BlockSpec(arr.shape, lambda i: (0,) * arr.ndim,
                                    pipeline_mode=pl.Buffered(1))
    return pl.pallas_call(
        functools.partial(_out_kernel, tm=tm, ns=ns),
        grid=(n // tm,),
        in_specs=[tok(D_MODEL), tok(CONV_DIM), tok(RWKV_DIM)] + [full(p) for p in params],
        out_specs=tok(D_MODEL),
        out_shape=jax.ShapeDtypeStruct((n, D_MODEL), F32),
        compiler_params=pltpu.CompilerParams(
            dimension_semantics=("arbitrary",), vmem_limit_bytes=VMEM_LIMIT),
        name="outffn",
    )(x, yc, yr, *params)


def _tiles(B, T):
    if T >= 256:
        return 2, 256, 4, 2, 64, 4, 512, 2
    return B, T, 1, B, T, 1, B * T, 1


def _layer(x, conv_buf, shift_prev, wkv_state, proj_params, gn, norms, out_w):
    B, T, _ = x.shape
    pnb, ptt, pns, wnb, wc, wnc, tm, ons = _tiles(B, T)
    cast = [w for w in out_w if w.dtype != BF16]
    outs = _proj_call(x, conv_buf, shift_prev, proj_params, cast,
                      nb=pnb, tt=ptt, ns=pns)
    yconv, seqs, new_conv, new_shift = outs[0], outs[1:9], outs[9], outs[10]
    if cast:
        out_w = tuple(outs[11:])
    yr, new_wkv = _wkv_call(seqs, wkv_state, *gn, nb=wnb, c=wc, nc=wnc)
    out_params = (out_w[0], *norms, out_w[1], out_w[2])
    y = _out_call(x.reshape(B * T, D_MODEL), yconv.reshape(B * T, CONV_DIM),
                  yr.reshape(B * T, RWKV_DIM), out_params, tm=tm, ns=ons)
    return (y.reshape(B, T, D_MODEL), new_conv, new_shift, new_wkv,
            out_w)


def kernel(x_prompt, x_sample, state_conv, state_shift, state_wkv, norm_mix_pre, norm_mix_post,
           norm_ffn_pre, norm_ffn_post, w_in, conv_w, shift_mu, w_decay2, decay_w0, w_a2, a0,
           w_g2, k_k, k_a, r_k, gn_gain, gn_bias, w_out, w_ff1, w_ff2):
    depth = w_in.shape[0]
    Bp = x_prompt.shape[0]
    xp, xs = x_prompt, x_sample
    row = lambda t: t.reshape(1, -1).astype(F32)
    res = [[] for _ in range(6)]
    for l in range(depth):
        proj_params = (
            row(norm_mix_pre[l]), w_in[l].astype(BF16), conv_w[l], row(shift_mu[l]),
            w_decay2[l], row(decay_w0[l]), w_a2[l], row(a0[l]), w_g2[l],
            row(k_k[l]), row(k_a[l]), row(r_k[l]))
        gn = (row(gn_gain[l]), row(gn_bias[l]))
        norms = (row(norm_mix_post[l]), row(norm_ffn_pre[l]), row(norm_ffn_post[l]))
        xp, c_p, s_p, w_p, out_w = _layer(
            xp, jnp.zeros((Bp, CONV_K - 1, CONV_DIM), F32), jnp.zeros((Bp, RWKV_PROJ), F32),
            jnp.zeros((Bp, N_HEADS, HEAD_SIZE, HEAD_SIZE), F32), proj_params, gn, norms,
            (w_out[l], w_ff1[l], w_ff2[l]))
        xs, c_s, s_s, w_s, _ = _layer(xs, state_conv[l], state_shift[l], state_wkv[l],
                                      proj_params, gn, norms, out_w)
        for lst, val in zip(res, (c_p, s_p, w_p, c_s, s_s, w_s)):
            lst.append(val)
    return (xp, xs) + tuple(jnp.stack(r) for r in res)
```

```python
import functools

import jax
import jax.numpy as jnp
import numpy as np
from jax import lax
from jax.experimental import pallas as pl
from jax.experimental.pallas import tpu as pltpu

D_MODEL = 1024
CONV_DIM = 512
RWKV_DIM = 512
HEAD_SIZE = 64
N_HEADS = 8
CONV_K = 3
DECAY_LORA = 64
AAA_LORA = 64
GATE_LORA = 128
RWKV_PROJ = 3 * RWKV_DIM + DECAY_LORA + AAA_LORA + GATE_LORA
PROJ_DIM = 3 * CONV_DIM + RWKV_PROJ
NORM_EPS = 1e-6
GN_EPS = 64e-5
DECAY_SCALE = float(np.exp(-0.5))

SUBLANES = 8
LANES = 128
INV_BASE = 8
VMEM_LIMIT = 56 * 1024 * 1024

F32 = jnp.float32
BF16 = jnp.bfloat16

_NN = (((1,), (0,)), ((), ()))
_NT = (((1,), (1,)), ((), ()))


def _dg(a, b, dims=_NN):
    return lax.dot_general(a, b, dims, preferred_element_type=F32)


def _dot1(a, b, dims=_NN):
    return _dg(a.astype(BF16), b.astype(BF16), dims)


def _bf16_terms(a, n):
    terms = []
    for _ in range(n - 1):
        t = a.astype(BF16)
        terms.append(t)
        a = a - t.astype(F32)
    terms.append(a.astype(BF16))
    return terms


def _sigmoid(x):
    return 0.5 * jnp.tanh(0.5 * x) + 0.5


def _interleave(streams):
    live = list(streams)
    while live:
        live = [s for s in live if next(s, True) is None]


def _head_sums(x):
    pair = LANES
    low = lax.broadcasted_iota(jnp.int32, (x.shape[0], pair), 1) < HEAD_SIZE
    tiles = []
    for p in range(RWKV_DIM // pair):
        t = x[:, p * pair:(p + 1) * pair]
        s_lo = jnp.sum(jnp.where(low, t, 0.0), axis=-1, keepdims=True)
        s_hi = jnp.sum(jnp.where(low, 0.0, t), axis=-1, keepdims=True)
        tiles.append(jnp.where(low, s_lo, s_hi))
    return jnp.concatenate(tiles, axis=1)


def _rmsnorm(x, g):
    return x * lax.rsqrt(jnp.mean(x * x, axis=-1, keepdims=True) + NORM_EPS) * g


def _proj_stream(sq, rq, tt, x_ref, gpre_ref, win_ref, convw_ref, mu_ref, wd_ref, w0_ref, wa_ref,
                 a0_ref, wg_ref, kk_ref, ka_ref, rk_ref, yconv_ref, r_ref, k_ref,
                 v_ref, kkn_ref, b_ref, lw_ref, g_ref, bg_ref, nconv_ref, nshift_ref, ubuf, zbuf):
    nbs = sq.stop - sq.start
    rn = rq.stop - rq.start
    m = nbs * rn
    lo = SUBLANES + rq.start
    hi = lo + rn

    x = x_ref[sq, rq].reshape(m, D_MODEL)
    h = _rmsnorm(x, gpre_ref[...]).astype(BF16)
    z = _dg(h, win_ref[...])
    zb = z[:, :CONV_DIM]
    zc = z[:, CONV_DIM:2 * CONV_DIM]
    zh = z[:, 2 * CONV_DIM:3 * CONV_DIM]
    zr = z[:, 3 * CONV_DIM:]
    yield

    ubuf[sq, lo:hi, :] = (zc * zh).reshape(nbs, rn, CONV_DIM)
    cw = convw_ref[...]
    yc = (ubuf[sq, lo - 2:hi - 2, :] * cw[0:1, :]
          + ubuf[sq, lo - 1:hi - 1, :] * cw[1:2, :]
          + ubuf[sq, lo:hi, :] * cw[2:3, :])
    yconv_ref[sq, rq] = (zb.reshape(nbs, rn, CONV_DIM) * yc)

    zbuf[sq, lo:hi, :] = zr.reshape(nbs, rn, RWKV_PROJ)
    zprev = zbuf[sq, lo - 1:hi - 1, :].reshape(m, RWKV_PROJ)
    zs = zr + mu_ref[...] * (zprev - zr)

    if rq.stop == tt:
        last_u = ubuf[sq, hi - 2:hi, :]
        nconv_ref[sq] = last_u
        ubuf[sq, SUBLANES - 2:SUBLANES, :] = last_u
        for i in range(sq.start, sq.stop):
            last_z = zbuf[i, hi - 1:hi, :]
            nshift_ref[i:i + 1, :] = last_z
            zbuf[i, SUBLANES - 1:SUBLANES, :] = last_z

    r = zs[:, :RWKV_DIM]
    k = zs[:, RWKV_DIM:2 * RWKV_DIM]
    v = zs[:, 2 * RWKV_DIM:3 * RWKV_DIM]
    lwa = zs[:, 3 * RWKV_DIM:3 * RWKV_DIM + DECAY_LORA + AAA_LORA]
    lg = zs[:, 3 * RWKV_DIM + DECAY_LORA + AAA_LORA:]
    yield

    logit_w = w0_ref[...] + _dot1(jnp.tanh(lwa[:, :DECAY_LORA]), wd_ref[...])
    logw = -DECAY_SCALE * _sigmoid(logit_w)
    a = _sigmoid(a0_ref[...] + _dot1(lwa[:, DECAY_LORA:], wa_ref[...]))
    g = _dot1(_sigmoid(lg), wg_ref[...])
    yield

    kk = k * kk_ref[...]
    ss = _head_sums(kk * kk)
    kp = k * (1.0 + (a - 1.0) * ka_ref[...])
    bsum = _head_sums(r * kp * rk_ref[...])
    yield

    kk = kk * lax.rsqrt(jnp.maximum(ss, 1e-24))
    shp = (nbs, rn, RWKV_DIM)
    r_ref[sq, rq] = r.reshape(shp)
    k_ref[sq, rq] = kp.reshape(shp)
    v_ref[sq, rq] = v.reshape(shp)
    kkn_ref[sq, rq] = kk.reshape(shp)
    b_ref[sq, rq] = (kk * a).reshape(shp)
    lw_ref[sq, rq] = logw.reshape(shp)
    g_ref[sq, rq] = g.reshape(shp)
    bg_ref[sq, rq] = (bsum * v * g).reshape(shp)


def _proj_kernel(x_ref, cbuf_ref, sprev_ref, *rest, nb, tt, ns, ncast):
    ubuf, zbuf = rest[-2:]
    params, cast_in = rest[:12], rest[12:12 + ncast]
    outs, cast_out = rest[12 + ncast:23 + ncast], rest[23 + ncast:23 + 2 * ncast]

    @pl.when(pl.program_id(1) == 0)
    def _():
        ubuf[:, SUBLANES - 2:SUBLANES, :] = cbuf_ref[...]
        for i in range(nb):
            zbuf[i, SUBLANES - 1:SUBLANES, :] = sprev_ref[i:i + 1, :]

    for src, dst in zip(cast_in, cast_out):
        dst[...] = src[...].astype(BF16)

    nseq = min(ns, nb)
    per, rn = nb // nseq, tt // (ns // nseq)
    _interleave([_proj_stream(slice(i * per, (i + 1) * per), slice(r0, r0 + rn), tt, x_ref,
                              *params, *outs, ubuf, zbuf)
                 for i in range(nseq) for r0 in range(0, tt, rn)])


def _proj_call(x, conv_buf, shift_prev, params, cast_srcs, *, nb, tt, ns):
    B, T, _ = x.shape
    nt = T // tt
    grid = (B // nb, nt)
    steps = grid[0] * nt
    tok = lambda w: pl.BlockSpec((nb, tt, w), lambda b, t: (b, t, 0))
    full = lambda arr: pl.BlockSpec(arr.shape, lambda b, t: (0,) * arr.ndim)
    rows = lambda arr: pl.BlockSpec((arr.shape[0] // steps, arr.shape[1]),
                                    lambda b, t: (b * nt + t, 0))
    in_specs = [tok(D_MODEL),
                pl.BlockSpec((nb, CONV_K - 1, CONV_DIM), lambda b, t: (b, 0, 0)),
                pl.BlockSpec((nb, RWKV_PROJ), lambda b, t: (b, 0))]
    in_specs += [full(p) for p in params] + [rows(w) for w in cast_srcs]
    seq = lambda w: jax.ShapeDtypeStruct((B, T, w), F32)
    out_shape = [seq(CONV_DIM)] + [seq(RWKV_DIM)] * 8 + [
        jax.ShapeDtypeStruct((B, CONV_K - 1, CONV_DIM), F32),
        jax.ShapeDtypeStruct((B, RWKV_PROJ), F32)]
    out_shape += [jax.ShapeDtypeStruct(w.shape, BF16) for w in cast_srcs]
    out_specs = [tok(CONV_DIM)] + [tok(RWKV_DIM)] * 8 + [
        pl.BlockSpec((nb, CONV_K - 1, CONV_DIM), lambda b, t: (b, 0, 0)),
        pl.BlockSpec((nb, RWKV_PROJ), lambda b, t: (b, 0))]
    out_specs += [rows(w) for w in cast_srcs]
    return pl.pallas_call(
        functools.partial(_proj_kernel, nb=nb, tt=tt, ns=ns, ncast=len(cast_srcs)),
        grid=grid, in_specs=in_specs, out_specs=out_specs, out_shape=out_shape,
        scratch_shapes=[pltpu.VMEM((nb, tt + SUBLANES, CONV_DIM), F32),
                        pltpu.VMEM((nb, tt + SUBLANES, RWKV_PROJ), F32)],
        compiler_params=pltpu.CompilerParams(
            dimension_semantics=("arbitrary", "arbitrary"), vmem_limit_bytes=VMEM_LIMIT),
        name="proj",
    )(x, conv_buf, shift_prev, *params, *cast_srcs)


def _unit_lower_inverse(lows, c):
    base = min(INV_BASE, c)
    nblk = c // base
    prow = lax.broadcasted_iota(jnp.int32, (base, c), 0)
    pcol = lax.broadcasted_iota(jnp.int32, (base, c), 1)
    lane_blk = pcol // base
    eye_p = (pcol - lane_blk * base == prow).astype(F32)

    def pack(m):
        return sum(jnp.where(lane_blk == i, m[i * base:(i + 1) * base, :], 0.0)
                   for i in range(nblk))

    def expand(p):
        return jnp.concatenate([jnp.where(lane_blk == i, p, 0.0) for i in range(nblk)], axis=0)

    ps = [pack(low) for low in lows]
    invs = [eye_p - p for p in ps]
    ps = [_dot1(p, expand(p)) for p in ps]
    span = 4
    while span < base:
        both = [_dot1(jnp.concatenate([inv, p], axis=0), expand(p)) for inv, p in zip(invs, ps)]
        invs = [inv + b[:base] for inv, b in zip(invs, both)]
        ps = [b[base:] for b in both]
        span *= 2
    invs = [inv + _dot1(inv, expand(p)) for inv, p in zip(invs, ps)]
    invs = [expand(inv) for inv in invs]

    row = lax.broadcasted_iota(jnp.int32, (c, c), 0)
    col = lax.broadcasted_iota(jnp.int32, (c, c), 1)
    blk = base
    while blk < c:
        sel = ((row // (2 * blk)) == (col // (2 * blk))) & ((row // blk) != (col // blk))
        odd = [slice((2 * b + 1) * blk, (2 * b + 2) * blk) for b in range(c // (2 * blk))]
        zero = jnp.zeros((blk, c), F32)

        def take(m):
            return jnp.concatenate([m[s] for s in odd], axis=0)

        def place(mr):
            parts = []
            for b in range(len(odd)):
                parts += [zero, mr[b * blk:(b + 1) * blk]]
            return jnp.concatenate(parts, axis=0)

        tmp = [place(_dot1(take(jnp.where(sel, low, 0.0)), inv)) for low, inv in zip(lows, invs)]
        invs = [inv - place(_dot1(take(inv), t)) for inv, t in zip(invs, tmp)]
        blk *= 2
    return invs


def _wkv_kernel(r_ref, k_ref, v_ref, kk_ref, b_ref, lw_ref, g_ref, bg_ref, s0_ref,
                gain_ref, bias_ref, y_ref, sout_ref, s_scr, *, nb, c, nc):
    assert 2 * HEAD_SIZE == LANES
    pair = LANES
    zpad = jnp.zeros((HEAD_SIZE, HEAD_SIZE), F32)

    @pl.when(pl.program_id(1) == 0)
    def _():
        for i in range(nb):
            for hd in range(N_HEADS):
                s0 = s0_ref[i, hd]
                s_scr[i, hd] = jnp.concatenate([s0, zpad] if hd % 2 == 0 else [zpad, s0], axis=1)

    row = lax.broadcasted_iota(jnp.int32, (c, c), 0)
    col = lax.broadcasted_iota(jnp.int32, (c, c), 1)
    strict = row > col
    incl = row >= col
    row2 = lax.broadcasted_iota(jnp.int32, (c, 2 * c), 0)
    col2 = lax.broadcasted_iota(jnp.int32, (c, 2 * c), 1)
    incl2 = row2 >= jnp.where(col2 >= c, col2 - c, col2)
    tri = incl.astype(BF16)
    lane_half = lax.broadcasted_iota(jnp.int32, (c, pair), 1) // HEAD_SIZE
    own = [lane_half == 0, lane_half == 1]
    zero_b = jnp.zeros((c, pair), BF16)

    def precompute(chunks):
        xl, xr, xe, vh, pe = [], [], [], [], []
        for q in chunks:
            rows = slice(q * c, (q + 1) * c)
            for i in range(nb):
                logw = lw_ref[i, rows, :]
                cum = sum(_dg(tri, t) for t in _bf16_terms(logw, 2))
                cum_end = cum[c - 1:c, :]
                k = k_ref[i, rows, :]
                b = b_ref[i, rows, :]
                e_neg = jnp.exp(-cum)
                rt = (r_ref[i, rows, :] * jnp.exp(cum)).astype(BF16)
                at = (kk_ref[i, rows, :] * jnp.exp(cum - logw)).astype(BF16)
                bt = (b * e_neg).astype(BF16)
                kt = (k * e_neg).astype(BF16)
                d_end = jnp.exp(cum_end - cum)
                bh = (b * d_end).astype(BF16)
                kh = (k * d_end).astype(BF16)
                p_end = jnp.exp(cum_end)
                v = v_ref[i, rows, :]
                for hd in range(N_HEADS):
                    lp = slice((hd // 2) * pair, (hd // 2 + 1) * pair)
                    m = own[hd % 2]
                    if hd % 2 == 0:
                        xr_pair = jnp.concatenate([bt[:, lp], kt[:, lp]], axis=0)
                    xr.append(xr_pair)
                    xl.append(jnp.concatenate([jnp.where(m, at[:, lp], zero_b),
                                               jnp.where(m, rt[:, lp], zero_b)], axis=0))
                    xe.append(jnp.concatenate([jnp.where(m, bh[:, lp], zero_b),
                                               jnp.where(m, kh[:, lp], zero_b)], axis=0))
                    vh.append(v[:, hd * HEAD_SIZE:(hd + 1) * HEAD_SIZE])
                    pe.append(p_end[:, lp])
        n = len(xl)

        gram = [_dg(xl[j], xr[j], _NT) for j in range(n)]
        tinv = _unit_lower_inverse([jnp.where(strict, g[:c, :c], 0.0) for g in gram], c)
        yk = [_dot1(jnp.where(strict, gram[j][:c, c:], 0.0), vh[j]) for j in range(n)]
        wu = [_dg(tinv[j].astype(BF16),
                  jnp.concatenate([xl[j][:c], (-yk[j]).astype(BF16)], axis=1)) for j in range(n)]
        w_b = [wu[j][:, :pair].astype(BF16) for j in range(n)]
        r_b = [xl[j][c:] for j in range(n)]
        u0t = [wu[j][:, pair:].T for j in range(n)]
        vt = [x.T.astype(BF16) for x in vh]
        m_r = [jnp.where(incl2, g[c:, :], 0.0).astype(BF16) for g in gram]
        return w_b, r_b, u0t, vt, m_r, xe, pe

    w_b, r_b, u0t, vt, m_r, xe, pe = precompute(range(nc))

    nh = nb * N_HEADS
    st = [s_scr[jj // N_HEADS, jj % N_HEADS] for jj in range(nh)]
    gain = gain_ref[...]
    bias = bias_ref[...]
    for q in range(nc):
        rows = slice(q * c, (q + 1) * c)
        js = [q * nh + jj for jj in range(nh)]
        sb = [s.astype(BF16) for s in st]
        ut = [u0t[j] - _dg(sb[jj], w_b[j], _NT) for jj, j in enumerate(js)]
        rst = [_dg(sb[jj], r_b[j], _NT) for jj, j in enumerate(js)]
        uvt = [jnp.concatenate([ut[jj].astype(BF16), vt[j]], axis=1)
               for jj, j in enumerate(js)]
        st = [st[jj] * pe[j] + _dg(uvt[jj], xe[j]) for jj, j in enumerate(js)]
        ot = [rst[jj] + _dg(uvt[jj], m_r[j], _NT) for jj, j in enumerate(js)]
        mu = [jnp.mean(x, axis=0, keepdims=True) for x in ot]
        dev = [x - m for x, m in zip(ot, mu)]
        var = [jnp.mean(jnp.square(d), axis=0, keepdims=True) for d in dev]
        on = [(d * lax.rsqrt(s + GN_EPS)).T for d, s in zip(dev, var)]
        for jj in range(0, nh, 2):
            i, hd = jj // N_HEADS, jj % N_HEADS
            lp = slice(hd * HEAD_SIZE, (hd + 2) * HEAD_SIZE)
            on_pair = jnp.concatenate([on[jj], on[jj + 1]], axis=1)
            y_ref[i, rows, lp] = ((on_pair * gain[:, lp] + bias[:, lp]) * g_ref[i, rows, lp]
                                  + bg_ref[i, rows, lp])

    for jj in range(nh):
        s_scr[jj // N_HEADS, jj % N_HEADS] = st[jj]

    @pl.when(pl.program_id(1) == pl.num_programs(1) - 1)
    def _():
        for jj in range(nh):
            hd = jj % N_HEADS
            sout_ref[jj // N_HEADS, hd] = st[jj][:, (hd % 2) * HEAD_SIZE:(hd % 2 + 1) * HEAD_SIZE]


def _wkv_call(seqs, s0, gain, bias, *, nb, c, nc):
    B, T, _ = seqs[0].shape
    tok = pl.BlockSpec((nb, nc * c, RWKV_DIM), lambda b, t: (b, t, 0))
    st = pl.BlockSpec((nb, N_HEADS, HEAD_SIZE, HEAD_SIZE), lambda b, t: (b, 0, 0, 0))
    vec = pl.BlockSpec((1, RWKV_DIM), lambda b, t: (0, 0))
    return pl.pallas_call(
        functools.partial(_wkv_kernel, nb=nb, c=c, nc=nc),
        grid=(B // nb, T // (nc * c)),
        in_specs=[tok] * 8 + [st, vec, vec],
        out_specs=[tok, st],
        out_shape=[jax.ShapeDtypeStruct((B, T, RWKV_DIM), F32),
                   jax.ShapeDtypeStruct((B, N_HEADS, HEAD_SIZE, HEAD_SIZE), F32)],
        scratch_shapes=[pltpu.VMEM((nb, N_HEADS, HEAD_SIZE, 2 * HEAD_SIZE), F32)],
        compiler_params=pltpu.CompilerParams(
            dimension_semantics=("arbitrary", "arbitrary"), vmem_limit_bytes=VMEM_LIMIT),
        name="wkv",
    )(*seqs, s0, gain, bias)


def _out_stream(rows, x_ref, yc_ref, yr_ref, wout_ref, gpost_ref, gfpre_ref, gfpost_ref,
                wff1_ref, wff2_ref, o_ref):
    wout = wout_ref[...]
    mix = (_dg(yc_ref[rows, :].astype(BF16), wout[:CONV_DIM])
           + _dg(yr_ref[rows, :].astype(BF16), wout[CONV_DIM:]))
    yield
    x1 = x_ref[rows, :] + _rmsnorm(mix, gpost_ref[...])
    h2 = _rmsnorm(x1, gfpre_ref[...]).astype(BF16)
    yield
    f1 = _dg(h2, wff1_ref[...])
    yield
    f1 = jnp.square(jnp.maximum(f1, 0.0)).astype(BF16)
    yield
    f2 = _dg(f1, wff2_ref[...])
    yield
    o_ref[rows, :] = x1 + _rmsnorm(f2, gfpost_ref[...])


def _out_kernel(*refs, tm, ns):
    per = tm // ns
    _interleave([_out_stream(slice(i * per, (i + 1) * per), *refs) for i in range(ns)])


def _out_call(x, yc, yr, params, *, tm, ns):
    n = x.shape[0]
    tok = lambda w: pl.BlockSpec((tm, w), lambda i: (i, 0))
    full = lambda arr: pl.BlockSpec(arr.shape, lambda i: (0,) * arr.ndim,
                                    pipeline_mode=pl.Buffered(1))
    return pl.pallas_call(
        functools.partial(_out_kernel, tm=tm, ns=ns),
        grid=(n // tm,),
        in_specs=[tok(D_MODEL), tok(CONV_DIM), tok(RWKV_DIM)] + [full(p) for p in params],
        out_specs=tok(D_MODEL),
        out_shape=jax.ShapeDtypeStruct((n, D_MODEL), F32),
        compiler_params=pltpu.CompilerParams(
            dimension_semantics=("arbitrary",), vmem_limit_bytes=VMEM_LIMIT),
        name="outffn",
    )(x, yc, yr, *params)


def _tiles(B, T):
    if T >= 256:
        return 2, 256, 4, 2, 64, 4, 512, 2
    return B, T, 1, B, T, 1, B * T, 1


def _layer(x, conv_buf, shift_prev, wkv_state, proj_params, gn, norms, out_w):
    B, T, _ = x.shape
    pnb, ptt, pns, wnb, wc, wnc, tm, ons = _tiles(B, T)
    cast = [w for w in out_w if w.dtype != BF16]
    outs = _proj_call(x, conv_buf, shift_prev, proj_params, cast,
                      nb=pnb, tt=ptt, ns=pns)
    yconv, seqs, new_conv, new_shift = outs[0], outs[1:9], outs[9], outs[10]
    if cast:
        out_w = tuple(outs[11:])
    yr, new_wkv = _wkv_call(seqs, wkv_state, *gn, nb=wnb, c=wc, nc=wnc)
    out_params = (out_w[0], *norms, out_w[1], out_w[2])
    y = _out_call(x.reshape(B * T, D_MODEL), yconv.reshape(B * T, CONV_DIM),
                  yr.reshape(B * T, RWKV_DIM), out_params, tm=tm, ns=ons)
    return (y.reshape(B, T, D_MODEL), new_conv, new_shift, new_wkv,
            out_w)


def kernel(x_prompt, x_sample, state_conv, state_shift, state_wkv, norm_mix_pre, norm_mix_post,
           norm_ffn_pre, norm_ffn_post, w_in, conv_w, shift_mu, w_decay2, decay_w0, w_a2, a0,
           w_g2, k_k, k_a, r_k, gn_gain, gn_bias, w_out, w_ff1, w_ff2):
    depth = w_in.shape[0]
    Bp = x_prompt.shape[0]
    xp, xs = x_prompt, x_sample
    row = lambda t: t.reshape(1, -1).astype(F32)
    res = [[] for _ in range(6)]
    for l in range(depth):
        proj_params = (
            row(norm_mix_pre[l]), w_in[l].astype(BF16), conv_w[l], row(shift_mu[l]),
            w_decay2[l], row(decay_w0[l]), w_a2[l], row(a0[l]), w_g2[l],
            row(k_k[l]), row(k_a[l]), row(r_k[l]))
        gn = (row(gn_gain[l]), row(gn_bias[l]))
        norms = (row(norm_mix_post[l]), row(norm_ffn_pre[l]), row(norm_ffn_post[l]))
        xp, c_p, s_p, w_p, out_w = _layer(
            xp, jnp.zeros((Bp, CONV_K - 1, CONV_DIM), F32), jnp.zeros((Bp, RWKV_PROJ), F32),
            jnp.zeros((Bp, N_HEADS, HEAD_SIZE, HEAD_SIZE), F32), proj_params, gn, norms,
            (w_out[l], w_ff1[l], w_ff2[l]))
        xs, c_s, s_s, w_s, _ = _layer(xs, state_conv[l], state_shift[l], state_wkv[l],
                                      proj_params, gn, norms, out_w)
        for lst, val in zip(res, (c_p, s_p, w_p, c_s, s_s, w_s)):
            lst.append(val)
    return (xp, xs) + tuple(jnp.stack(r) for r in res)
```

```python
import functools

import jax
import jax.numpy as jnp
import numpy as np
from jax import lax
from jax.experimental import pallas as pl
from jax.experimental.pallas import tpu as pltpu

D_MODEL = 1024
CONV_DIM = 512
RWKV_DIM = 512
HEAD_SIZE = 64
N_HEADS = 8
CONV_K = 3
DECAY_LORA = 64
AAA_LORA = 64
GATE_LORA = 128
RWKV_PROJ = 3 * RWKV_DIM + DECAY_LORA + AAA_LORA + GATE_LORA
PROJ_DIM = 3 * CONV_DIM + RWKV_PROJ
NORM_EPS = 1e-6
GN_EPS = 64e-5
DECAY_SCALE = float(np.exp(-0.5))

SUBLANES = 8
LANES = 128
INV_BASE = 8
VMEM_LIMIT = 56 * 1024 * 1024

F32 = jnp.float32
BF16 = jnp.bfloat16

_NN = (((1,), (0,)), ((), ()))
_NT = (((1,), (1,)), ((), ()))


def _dg(a, b, dims=_NN):
    return lax.dot_general(a, b, dims, preferred_element_type=F32)


def _dot1(a, b, dims=_NN):
    return _dg(a.astype(BF16), b.astype(BF16), dims)


def _bf16_terms(a, n):
    terms = []
    for _ in range(n - 1):
        t = a.astype(BF16)
        terms.append(t)
        a = a - t.astype(F32)
    terms.append(a.astype(BF16))
    return terms


def _sigmoid(x):
    return 0.5 * jnp.tanh(0.5 * x) + 0.5


def _interleave(streams):
    live = list(streams)
    while live:
        live = [s for s in live if next(s, True) is None]


def _head_sums(x):
    pair = LANES
    low = lax.broadcasted_iota(jnp.int32, (x.shape[0], pair), 1) < HEAD_SIZE
    tiles = []
    for p in range(RWKV_DIM // pair):
        t = x[:, p * pair:(p + 1) * pair]
        s_lo = jnp.sum(jnp.where(low, t, 0.0), axis=-1, keepdims=True)
        s_hi = jnp.sum(jnp.where(low, 0.0, t), axis=-1, keepdims=True)
        tiles.append(jnp.where(low, s_lo, s_hi))
    return jnp.concatenate(tiles, axis=1)


def _rmsnorm(x, g):
    return x * lax.rsqrt(jnp.mean(x * x, axis=-1, keepdims=True) + NORM_EPS) * g


def _proj_stream(sq, rq, tt, x_ref, gpre_ref, win_ref, convw_ref, mu_ref, wd_ref, w0_ref, wa_ref,
                 a0_ref, wg_ref, kk_ref, ka_ref, rk_ref, yconv_ref, r_ref, k_ref,
                 v_ref, kkn_ref, b_ref, lw_ref, g_ref, bg_ref, nconv_ref, nshift_ref, ubuf, zbuf):
    nbs = sq.stop - sq.start
    rn = rq.stop - rq.start
    m = nbs * rn
    lo = SUBLANES + rq.start
    hi = lo + rn

    x = x_ref[sq, rq].reshape(m, D_MODEL)
    h = _rmsnorm(x, gpre_ref[...]).astype(BF16)
    z = _dg(h, win_ref[...])
    zb = z[:, :CONV_DIM]
    zc = z[:, CONV_DIM:2 * CONV_DIM]
    zh = z[:, 2 * CONV_DIM:3 * CONV_DIM]
    zr = z[:, 3 * CONV_DIM:]
    yield

    ubuf[sq, lo:hi, :] = (zc * zh).reshape(nbs, rn, CONV_DIM)
    cw = convw_ref[...]
    yc = (ubuf[sq, lo - 2:hi - 2, :] * cw[0:1, :]
          + ubuf[sq, lo - 1:hi - 1, :] * cw[1:2, :]
          + ubuf[sq, lo:hi, :] * cw[2:3, :])
    yconv_ref[sq, rq] = (zb.reshape(nbs, rn, CONV_DIM) * yc)

    zbuf[sq, lo:hi, :] = zr.reshape(nbs, rn, RWKV_PROJ)
    zprev = zbuf[sq, lo - 1:hi - 1, :].reshape(m, RWKV_PROJ)
    zs = zr + mu_ref[...] * (zprev - zr)

    if rq.stop == tt:
        last_u = ubuf[sq, hi - 2:hi, :]
        nconv_ref[sq] = last_u
        last_z = [zbuf[i, hi - 1:hi, :] for i in range(sq.start, sq.stop)]
        for i, lz in zip(range(sq.start, sq.stop), last_z):
            nshift_ref[i:i + 1, :] = lz

        @pl.when(pl.program_id(1) < pl.num_programs(1) - 2)
        def _():
            ubuf[sq, SUBLANES - 2:SUBLANES, :] = last_u
            for i, lz in zip(range(sq.start, sq.stop), last_z):
                zbuf[i, SUBLANES - 1:SUBLANES, :] = lz

    r = zs[:, :RWKV_DIM]
    k = zs[:, RWKV_DIM:2 * RWKV_DIM]
    v = zs[:, 2 * RWKV_DIM:3 * RWKV_DIM]
    lwa = zs[:, 3 * RWKV_DIM:3 * RWKV_DIM + DECAY_LORA + AAA_LORA]
    lg = zs[:, 3 * RWKV_DIM + DECAY_LORA + AAA_LORA:]
    yield

    logit_w = w0_ref[...] + _dot1(jnp.tanh(lwa[:, :DECAY_LORA]), wd_ref[...])
    logw = -DECAY_SCALE * _sigmoid(logit_w)
    a = _sigmoid(a0_ref[...] + _dot1(lwa[:, DECAY_LORA:], wa_ref[...]))
    g = _dot1(_sigmoid(lg), wg_ref[...])
    yield

    kk = k * kk_ref[...]
    ss = _head_sums(kk * kk)
    kp = k * (1.0 + (a - 1.0) * ka_ref[...])
    bsum = _head_sums(r * kp * rk_ref[...])
    yield

    kk = kk * lax.rsqrt(jnp.maximum(ss, 1e-24))
    shp = (nbs, rn, RWKV_DIM)
    r_ref[sq, rq] = r.reshape(shp)
    k_ref[sq, rq] = kp.reshape(shp)
    v_ref[sq, rq] = v.reshape(shp)
    kkn_ref[sq, rq] = kk.reshape(shp)
    b_ref[sq, rq] = (kk * a).reshape(shp)
    lw_ref[sq, rq] = logw.reshape(shp)
    g_ref[sq, rq] = g.reshape(shp)
    bg_ref[sq, rq] = (bsum * v * g).reshape(shp)


def _proj_kernel(x_ref, cbuf_ref, sprev_ref, *rest, nb, tt, ns, ncast):
    ubuf, zbuf = rest[-2:]
    params, cast_in = rest[:12], rest[12:12 + ncast]
    outs, cast_out = rest[12 + ncast:23 + ncast], rest[23 + ncast:23 + 2 * ncast]

    @pl.when(pl.program_id(1) == 0)
    def _():
        ubuf[:, SUBLANES - 2:SUBLANES, :] = cbuf_ref[...]
        for i in range(nb):
            zbuf[i, SUBLANES - 1:SUBLANES, :] = sprev_ref[i:i + 1, :]

    for src, dst in zip(cast_in, cast_out):
        dst[...] = src[...].astype(BF16)

    nseq = min(ns, nb)
    per, rn = nb // nseq, tt // (ns // nseq)
    _interleave([_proj_stream(slice(i * per, (i + 1) * per), slice(r0, r0 + rn), tt, x_ref,
                              *params, *outs, ubuf, zbuf)
                 for i in range(nseq) for r0 in range(0, tt, rn)])


def _proj_call(x, conv_buf, shift_prev, params, cast_srcs, *, nb, tt, ns):
    B, T, _ = x.shape
    nt = T // tt
    grid = (B // nb, nt)
    steps = grid[0] * nt
    tok = lambda w: pl.BlockSpec((nb, tt, w), lambda b, t: (b, t, 0))
    full = lambda arr: pl.BlockSpec(arr.shape, lambda b, t: (0,) * arr.ndim)
    rows = lambda arr: pl.BlockSpec((arr.shape[0] // steps, arr.shape[1]),
                                    lambda b, t: (b * nt + t, 0))
    in_specs = [tok(D_MODEL),
                pl.BlockSpec((nb, CONV_K - 1, CONV_DIM), lambda b, t: (b, 0, 0)),
                pl.BlockSpec((nb, RWKV_PROJ), lambda b, t: (b, 0))]
    in_specs += [full(p) for p in params] + [rows(w) for w in cast_srcs]
    seq = lambda w: jax.ShapeDtypeStruct((B, T, w), F32)
    out_shape = [seq(CONV_DIM)] + [seq(RWKV_DIM)] * 8 + [
        jax.ShapeDtypeStruct((B, CONV_K - 1, CONV_DIM), F32),
        jax.ShapeDtypeStruct((B, RWKV_PROJ), F32)]
    out_shape += [jax.ShapeDtypeStruct(w.shape, BF16) for w in cast_srcs]
    out_specs = [tok(CONV_DIM)] + [tok(RWKV_DIM)] * 8 + [
        pl.BlockSpec((nb, CONV_K - 1, CONV_DIM), lambda b, t: (b, 0, 0)),
        pl.BlockSpec((nb, RWKV_PROJ), lambda b, t: (b, 0))]
    out_specs += [rows(w) for w in cast_srcs]
    return pl.pallas_call(
        functools.partial(_proj_kernel, nb=nb, tt=tt, ns=ns, ncast=len(cast_srcs)),
        grid=grid, in_specs=in_specs, out_specs=out_specs, out_shape=out_shape,
        scratch_shapes=[pltpu.VMEM((nb, tt + SUBLANES, CONV_DIM), F32),
                        pltpu.VMEM((nb, tt + SUBLANES, RWKV_PROJ), F32)],
        compiler_params=pltpu.CompilerParams(
            dimension_semantics=("arbitrary", "arbitrary"), vmem_limit_bytes=VMEM_LIMIT),
        name="proj",
    )(x, conv_buf, shift_prev, *params, *cast_srcs)


def _unit_lower_inverse(lows, c):
    base = min(INV_BASE, c)
    nblk = c // base
    prow = lax.broadcasted_iota(jnp.int32, (base, c), 0)
    pcol = lax.broadcasted_iota(jnp.int32, (base, c), 1)
    lane_blk = pcol // base
    eye_p = (pcol - lane_blk * base == prow).astype(F32)

    def pack(m):
        return sum(jnp.where(lane_blk == i, m[i * base:(i + 1) * base, :], 0.0)
                   for i in range(nblk))

    def expand(p):
        return jnp.concatenate([jnp.where(lane_blk == i, p, 0.0) for i in range(nblk)], axis=0)

    ps = [pack(low) for low in lows]
    invs = [eye_p - p for p in ps]
    ps = [_dot1(p, expand(p)) for p in ps]
    span = 4
    while span < base:
        both = [_dot1(jnp.concatenate([inv, p], axis=0), expand(p)) for inv, p in zip(invs, ps)]
        invs = [inv + b[:base] for inv, b in zip(invs, both)]
        ps = [b[base:] for b in both]
        span *= 2
    invs = [inv + _dot1(inv, expand(p)) for inv, p in zip(invs, ps)]
    invs = [expand(inv) for inv in invs]

    row = lax.broadcasted_iota(jnp.int32, (c, c), 0)
    col = lax.broadcasted_iota(jnp.int32, (c, c), 1)
    blk = base
    while blk < c:
        sel = ((row // (2 * blk)) == (col // (2 * blk))) & ((row // blk) != (col // blk))
        odd = [slice((2 * b + 1) * blk, (2 * b + 2) * blk) for b in range(c // (2 * blk))]
        zero = jnp.zeros((blk, c), F32)

        def take(m):
            return jnp.concatenate([m[s] for s in odd], axis=0)

        def place(mr):
            parts = []
            for b in range(len(odd)):
                parts += [zero, mr[b * blk:(b + 1) * blk]]
            return jnp.concatenate(parts, axis=0)

        tmp = [place(_dot1(take(jnp.where(sel, low, 0.0)), inv)) for low, inv in zip(lows, invs)]
        invs = [inv - place(_dot1(take(inv), t)) for inv, t in zip(invs, tmp)]
        blk *= 2
    return invs


def _wkv_kernel(r_ref, k_ref, v_ref, kk_ref, b_ref, lw_ref, g_ref, bg_ref, s0_ref,
                gain_ref, bias_ref, y_ref, sout_ref, s_scr, *, nb, c, nc):
    assert 2 * HEAD_SIZE == LANES
    pair = LANES
    zpad = jnp.zeros((HEAD_SIZE, HEAD_SIZE), F32)

    @pl.when(pl.program_id(1) == 0)
    def _():
        for i in range(nb):
            for hd in range(N_HEADS):
                s0 = s0_ref[i, hd]
                s_scr[i, hd] = jnp.concatenate([s0, zpad] if hd % 2 == 0 else [zpad, s0], axis=1)

    row = lax.broadcasted_iota(jnp.int32, (c, c), 0)
    col = lax.broadcasted_iota(jnp.int32, (c, c), 1)
    strict = row > col
    incl = row >= col
    row2 = lax.broadcasted_iota(jnp.int32, (c, 2 * c), 0)
    col2 = lax.broadcasted_iota(jnp.int32, (c, 2 * c), 1)
    incl2 = row2 >= jnp.where(col2 >= c, col2 - c, col2)
    tri = incl.astype(BF16)
    lane_half = lax.broadcasted_iota(jnp.int32, (c, pair), 1) // HEAD_SIZE
    own = [lane_half == 0, lane_half == 1]
    zero_b = jnp.zeros((c, pair), BF16)

    def precompute(chunks):
        xl, xr, xe, vh, pe = [], [], [], [], []
        for q in chunks:
            rows = slice(q * c, (q + 1) * c)
            for i in range(nb):
                logw = lw_ref[i, rows, :]
                cum = sum(_dg(tri, t) for t in _bf16_terms(logw, 2))
                cum_end = cum[c - 1:c, :]
                k = k_ref[i, rows, :]
                b = b_ref[i, rows, :]
                e_neg = jnp.exp(-cum)
                rt = (r_ref[i, rows, :] * jnp.exp(cum)).astype(BF16)
                at = (kk_ref[i, rows, :] * jnp.exp(cum - logw)).astype(BF16)
                bt = (b * e_neg).astype(BF16)
                kt = (k * e_neg).astype(BF16)
                d_end = jnp.exp(cum_end - cum)
                bh = (b * d_end).astype(BF16)
                kh = (k * d_end).astype(BF16)
                p_end = jnp.exp(cum_end)
                v = v_ref[i, rows, :]
                for hd in range(N_HEADS):
                    lp = slice((hd // 2) * pair, (hd // 2 + 1) * pair)
                    m = own[hd % 2]
                    if hd % 2 == 0:
                        xr_pair = jnp.concatenate([bt[:, lp], kt[:, lp]], axis=0)
                    xr.append(xr_pair)
                    xl.append(jnp.concatenate([jnp.where(m, at[:, lp], zero_b),
                                               jnp.where(m, rt[:, lp], zero_b)], axis=0))
                    xe.append(jnp.concatenate([jnp.where(m, bh[:, lp], zero_b),
                                               jnp.where(m, kh[:, lp], zero_b)], axis=0))
                    vh.append(v[:, hd * HEAD_SIZE:(hd + 1) * HEAD_SIZE])
                    pe.append(p_end[:, lp])
        n = len(xl)

        gram = [_dg(xl[j], xr[j], _NT) for j in range(n)]
        tinv = _unit_lower_inverse([jnp.where(strict, g[:c, :c], 0.0) for g in gram], c)
        yk = [_dot1(jnp.where(strict, gram[j][:c, c:], 0.0), vh[j]) for j in range(n)]
        wu = [_dg(tinv[j].astype(BF16),
                  jnp.concatenate([xl[j][:c], (-yk[j]).astype(BF16)], axis=1)) for j in range(n)]
        w_b = [wu[j][:, :pair].astype(BF16) for j in range(n)]
        r_b = [xl[j][c:] for j in range(n)]
        u0t = [wu[j][:, pair:].T for j in range(n)]
        vt = [x.T.astype(BF16) for x in vh]
        m_r = [jnp.where(incl2, g[c:, :], 0.0).astype(BF16) for g in gram]
        return w_b, r_b, u0t, vt, m_r, xe, pe

    w_b, r_b, u0t, vt, m_r, xe, pe = precompute(range(nc))

    nh = nb * N_HEADS
    st = [s_scr[jj // N_HEADS, jj % N_HEADS] for jj in range(nh)]
    gain = gain_ref[...]
    bias = bias_ref[...]
    for q in range(nc):
        rows = slice(q * c, (q + 1) * c)
        js = [q * nh + jj for jj in range(nh)]
        sb = [s.astype(BF16) for s in st]
        ut = [u0t[j] - _dg(sb[jj], w_b[j], _NT) for jj, j in enumerate(js)]
        rst = [_dg(sb[jj], r_b[j], _NT) for jj, j in enumerate(js)]
        uvt = [jnp.concatenate([ut[jj].astype(BF16), vt[j]], axis=1)
               for jj, j in enumerate(js)]
        st = [st[jj] * pe[j] + _dg(uvt[jj], xe[j]) for jj, j in enumerate(js)]
        ot = [rst[jj] + _dg(uvt[jj], m_r[j], _NT) for jj, j in enumerate(js)]
        mu = [jnp.mean(x, axis=0, keepdims=True) for x in ot]
        dev = [x - m for x, m in zip(ot, mu)]
        var = [jnp.mean(jnp.square(d), axis=0, keepdims=True) for d in dev]
        on = [(d * lax.rsqrt(s + GN_EPS)).T for d, s in zip(dev, var)]
        for jj in range(0, nh, 2):
            i, hd = jj // N_HEADS, jj % N_HEADS
            lp = slice(hd * HEAD_SIZE, (hd + 2) * HEAD_SIZE)
            on_pair = jnp.concatenate([on[jj], on[jj + 1]], axis=1)
            y_ref[i, rows, lp] = ((on_pair * gain[:, lp] + bias[:, lp]) * g_ref[i, rows, lp]
                                  + bg_ref[i, rows, lp])

    for jj in range(nh):
        s_scr[jj // N_HEADS, jj % N_HEADS] = st[jj]

    @pl.when(pl.program_id(1) == pl.num_programs(1) - 1)
    def _():
        for jj in range(nh):
            hd = jj % N_HEADS
            sout_ref[jj // N_HEADS, hd] = st[jj][:, (hd % 2) * HEAD_SIZE:(hd % 2 + 1) * HEAD_SIZE]


def _wkv_call(seqs, s0, gain, bias, *, nb, c, nc):
    B, T, _ = seqs[0].shape
    tok = pl.BlockSpec((nb, nc * c, RWKV_DIM), lambda b, t: (b, t, 0))
    st = pl.BlockSpec((nb, N_HEADS, HEAD_SIZE, HEAD_SIZE), lambda b, t: (b, 0, 0, 0))
    vec = pl.BlockSpec((1, RWKV_DIM), lambda b, t: (0, 0))
    return pl.pallas_call(
        functools.partial(_wkv_kernel, nb=nb, c=c, nc=nc),
        grid=(B // nb, T // (nc * c)),
        in_specs=[tok] * 8 + [st, vec, vec],
        out_specs=[tok, st],
        out_shape=[jax.ShapeDtypeStruct((B, T, RWKV_DIM), F32),
                   jax.ShapeDtypeStruct((B, N_HEADS, HEAD_SIZE, HEAD_SIZE), F32)],
        scratch_shapes=[pltpu.VMEM((nb, N_HEADS, HEAD_SIZE, 2 * HEAD_SIZE), F32)],
        compiler_params=pltpu.CompilerParams(
            dimension_semantics=("arbitrary", "arbitrary"), vmem_limit_bytes=VMEM_LIMIT),
        name="wkv",
    )(*seqs, s0, gain, bias)


N_SEQ = 8


def _front_kernel(*refs, nb, tt, ns, ncast, c, nc):
    n_in = 3 + 12 + ncast
    x_ref, cbuf_ref, sprev_ref = refs[:3]
    params, cast_in = refs[3:15], refs[15:n_in]
    s0_ref, gain_ref, bias_ref = refs[n_in:n_in + 3]
    o = n_in + 3
    yconv_ref, nconv_ref, nshift_ref = refs[o:o + 3]
    cast_out = refs[o + 3:o + 3 + ncast]
    y_ref, sout_ref = refs[o + 3 + ncast:o + 5 + ncast]
    scr = refs[o + 5 + ncast:]
    ubuf, zbuf, seq, s_scr = scr[0], scr[1], scr[2:2 + N_SEQ], scr[2 + N_SEQ]

    @pl.when(pl.program_id(1) == 0)
    def _():
        for buf in seq:
            buf[...] = jnp.zeros_like(buf)

    _wkv_kernel(*seq, s0_ref, gain_ref, bias_ref, y_ref, sout_ref, s_scr, nb=nb, c=c, nc=nc)
    _proj_kernel(x_ref, cbuf_ref, sprev_ref, *params, *cast_in, yconv_ref, *seq, nconv_ref,
                 nshift_ref, *cast_out, ubuf, zbuf, nb=nb, tt=tt, ns=ns, ncast=ncast)


def _front_call(x, conv_buf, shift_prev, wkv_state, params, gn, cast_srcs, *, nb, tt, ns, c, nc):
    B, T, _ = x.shape
    assert B == nb and tt == nc * c
    nt = T // tt
    cur = lambda w: pl.BlockSpec((nb, tt, w), lambda b, t: (b, jnp.minimum(t, nt - 1), 0))
    prev = lambda w: pl.BlockSpec((nb, tt, w), lambda b, t: (b, jnp.maximum(t - 1, 0), 0))
    full = lambda arr: pl.BlockSpec(arr.shape, lambda b, t: (0,) * arr.ndim)
    rows = lambda arr: pl.BlockSpec((arr.shape[0] // nt, arr.shape[1]),
                                    lambda b, t: (jnp.minimum(t, nt - 1), 0))
    conv_st = pl.BlockSpec((nb, CONV_K - 1, CONV_DIM), lambda b, t: (b, 0, 0))
    shift_st = pl.BlockSpec((nb, RWKV_PROJ), lambda b, t: (b, 0))
    wkv_st = pl.BlockSpec((nb, N_HEADS, HEAD_SIZE, HEAD_SIZE), lambda b, t: (b, 0, 0, 0))
    in_specs = [cur(D_MODEL), conv_st, shift_st] + [full(p) for p in params]
    in_specs += [rows(w) for w in cast_srcs] + [wkv_st, full(gn[0]), full(gn[1])]
    out_specs = [cur(CONV_DIM), conv_st, shift_st] + [rows(w) for w in cast_srcs]
    out_specs += [prev(RWKV_DIM), wkv_st]
    out_shape = [jax.ShapeDtypeStruct((B, T, CONV_DIM), F32),
                 jax.ShapeDtypeStruct((B, CONV_K - 1, CONV_DIM), F32),
                 jax.ShapeDtypeStruct((B, RWKV_PROJ), F32)]
    out_shape += [jax.ShapeDtypeStruct(w.shape, BF16) for w in cast_srcs]
    out_shape += [jax.ShapeDtypeStruct((B, T, RWKV_DIM), F32),
                  jax.ShapeDtypeStruct((B, N_HEADS, HEAD_SIZE, HEAD_SIZE), F32)]
    scratch = [pltpu.VMEM((nb, tt + SUBLANES, CONV_DIM), F32),
               pltpu.VMEM((nb, tt + SUBLANES, RWKV_PROJ), F32)]
    scratch += [pltpu.VMEM((nb, tt, RWKV_DIM), F32)] * N_SEQ
    scratch += [pltpu.VMEM((nb, N_HEADS, HEAD_SIZE, 2 * HEAD_SIZE), F32)]
    return pl.pallas_call(
        functools.partial(_front_kernel, nb=nb, tt=tt, ns=ns, ncast=len(cast_srcs), c=c, nc=nc),
        grid=(1, nt + 1), in_specs=in_specs, out_specs=out_specs, out_shape=out_shape,
        scratch_shapes=scratch,
        compiler_params=pltpu.CompilerParams(
            dimension_semantics=("arbitrary", "arbitrary"), vmem_limit_bytes=VMEM_LIMIT),
        name="front",
    )(x, conv_buf, shift_prev, *params, *cast_srcs, wkv_state, *gn)


def _out_stream(rows, x_ref, yc_ref, yr_ref, wout_ref, gpost_ref, gfpre_ref, gfpost_ref,
                wff1_ref, wff2_ref, o_ref):
    wout = wout_ref[...]
    mix = (_dg(yc_ref[rows, :].astype(BF16), wout[:CONV_DIM])
           + _dg(yr_ref[rows, :].astype(BF16), wout[CONV_DIM:]))
    yield
    x1 = x_ref[rows, :] + _rmsnorm(mix, gpost_ref[...])
    h2 = _rmsnorm(x1, gfpre_ref[...]).astype(BF16)
    yield
    f1 = _dg(h2, wff1_ref[...])
    yield
    f1 = jnp.square(jnp.maximum(f1, 0.0)).astype(BF16)
    yield
    f2 = _dg(f1, wff2_ref[...])
    yield
    o_ref[rows, :] = x1 + _rmsnorm(f2, gfpost_ref[...])


def _out_kernel(*refs, tm, ns):
    per = tm // ns
    _interleave([_out_stream(slice(i * per, (i + 1) * per), *refs) for i in range(ns)])


def _out_call(x, yc, yr, params, *, tm, ns):
    n = x.shape[0]
    tok = lambda w: pl.BlockSpec((tm, w), lambda i: (i, 0))
    full = lambda arr: pl.BlockSpec(arr.shape, lambda i: (0,) * arr.ndim,
                                    pipeline_mode=pl.Buffered(1))
    return pl.pallas_call(
        functools.partial(_out_kernel, tm=tm, ns=ns),
        grid=(n // tm,),
        in_specs=[tok(D_MODEL), tok(CONV_DIM), tok(RWKV_DIM)] + [full(p) for p in params],
        out_specs=tok(D_MODEL),
        out_shape=jax.ShapeDtypeStruct((n, D_MODEL), F32),
        compiler_params=pltpu.CompilerParams(
            dimension_semantics=("arbitrary",), vmem_limit_bytes=VMEM_LIMIT),
        name="outffn",
    )(x, yc, yr, *params)


def _tiles(B, T):
    if T >= 256:
        return 2, 256, 4, 2, 64, 4, 512, 2
    return B, T, 1, B, T, 1, B * T, 1


def _layer(x, conv_buf, shift_prev, wkv_state, proj_params, gn, norms, out_w):
    B, T, _ = x.shape
    pnb, ptt, pns, wnb, wc, wnc, tm, ons = _tiles(B, T)
    cast = [w for w in out_w if w.dtype != BF16]
    outs = _front_call(x, conv_buf, shift_prev, wkv_state, proj_params, gn, cast,
                       nb=pnb, tt=ptt, ns=pns, c=wc, nc=wnc)
    yconv, new_conv, new_shift = outs[:3]
    yr, new_wkv = outs[-2:]
    if cast:
        out_w = tuple(outs[3:-2])
    out_params = (out_w[0], *norms, out_w[1], out_w[2])
    y = _out_call(x.reshape(B * T, D_MODEL), yconv.reshape(B * T, CONV_DIM),
                  yr.reshape(B * T, RWKV_DIM), out_params, tm=tm, ns=ons)
    return (y.reshape(B, T, D_MODEL), new_conv, new_shift, new_wkv,
            out_w)


def kernel(x_prompt, x_sample, state_conv, state_shift, state_wkv, norm_mix_pre, norm_mix_post,
           norm_ffn_pre, norm_ffn_post, w_in, conv_w, shift_mu, w_decay2, decay_w0, w_a2, a0,
           w_g2, k_k, k_a, r_k, gn_gain, gn_bias, w_out, w_ff1, w_ff2):
    depth = w_in.shape[0]
    Bp = x_prompt.shape[0]
    xp, xs = x_prompt, x_sample
    row = lambda t: t.reshape(1, -1).astype(F32)
    res = [[] for _ in range(6)]
    for l in range(depth):
        proj_params = (
            row(norm_mix_pre[l]), w_in[l].astype(BF16), conv_w[l], row(shift_mu[l]),
            w_decay2[l], row(decay_w0[l]), w_a2[l], row(a0[l]), w_g2[l],
            row(k_k[l]), row(k_a[l]), row(r_k[l]))
        gn = (row(gn_gain[l]), row(gn_bias[l]))
        norms = (row(norm_mix_post[l]), row(norm_ffn_pre[l]), row(norm_ffn_post[l]))
        xp, c_p, s_p, w_p, out_w = _layer(
            xp, jnp.zeros((Bp, CONV_K - 1, CONV_DIM), F32), jnp.zeros((Bp, RWKV_PROJ), F32),
            jnp.zeros((Bp, N_HEADS, HEAD_SIZE, HEAD_SIZE), F32), proj_params, gn, norms,
            (w_out[l], w_ff1[l], w_ff2[l]))
        xs, c_s, s_s, w_s, _ = _layer(xs, state_conv[l], state_shift[l], state_wkv[l],
                                      proj_params, gn, norms, out_w)
        for lst, val in zip(res, (c_p, s_p, w_p, c_s, s_s, w_s)):
            lst.append(val)
    return (xp, xs) + tuple(jnp.stack(r) for r in res)
```

```python
import functools

import jax
import jax.numpy as jnp
import numpy as np
from jax import lax
from jax.experimental import pallas as pl
from jax.experimental.pallas import tpu as pltpu

D_MODEL = 1024
CONV_DIM = 512
RWKV_DIM = 512
HEAD_SIZE = 64
N_HEADS = 8
CONV_K = 3
DECAY_LORA = 64
AAA_LORA = 64
GATE_LORA = 128
RWKV_PROJ = 3 * RWKV_DIM + DECAY_LORA + AAA_LORA + GATE_LORA
PROJ_DIM = 3 * CONV_DIM + RWKV_PROJ
NORM_EPS = 1e-6
GN_EPS = 64e-5
DECAY_SCALE = float(np.exp(-0.5))

SUBLANES = 8
LANES = 128
INV_BASE = 8
VMEM_LIMIT = 56 * 1024 * 1024

F32 = jnp.float32
BF16 = jnp.bfloat16

_NN = (((1,), (0,)), ((), ()))
_NT = (((1,), (1,)), ((), ()))


def _dg(a, b, dims=_NN):
    return lax.dot_general(a, b, dims, preferred_element_type=F32)


def _dot1(a, b, dims=_NN):
    return _dg(a.astype(BF16), b.astype(BF16), dims)


def _bf16_terms(a, n):
    terms = []
    for _ in range(n - 1):
        t = a.astype(BF16)
        terms.append(t)
        a = a - t.astype(F32)
    terms.append(a.astype(BF16))
    return terms


def _sigmoid(x):
    return 0.5 * jnp.tanh(0.5 * x) + 0.5


def _interleave(streams):
    live = list(streams)
    while live:
        live = [s for s in live if next(s, True) is None]


def _head_sums(x):
    pair = LANES
    low = lax.broadcasted_iota(jnp.int32, (x.shape[0], pair), 1) < HEAD_SIZE
    tiles = []
    for p in range(RWKV_DIM // pair):
        t = x[:, p * pair:(p + 1) * pair]
        s_lo = jnp.sum(jnp.where(low, t, 0.0), axis=-1, keepdims=True)
        s_hi = jnp.sum(jnp.where(low, 0.0, t), axis=-1, keepdims=True)
        tiles.append(jnp.where(low, s_lo, s_hi))
    return jnp.concatenate(tiles, axis=1)


def _rmsnorm(x, g):
    return x * lax.rsqrt(jnp.mean(x * x, axis=-1, keepdims=True) + NORM_EPS) * g


def _proj_stream(sq, rq, tt, x_ref, gpre_ref, win_ref, convw_ref, mu_ref, wd_ref, w0_ref, wa_ref,
                 a0_ref, wg_ref, kk_ref, ka_ref, rk_ref, yconv_ref, r_ref, k_ref,
                 v_ref, kkn_ref, b_ref, lw_ref, g_ref, bg_ref, nconv_ref, nshift_ref, ubuf, zbuf):
    nbs = sq.stop - sq.start
    rn = rq.stop - rq.start
    m = nbs * rn
    lo = SUBLANES + rq.start
    hi = lo + rn

    x = x_ref[sq, rq].reshape(m, D_MODEL)
    h = _rmsnorm(x, gpre_ref[...]).astype(BF16)
    z = _dg(h, win_ref[...])
    zb = z[:, :CONV_DIM]
    zc = z[:, CONV_DIM:2 * CONV_DIM]
    zh = z[:, 2 * CONV_DIM:3 * CONV_DIM]
    zr = z[:, 3 * CONV_DIM:]
    yield

    ubuf[sq, lo:hi, :] = (zc * zh).reshape(nbs, rn, CONV_DIM)
    cw = convw_ref[...]
    yc = (ubuf[sq, lo - 2:hi - 2, :] * cw[0:1, :]
          + ubuf[sq, lo - 1:hi - 1, :] * cw[1:2, :]
          + ubuf[sq, lo:hi, :] * cw[2:3, :])
    yconv_ref[sq, rq] = (zb.reshape(nbs, rn, CONV_DIM) * yc)

    zbuf[sq, lo:hi, :] = zr.reshape(nbs, rn, RWKV_PROJ)
    zprev = zbuf[sq, lo - 1:hi - 1, :].reshape(m, RWKV_PROJ)
    zs = zr + mu_ref[...] * (zprev - zr)

    if rq.stop == tt:
        last_u = ubuf[sq, hi - 2:hi, :]
        nconv_ref[sq] = last_u
        ubuf[sq, SUBLANES - 2:SUBLANES, :] = last_u
        for i in range(sq.start, sq.stop):
            last_z = zbuf[i, hi - 1:hi, :]
            nshift_ref[i:i + 1, :] = last_z
            zbuf[i, SUBLANES - 1:SUBLANES, :] = last_z

    r = zs[:, :RWKV_DIM]
    k = zs[:, RWKV_DIM:2 * RWKV_DIM]
    v = zs[:, 2 * RWKV_DIM:3 * RWKV_DIM]
    lwa = zs[:, 3 * RWKV_DIM:3 * RWKV_DIM + DECAY_LORA + AAA_LORA]
    lg = zs[:, 3 * RWKV_DIM + DECAY_LORA + AAA_LORA:]
    yield

    logit_w = w0_ref[...] + _dot1(jnp.tanh(lwa[:, :DECAY_LORA]), wd_ref[...])
    logw = -DECAY_SCALE * _sigmoid(logit_w)
    a = _sigmoid(a0_ref[...] + _dot1(lwa[:, DECAY_LORA:], wa_ref[...]))
    g = _dot1(_sigmoid(lg), wg_ref[...])
    yield

    kk = k * kk_ref[...]
    ss = _head_sums(kk * kk)
    kp = k * (1.0 + (a - 1.0) * ka_ref[...])
    bsum = _head_sums(r * kp * rk_ref[...])
    yield

    kk = kk * lax.rsqrt(jnp.maximum(ss, 1e-24))
    shp = (nbs, rn, RWKV_DIM)
    r_ref[sq, rq] = r.reshape(shp)
    k_ref[sq, rq] = kp.reshape(shp)
    v_ref[sq, rq] = v.reshape(shp)
    kkn_ref[sq, rq] = kk.reshape(shp)
    b_ref[sq, rq] = (kk * a).reshape(shp)
    lw_ref[sq, rq] = logw.reshape(shp)
    g_ref[sq, rq] = g.reshape(shp)
    bg_ref[sq, rq] = (bsum * v * g).reshape(shp)


def _proj_kernel(x_ref, cbuf_ref, sprev_ref, *rest, nb, tt, ns, ncast):
    ubuf, zbuf = rest[-2:]
    params, cast_in = rest[:12], rest[12:12 + ncast]
    outs, cast_out = rest[12 + ncast:23 + ncast], rest[23 + ncast:23 + 2 * ncast]

    @pl.when(pl.program_id(1) == 0)
    def _():
        ubuf[:, SUBLANES - 2:SUBLANES, :] = cbuf_ref[...]
        for i in range(nb):
            zbuf[i, SUBLANES - 1:SUBLANES, :] = sprev_ref[i:i + 1, :]

    for src, dst in zip(cast_in, cast_out):
        dst[...] = src[...].astype(BF16)

    nseq = min(ns, nb)
    per, rn = nb // nseq, tt // (ns // nseq)
    _interleave([_proj_stream(slice(i * per, (i + 1) * per), slice(r0, r0 + rn), tt, x_ref,
                              *params, *outs, ubuf, zbuf)
                 for i in range(nseq) for r0 in range(0, tt, rn)])


def _proj_call(x, conv_buf, shift_prev, params, cast_srcs, *, nb, tt, ns):
    B, T, _ = x.shape
    nt = T // tt
    grid = (B // nb, nt)
    steps = grid[0] * nt
    tok = lambda w: pl.BlockSpec((nb, tt, w), lambda b, t: (b, t, 0))
    full = lambda arr: pl.BlockSpec(arr.shape, lambda b, t: (0,) * arr.ndim)
    rows = lambda arr: pl.BlockSpec((arr.shape[0] // steps, arr.shape[1]),
                                    lambda b, t: (b * nt + t, 0))
    in_specs = [tok(D_MODEL),
                pl.BlockSpec((nb, CONV_K - 1, CONV_DIM), lambda b, t: (b, 0, 0)),
                pl.BlockSpec((nb, RWKV_PROJ), lambda b, t: (b, 0))]
    in_specs += [full(p) for p in params] + [rows(w) for w in cast_srcs]
    seq = lambda w: jax.ShapeDtypeStruct((B, T, w), F32)
    out_shape = [seq(CONV_DIM)] + [seq(RWKV_DIM)] * 8 + [
        jax.ShapeDtypeStruct((B, CONV_K - 1, CONV_DIM), F32),
        jax.ShapeDtypeStruct((B, RWKV_PROJ), F32)]
    out_shape += [jax.ShapeDtypeStruct(w.shape, BF16) for w in cast_srcs]
    out_specs = [tok(CONV_DIM)] + [tok(RWKV_DIM)] * 8 + [
        pl.BlockSpec((nb, CONV_K - 1, CONV_DIM), lambda b, t: (b, 0, 0)),
        pl.BlockSpec((nb, RWKV_PROJ), lambda b, t: (b, 0))]
    out_specs += [rows(w) for w in cast_srcs]
    return pl.pallas_call(
        functools.partial(_proj_kernel, nb=nb, tt=tt, ns=ns, ncast=len(cast_srcs)),
        grid=grid, in_specs=in_specs, out_specs=out_specs, out_shape=out_shape,
        scratch_shapes=[pltpu.VMEM((nb, tt + SUBLANES, CONV_DIM), F32),
                        pltpu.VMEM((nb, tt + SUBLANES, RWKV_PROJ), F32)],
        compiler_params=pltpu.CompilerParams(
            dimension_semantics=("arbitrary", "arbitrary"), vmem_limit_bytes=VMEM_LIMIT),
        name="proj",
    )(x, conv_buf, shift_prev, *params, *cast_srcs)


def _unit_lower_inverse(lows, c):
    base = min(INV_BASE, c)
    nblk = c // base
    prow = lax.broadcasted_iota(jnp.int32, (base, c), 0)
    pcol = lax.broadcasted_iota(jnp.int32, (base, c), 1)
    lane_blk = pcol // base
    eye_p = (pcol - lane_blk * base == prow).astype(F32)

    def pack(m):
        return sum(jnp.where(lane_blk == i, m[i * base:(i + 1) * base, :], 0.0)
                   for i in range(nblk))

    def expand(p):
        return jnp.concatenate([jnp.where(lane_blk == i, p, 0.0) for i in range(nblk)], axis=0)

    ps = [pack(low) for low in lows]
    invs = [eye_p - p for p in ps]
    ps = [_dot1(p, expand(p)) for p in ps]
    span = 4
    while span < base:
        both = [_dot1(jnp.concatenate([inv, p], axis=0), expand(p)) for inv, p in zip(invs, ps)]
        invs = [inv + b[:base] for inv, b in zip(invs, both)]
        ps = [b[base:] for b in both]
        span *= 2
    invs = [inv + _dot1(inv, expand(p)) for inv, p in zip(invs, ps)]
    invs = [expand(inv) for inv in invs]

    row = lax.broadcasted_iota(jnp.int32, (c, c), 0)
    col = lax.broadcasted_iota(jnp.int32, (c, c), 1)
    blk = base
    while blk < c:
        sel = ((row // (2 * blk)) == (col // (2 * blk))) & ((row // blk) != (col // blk))
        odd = [slice((2 * b + 1) * blk, (2 * b + 2) * blk) for b in range(c // (2 * blk))]
        zero = jnp.zeros((blk, c), F32)

        def take(m):
            return jnp.concatenate([m[s] for s in odd], axis=0)

        def place(mr):
            parts = []
            for b in range(len(odd)):
                parts += [zero, mr[b * blk:(b + 1) * blk]]
            return jnp.concatenate(parts, axis=0)

        tmp = [place(_dot1(take(jnp.where(sel, low, 0.0)), inv)) for low, inv in zip(lows, invs)]
        invs = [inv - place(_dot1(take(inv), t)) for inv, t in zip(invs, tmp)]
        blk *= 2
    return invs


def _wkv_kernel(r_ref, k_ref, v_ref, kk_ref, b_ref, lw_ref, g_ref, bg_ref, s0_ref,
                gain_ref, bias_ref, y_ref, sout_ref, s_scr, *, nb, c, nc):
    assert 2 * HEAD_SIZE == LANES
    pair = LANES
    zpad = jnp.zeros((HEAD_SIZE, HEAD_SIZE), F32)

    @pl.when(pl.program_id(1) == 0)
    def _():
        for i in range(nb):
            for hd in range(N_HEADS):
                s0 = s0_ref[i, hd]
                s_scr[i, hd] = jnp.concatenate([s0, zpad] if hd % 2 == 0 else [zpad, s0], axis=1)

    row = lax.broadcasted_iota(jnp.int32, (c, c), 0)
    col = lax.broadcasted_iota(jnp.int32, (c, c), 1)
    strict = row > col
    incl = row >= col
    row2 = lax.broadcasted_iota(jnp.int32, (c, 2 * c), 0)
    col2 = lax.broadcasted_iota(jnp.int32, (c, 2 * c), 1)
    incl2 = row2 >= jnp.where(col2 >= c, col2 - c, col2)
    tri = incl.astype(BF16)
    lane_half = lax.broadcasted_iota(jnp.int32, (c, pair), 1) // HEAD_SIZE
    own = [lane_half == 0, lane_half == 1]
    zero_b = jnp.zeros((c, pair), BF16)

    def precompute(chunks):
        xl, xr, xe, vh, pe = [], [], [], [], []
        for q in chunks:
            rows = slice(q * c, (q + 1) * c)
            for i in range(nb):
                logw = lw_ref[i, rows, :]
                cum = sum(_dg(tri, t) for t in _bf16_terms(logw, 2))
                cum_end = cum[c - 1:c, :]
                k = k_ref[i, rows, :]
                b = b_ref[i, rows, :]
                e_neg = jnp.exp(-cum)
                rt = (r_ref[i, rows, :] * jnp.exp(cum)).astype(BF16)
                at = (kk_ref[i, rows, :] * jnp.exp(cum - logw)).astype(BF16)
                bt = (b * e_neg).astype(BF16)
                kt = (k * e_neg).astype(BF16)
                d_end = jnp.exp(cum_end - cum)
                bh = (b * d_end).astype(BF16)
                kh = (k * d_end).astype(BF16)
                p_end = jnp.exp(cum_end)
                v = v_ref[i, rows, :]
                for hd in range(N_HEADS):
                    lp = slice((hd // 2) * pair, (hd // 2 + 1) * pair)
                    m = own[hd % 2]
                    if hd % 2 == 0:
                        xr_pair = jnp.concatenate([bt[:, lp], kt[:, lp]], axis=0)
                    xr.append(xr_pair)
                    xl.append(jnp.concatenate([jnp.where(m, at[:, lp], zero_b),
                                               jnp.where(m, rt[:, lp], zero_b)], axis=0))
                    xe.append(jnp.concatenate([jnp.where(m, bh[:, lp], zero_b),
                                               jnp.where(m, kh[:, lp], zero_b)], axis=0))
                    vh.append(v[:, hd * HEAD_SIZE:(hd + 1) * HEAD_SIZE])
                    pe.append(p_end[:, lp])
        n = len(xl)

        gram = [_dg(xl[j], xr[j], _NT) for j in range(n)]
        tinv = _unit_lower_inverse([jnp.where(strict, g[:c, :c], 0.0) for g in gram], c)
        yk = [_dot1(jnp.where(strict, gram[j][:c, c:], 0.0), vh[j]) for j in range(n)]
        wu = [_dg(tinv[j].astype(BF16),
                  jnp.concatenate([xl[j][:c], (-yk[j]).astype(BF16)], axis=1)) for j in range(n)]
        w_b = [wu[j][:, :pair].astype(BF16) for j in range(n)]
        r_b = [xl[j][c:] for j in range(n)]
        u0t = [wu[j][:, pair:].T for j in range(n)]
        vt = [x.T.astype(BF16) for x in vh]
        m_r = [jnp.where(incl2, g[c:, :], 0.0).astype(BF16) for g in gram]
        return w_b, r_b, u0t, vt, m_r, xe, pe

    w_b, r_b, u0t, vt, m_r, xe, pe = precompute(range(nc))

    nh = nb * N_HEADS
    st = [s_scr[jj // N_HEADS, jj % N_HEADS] for jj in range(nh)]
    gain = gain_ref[...]
    bias = bias_ref[...]
    for q in range(nc):
        rows = slice(q * c, (q + 1) * c)
        js = [q * nh + jj for jj in range(nh)]
        sb = [s.astype(BF16) for s in st]
        ut = [u0t[j] - _dg(sb[jj], w_b[j], _NT) for jj, j in enumerate(js)]
        rst = [_dg(sb[jj], r_b[j], _NT) for jj, j in enumerate(js)]
        uvt = [jnp.concatenate([ut[jj].astype(BF16), vt[j]], axis=1)
               for jj, j in enumerate(js)]
        st = [st[jj] * pe[j] + _dg(uvt[jj], xe[j]) for jj, j in enumerate(js)]
        ot = [rst[jj] + _dg(uvt[jj], m_r[j], _NT) for jj, j in enumerate(js)]
        mu = [jnp.mean(x, axis=0, keepdims=True) for x in ot]
        dev = [x - m for x, m in zip(ot, mu)]
        var = [jnp.mean(jnp.square(d), axis=0, keepdims=True) for d in dev]
        on = [(d * lax.rsqrt(s + GN_EPS)).T for d, s in zip(dev, var)]
        for jj in range(0, nh, 2):
            i, hd = jj // N_HEADS, jj % N_HEADS
            lp = slice(hd * HEAD_SIZE, (hd + 2) * HEAD_SIZE)
            on_pair = jnp.concatenate([on[jj], on[jj + 1]], axis=1)
            y_ref[i, rows, lp] = ((on_pair * gain[:, lp] + bias[:, lp]) * g_ref[i, rows, lp]
                                  + bg_ref[i, rows, lp])

    for jj in range(nh):
        s_scr[jj // N_HEADS, jj % N_HEADS] = st[jj]

    @pl.when(pl.program_id(1) == pl.num_programs(1) - 1)
    def _():
        for jj in range(nh):
            hd = jj % N_HEADS
            sout_ref[jj // N_HEADS, hd] = st[jj][:, (hd % 2) * HEAD_SIZE:(hd % 2 + 1) * HEAD_SIZE]


def _wkv_call(seqs, s0, gain, bias, *, nb, c, nc):
    B, T, _ = seqs[0].shape
    tok = pl.BlockSpec((nb, nc * c, RWKV_DIM), lambda b, t: (b, t, 0))
    st = pl.BlockSpec((nb, N_HEADS, HEAD_SIZE, HEAD_SIZE), lambda b, t: (b, 0, 0, 0))
    vec = pl.BlockSpec((1, RWKV_DIM), lambda b, t: (0, 0))
    return pl.pallas_call(
        functools.partial(_wkv_kernel, nb=nb, c=c, nc=nc),
        grid=(B // nb, T // (nc * c)),
        in_specs=[tok] * 8 + [st, vec, vec],
        out_specs=[tok, st],
        out_shape=[jax.ShapeDtypeStruct((B, T, RWKV_DIM), F32),
                   jax.ShapeDtypeStruct((B, N_HEADS, HEAD_SIZE, HEAD_SIZE), F32)],
        scratch_shapes=[pltpu.VMEM((nb, N_HEADS, HEAD_SIZE, 2 * HEAD_SIZE), F32)],
        compiler_params=pltpu.CompilerParams(
            dimension_semantics=("arbitrary", "arbitrary"), vmem_limit_bytes=VMEM_LIMIT),
        name="wkv",
    )(*seqs, s0, gain, bias)


def _out_stream(rows, x_ref, yc_ref, yr_ref, wout_ref, gpost_ref, gfpre_ref, gfpost_ref,
                wff1_ref, wff2_ref, o_ref):
    wout = wout_ref[...]
    mix = (_dg(yc_ref[rows, :].astype(BF16), wout[:CONV_DIM])
           + _dg(yr_ref[rows, :].astype(BF16), wout[CONV_DIM:]))
    yield
    x1 = x_ref[rows, :] + _rmsnorm(mix, gpost_ref[...])
    h2 = _rmsnorm(x1, gfpre_ref[...]).astype(BF16)
    yield
    f1 = _dg(h2, wff1_ref[...])
    yield
    f1 = jnp.square(jnp.maximum(f1, 0.0)).astype(BF16)
    yield
    f2 = _dg(f1, wff2_ref[...])
    yield
    o_ref[rows, :] = x1 + _rmsnorm(f2, gfpost_ref[...])


def _out_kernel(*refs, tm, ns):
    per = tm // ns
    _interleave([_out_stream(slice(i * per, (i + 1) * per), *refs) for i in range(ns)])


def _out_call(x, yc, yr, params, *, tm, ns):
    n = x.shape[0]
    tok = lambda w: pl.BlockSpec((tm, w), lambda i: (i, 0))
    full = lambda arr: pl.BlockSpec(arr.shape, lambda i: (0,) * arr.ndim,
                                    pipeline_mode=pl.Buffered(1))
    return pl.pallas_call(
        functools.partial(_out_kernel, tm=tm, ns=ns),
        grid=(n // tm,),
        in_specs=[tok(D_MODEL), tok(CONV_DIM), tok(RWKV_DIM)] + [full(p) for p in params],
        out_specs=tok(D_MODEL),
        out_shape=jax.ShapeDtypeStruct((n, D_MODEL), F32),
        compiler_params=pltpu.CompilerParams(
            dimension_semantics=("arbitrary",), vmem_limit_bytes=VMEM_LIMIT),
        name="outffn",
    )(x, yc, yr, *params)


FF_STEPS = 4


def _out_stream_kernel(x_ref, yc_ref, yr_ref, wout_ref, gpost_ref, gfpre_ref, gfpost_ref,
                       wff1_ref, wff2_ref, o_ref, x1_scr, h2_scr, acc_scr):
    j = pl.program_id(0)

    @pl.when(j == 0)
    def _():
        wout = wout_ref[...]
        mix = (_dg(yc_ref[...].astype(BF16), wout[:CONV_DIM])
               + _dg(yr_ref[...].astype(BF16), wout[CONV_DIM:]))
        x1 = x_ref[...] + _rmsnorm(mix, gpost_ref[...])
        x1_scr[...] = x1
        h2_scr[...] = _rmsnorm(x1, gfpre_ref[...]).astype(BF16)
        acc_scr[...] = jnp.zeros_like(acc_scr)

    f1 = jnp.square(jnp.maximum(_dg(h2_scr[...], wff1_ref[...]), 0.0)).astype(BF16)
    acc_scr[...] += _dg(f1, wff2_ref[...])

    @pl.when(j == pl.num_programs(0) - 1)
    def _():
        o_ref[...] = x1_scr[...] + _rmsnorm(acc_scr[...], gfpost_ref[...])


def _out_stream_call(x, yc, yr, params):
    n = x.shape[0]
    wout, gpost, gfpre, gfpost, wff1, wff2 = params
    d_ff = wff1.shape[1]
    fb = d_ff // FF_STEPS
    full = lambda arr: pl.BlockSpec(arr.shape, lambda j: (0,) * arr.ndim)
    return pl.pallas_call(
        _out_stream_kernel,
        grid=(FF_STEPS,),
        in_specs=[full(x), full(yc), full(yr), full(wout), full(gpost), full(gfpre), full(gfpost),
                  pl.BlockSpec((D_MODEL, fb), lambda j: (0, j)),
                  pl.BlockSpec((fb, D_MODEL), lambda j: (j, 0))],
        out_specs=full(x),
        out_shape=jax.ShapeDtypeStruct((n, D_MODEL), F32),
        scratch_shapes=[pltpu.VMEM((n, D_MODEL), F32), pltpu.VMEM((n, D_MODEL), BF16),
                        pltpu.VMEM((n, D_MODEL), F32)],
        compiler_params=pltpu.CompilerParams(
            dimension_semantics=("arbitrary",), vmem_limit_bytes=VMEM_LIMIT),
        name="outffn_stream",
    )(x, yc, yr, *params)


def _tiles(B, T):
    if T >= 256:
        return 2, 256, 4, 2, 64, 4, 512, 2
    return B, T, 1, B, T, 1, B * T, 1


def _layer(x, conv_buf, shift_prev, wkv_state, proj_params, gn, norms, out_w):
    B, T, _ = x.shape
    pnb, ptt, pns, wnb, wc, wnc, tm, ons = _tiles(B, T)
    cast = [w for w in out_w if w.dtype != BF16]
    outs = _proj_call(x, conv_buf, shift_prev, proj_params, cast,
                      nb=pnb, tt=ptt, ns=pns)
    yconv, seqs, new_conv, new_shift = outs[0], outs[1:9], outs[9], outs[10]
    if cast:
        out_w = tuple(outs[11:])
    yr, new_wkv = _wkv_call(seqs, wkv_state, *gn, nb=wnb, c=wc, nc=wnc)
    out_params = (out_w[0], *norms, out_w[1], out_w[2])
    flat = (x.reshape(B * T, D_MODEL), yconv.reshape(B * T, CONV_DIM), yr.reshape(B * T, RWKV_DIM))
    if tm == B * T:
        y = _out_stream_call(*flat, out_params)
    else:
        y = _out_call(*flat, out_params, tm=tm, ns=ons)
    return (y.reshape(B, T, D_MODEL), new_conv, new_shift, new_wkv,
            out_w)


def kernel(x_prompt, x_sample, state_conv, state_shift, state_wkv, norm_mix_pre, norm_mix_post,
           norm_ffn_pre, norm_ffn_post, w_in, conv_w, shift_mu, w_decay2, decay_w0, w_a2, a0,
           w_g2, k_k, k_a, r_k, gn_gain, gn_bias, w_out, w_ff1, w_ff2):
    depth = w_in.shape[0]
    Bp = x_prompt.shape[0]
    xp, xs = x_prompt, x_sample
    row = lambda t: t.reshape(1, -1).astype(F32)
    res = [[] for _ in range(6)]
    for l in range(depth):
        proj_params = (
            row(norm_mix_pre[l]), w_in[l].astype(BF16), conv_w[l], row(shift_mu[l]),
            w_decay2[l], row(decay_w0[l]), w_a2[l], row(a0[l]), w_g2[l],
            row(k_k[l]), row(k_a[l]), row(r_k[l]))
        gn = (row(gn_gain[l]), row(gn_bias[l]))
        norms = (row(norm_mix_post[l]), row(norm_ffn_pre[l]), row(norm_ffn_post[l]))
        xp, c_p, s_p, w_p, out_w = _layer(
            xp, jnp.zeros((Bp, CONV_K - 1, CONV_DIM), F32), jnp.zeros((Bp, RWKV_PROJ), F32),
            jnp.zeros((Bp, N_HEADS, HEAD_SIZE, HEAD_SIZE), F32), proj_params, gn, norms,
            (w_out[l], w_ff1[l], w_ff2[l]))
        xs, c_s, s_s, w_s, _ = _layer(xs, state_conv[l], state_shift[l], state_wkv[l],
                                      proj_params, gn, norms, out_w)
        for lst, val in zip(res, (c_p, s_p, w_p, c_s, s_s, w_s)):
            lst.append(val)
    return (xp, xs) + tuple(jnp.stack(r) for r in res)
```

```python
import functools

import jax
import jax.numpy as jnp
import numpy as np
from jax import lax
from jax.experimental import pallas as pl
from jax.experimental.pallas import tpu as pltpu

D_MODEL = 1024
CONV_DIM = 512
RWKV_DIM = 512
HEAD_SIZE = 64
N_HEADS = 8
CONV_K = 3
DECAY_LORA = 64
AAA_LORA = 64
GATE_LORA = 128
RWKV_PROJ = 3 * RWKV_DIM + DECAY_LORA + AAA_LORA + GATE_LORA
PROJ_DIM = 3 * CONV_DIM + RWKV_PROJ
NORM_EPS = 1e-6
GN_EPS = 64e-5
DECAY_SCALE = float(np.exp(-0.5))

SUBLANES = 8
LANES = 128
INV_BASE = 8
VMEM_LIMIT = 56 * 1024 * 1024

F32 = jnp.float32
BF16 = jnp.bfloat16

_NN = (((1,), (0,)), ((), ()))
_NT = (((1,), (1,)), ((), ()))


def _dg(a, b, dims=_NN):
    return lax.dot_general(a, b, dims, preferred_element_type=F32)


def _dot1(a, b, dims=_NN):
    return _dg(a.astype(BF16), b.astype(BF16), dims)


def _bf16_terms(a, n):
    terms = []
    for _ in range(n - 1):
        t = a.astype(BF16)
        terms.append(t)
        a = a - t.astype(F32)
    terms.append(a.astype(BF16))
    return terms


def _sigmoid(x):
    return 0.5 * jnp.tanh(0.5 * x) + 0.5


def _interleave(streams):
    live = list(streams)
    while live:
        live = [s for s in live if next(s, True) is None]


def _head_sums(x):
    pair = LANES
    low = lax.broadcasted_iota(jnp.int32, (x.shape[0], pair), 1) < HEAD_SIZE
    tiles = []
    for p in range(RWKV_DIM // pair):
        t = x[:, p * pair:(p + 1) * pair]
        s_lo = jnp.sum(jnp.where(low, t, 0.0), axis=-1, keepdims=True)
        s_hi = jnp.sum(jnp.where(low, 0.0, t), axis=-1, keepdims=True)
        tiles.append(jnp.where(low, s_lo, s_hi))
    return jnp.concatenate(tiles, axis=1)


def _rmsnorm(x, g):
    return x * lax.rsqrt(jnp.mean(x * x, axis=-1, keepdims=True) + NORM_EPS) * g


def _proj_stream(sq, rq, tt, x_ref, gpre_ref, win_ref, convw_ref, mu_ref, wd_ref, w0_ref, wa_ref,
                 a0_ref, wg_ref, kk_ref, ka_ref, rk_ref, yconv_ref, r_ref, k_ref,
                 v_ref, kkn_ref, b_ref, lw_ref, g_ref, bg_ref, nconv_ref, nshift_ref, ubuf, zbuf):
    nbs = sq.stop - sq.start
    rn = rq.stop - rq.start
    m = nbs * rn
    lo = SUBLANES + rq.start
    hi = lo + rn

    x = x_ref[sq, rq].reshape(m, D_MODEL)
    h = _rmsnorm(x, gpre_ref[...]).astype(BF16)
    z = _dg(h, win_ref[...])
    zb = z[:, :CONV_DIM]
    zc = z[:, CONV_DIM:2 * CONV_DIM]
    zh = z[:, 2 * CONV_DIM:3 * CONV_DIM]
    zr = z[:, 3 * CONV_DIM:]
    yield

    ubuf[sq, lo:hi, :] = (zc * zh).reshape(nbs, rn, CONV_DIM)
    cw = convw_ref[...]
    yc = (ubuf[sq, lo - 2:hi - 2, :] * cw[0:1, :]
          + ubuf[sq, lo - 1:hi - 1, :] * cw[1:2, :]
          + ubuf[sq, lo:hi, :] * cw[2:3, :])
    yconv_ref[sq, rq] = (zb.reshape(nbs, rn, CONV_DIM) * yc).astype(BF16)

    zbuf[sq, lo:hi, :] = zr.reshape(nbs, rn, RWKV_PROJ)
    zprev = zbuf[sq, lo - 1:hi - 1, :].reshape(m, RWKV_PROJ)
    zs = zr + mu_ref[...] * (zprev - zr)

    if rq.stop == tt:
        last_u = ubuf[sq, hi - 2:hi, :]
        nconv_ref[sq] = last_u
        ubuf[sq, SUBLANES - 2:SUBLANES, :] = last_u
        for i in range(sq.start, sq.stop):
            last_z = zbuf[i, hi - 1:hi, :]
            nshift_ref[i:i + 1, :] = last_z
            zbuf[i, SUBLANES - 1:SUBLANES, :] = last_z

    r = zs[:, :RWKV_DIM]
    k = zs[:, RWKV_DIM:2 * RWKV_DIM]
    v = zs[:, 2 * RWKV_DIM:3 * RWKV_DIM]
    lwa = zs[:, 3 * RWKV_DIM:3 * RWKV_DIM + DECAY_LORA + AAA_LORA]
    lg = zs[:, 3 * RWKV_DIM + DECAY_LORA + AAA_LORA:]
    yield

    logit_w = w0_ref[...] + _dot1(jnp.tanh(lwa[:, :DECAY_LORA]), wd_ref[...])
    logw = -DECAY_SCALE * _sigmoid(logit_w)
    a = _sigmoid(a0_ref[...] + _dot1(lwa[:, DECAY_LORA:], wa_ref[...]))
    g = _dot1(_sigmoid(lg), wg_ref[...])
    yield

    kk = k * kk_ref[...]
    ss = _head_sums(kk * kk)
    kp = k * (1.0 + (a - 1.0) * ka_ref[...])
    bsum = _head_sums(r * kp * rk_ref[...])
    yield

    kk = kk * lax.rsqrt(jnp.maximum(ss, 1e-24))
    shp = (nbs, rn, RWKV_DIM)
    r_ref[sq, rq] = r.reshape(shp)
    k_ref[sq, rq] = kp.reshape(shp)
    v_ref[sq, rq] = v.reshape(shp)
    kkn_ref[sq, rq] = kk.reshape(shp)
    b_ref[sq, rq] = (kk * a).reshape(shp)
    lw_ref[sq, rq] = logw.reshape(shp)
    g_ref[sq, rq] = g.reshape(shp)
    bg_ref[sq, rq] = (bsum * v * g).reshape(shp)


def _proj_kernel(x_ref, cbuf_ref, sprev_ref, *rest, nb, tt, ns, ncast):
    ubuf, zbuf = rest[-2:]
    params, cast_in = rest[:12], rest[12:12 + ncast]
    outs, cast_out = rest[12 + ncast:23 + ncast], rest[23 + ncast:23 + 2 * ncast]

    @pl.when(pl.program_id(1) == 0)
    def _():
        ubuf[:, SUBLANES - 2:SUBLANES, :] = cbuf_ref[...]
        for i in range(nb):
            zbuf[i, SUBLANES - 1:SUBLANES, :] = sprev_ref[i:i + 1, :]

    for src, dst in zip(cast_in, cast_out):
        dst[...] = src[...].astype(BF16)

    nseq = min(ns, nb)
    per, rn = nb // nseq, tt // (ns // nseq)
    _interleave([_proj_stream(slice(i * per, (i + 1) * per), slice(r0, r0 + rn), tt, x_ref,
                              *params, *outs, ubuf, zbuf)
                 for i in range(nseq) for r0 in range(0, tt, rn)])


def _proj_call(x, conv_buf, shift_prev, params, cast_srcs, *, nb, tt, ns):
    B, T, _ = x.shape
    nt = T // tt
    grid = (B // nb, nt)
    steps = grid[0] * nt
    tok = lambda w: pl.BlockSpec((nb, tt, w), lambda b, t: (b, t, 0))
    full = lambda arr: pl.BlockSpec(arr.shape, lambda b, t: (0,) * arr.ndim)
    rows = lambda arr: pl.BlockSpec((arr.shape[0] // steps, arr.shape[1]),
                                    lambda b, t: (b * nt + t, 0))
    in_specs = [tok(D_MODEL),
                pl.BlockSpec((nb, CONV_K - 1, CONV_DIM), lambda b, t: (b, 0, 0)),
                pl.BlockSpec((nb, RWKV_PROJ), lambda b, t: (b, 0))]
    in_specs += [full(p) for p in params] + [rows(w) for w in cast_srcs]
    seq = lambda w: jax.ShapeDtypeStruct((B, T, w), F32)
    out_shape = [jax.ShapeDtypeStruct((B, T, CONV_DIM), BF16)] + [seq(RWKV_DIM)] * 8 + [
        jax.ShapeDtypeStruct((B, CONV_K - 1, CONV_DIM), F32),
        jax.ShapeDtypeStruct((B, RWKV_PROJ), F32)]
    out_shape += [jax.ShapeDtypeStruct(w.shape, BF16) for w in cast_srcs]
    out_specs = [tok(CONV_DIM)] + [tok(RWKV_DIM)] * 8 + [
        pl.BlockSpec((nb, CONV_K - 1, CONV_DIM), lambda b, t: (b, 0, 0)),
        pl.BlockSpec((nb, RWKV_PROJ), lambda b, t: (b, 0))]
    out_specs += [rows(w) for w in cast_srcs]
    return pl.pallas_call(
        functools.partial(_proj_kernel, nb=nb, tt=tt, ns=ns, ncast=len(cast_srcs)),
        grid=grid, in_specs=in_specs, out_specs=out_specs, out_shape=out_shape,
        scratch_shapes=[pltpu.VMEM((nb, tt + SUBLANES, CONV_DIM), F32),
                        pltpu.VMEM((nb, tt + SUBLANES, RWKV_PROJ), F32)],
        compiler_params=pltpu.CompilerParams(
            dimension_semantics=("arbitrary", "arbitrary"), vmem_limit_bytes=VMEM_LIMIT),
        name="proj",
    )(x, conv_buf, shift_prev, *params, *cast_srcs)


def _unit_lower_inverse(lows, c):
    base = min(INV_BASE, c)
    nblk = c // base
    prow = lax.broadcasted_iota(jnp.int32, (base, c), 0)
    pcol = lax.broadcasted_iota(jnp.int32, (base, c), 1)
    lane_blk = pcol // base
    eye_p = (pcol - lane_blk * base == prow).astype(F32)

    def pack(m):
        return sum(jnp.where(lane_blk == i, m[i * base:(i + 1) * base, :], 0.0)
                   for i in range(nblk))

    def expand(p):
        return jnp.concatenate([jnp.where(lane_blk == i, p, 0.0) for i in range(nblk)], axis=0)

    ps = [pack(low) for low in lows]
    invs = [eye_p - p for p in ps]
    ps = [_dot1(p, expand(p)) for p in ps]
    span = 4
    while span < base:
        both = [_dot1(jnp.concatenate([inv, p], axis=0), expand(p)) for inv, p in zip(invs, ps)]
        invs = [inv + b[:base] for inv, b in zip(invs, both)]
        ps = [b[base:] for b in both]
        span *= 2
    invs = [inv + _dot1(inv, expand(p)) for inv, p in zip(invs, ps)]
    invs = [expand(inv) for inv in invs]

    row = lax.broadcasted_iota(jnp.int32, (c, c), 0)
    col = lax.broadcasted_iota(jnp.int32, (c, c), 1)
    blk = base
    while blk < c:
        sel = ((row // (2 * blk)) == (col // (2 * blk))) & ((row // blk) != (col // blk))
        odd = [slice((2 * b + 1) * blk, (2 * b + 2) * blk) for b in range(c // (2 * blk))]
        zero = jnp.zeros((blk, c), F32)

        def take(m):
            return jnp.concatenate([m[s] for s in odd], axis=0)

        def place(mr):
            parts = []
            for b in range(len(odd)):
                parts += [zero, mr[b * blk:(b + 1) * blk]]
            return jnp.concatenate(parts, axis=0)

        tmp = [place(_dot1(take(jnp.where(sel, low, 0.0)), inv)) for low, inv in zip(lows, invs)]
        invs = [inv - place(_dot1(take(inv), t)) for inv, t in zip(invs, tmp)]
        blk *= 2
    return invs


def _wkv_kernel(r_ref, k_ref, v_ref, kk_ref, b_ref, lw_ref, g_ref, bg_ref, s0_ref,
                gain_ref, bias_ref, y_ref, sout_ref, s_scr, *, nb, c, nc):
    assert 2 * HEAD_SIZE == LANES
    pair = LANES
    zpad = jnp.zeros((HEAD_SIZE, HEAD_SIZE), F32)

    @pl.when(pl.program_id(1) == 0)
    def _():
        for i in range(nb):
            for hd in range(N_HEADS):
                s0 = s0_ref[i, hd]
                s_scr[i, hd] = jnp.concatenate([s0, zpad] if hd % 2 == 0 else [zpad, s0], axis=1)

    row = lax.broadcasted_iota(jnp.int32, (c, c), 0)
    col = lax.broadcasted_iota(jnp.int32, (c, c), 1)
    strict = row > col
    incl = row >= col
    row2 = lax.broadcasted_iota(jnp.int32, (c, 2 * c), 0)
    col2 = lax.broadcasted_iota(jnp.int32, (c, 2 * c), 1)
    incl2 = row2 >= jnp.where(col2 >= c, col2 - c, col2)
    tri = incl.astype(BF16)
    lane_half = lax.broadcasted_iota(jnp.int32, (c, pair), 1) // HEAD_SIZE
    own = [lane_half == 0, lane_half == 1]
    zero_b = jnp.zeros((c, pair), BF16)

    def precompute(chunks):
        xl, xr, xe, vh, pe = [], [], [], [], []
        for q in chunks:
            rows = slice(q * c, (q + 1) * c)
            for i in range(nb):
                logw = lw_ref[i, rows, :]
                cum = sum(_dg(tri, t) for t in _bf16_terms(logw, 2))
                cum_end = cum[c - 1:c, :]
                k = k_ref[i, rows, :]
                b = b_ref[i, rows, :]
                e_neg = jnp.exp(-cum)
                rt = (r_ref[i, rows, :] * jnp.exp(cum)).astype(BF16)
                at = (kk_ref[i, rows, :] * jnp.exp(cum - logw)).astype(BF16)
                bt = (b * e_neg).astype(BF16)
                kt = (k * e_neg).astype(BF16)
                d_end = jnp.exp(cum_end - cum)
                bh = (b * d_end).astype(BF16)
                kh = (k * d_end).astype(BF16)
                p_end = jnp.exp(cum_end)
                v = v_ref[i, rows, :]
                for hd in range(N_HEADS):
                    lp = slice((hd // 2) * pair, (hd // 2 + 1) * pair)
                    m = own[hd % 2]
                    if hd % 2 == 0:
                        xr_pair = jnp.concatenate([bt[:, lp], kt[:, lp]], axis=0)
                    xr.append(xr_pair)
                    xl.append(jnp.concatenate([jnp.where(m, at[:, lp], zero_b),
                                               jnp.where(m, rt[:, lp], zero_b)], axis=0))
                    xe.append(jnp.concatenate([jnp.where(m, bh[:, lp], zero_b),
                                               jnp.where(m, kh[:, lp], zero_b)], axis=0))
                    vh.append(v[:, hd * HEAD_SIZE:(hd + 1) * HEAD_SIZE])
                    pe.append(p_end[:, lp])
        n = len(xl)

        gram = [_dg(xl[j], xr[j], _NT) for j in range(n)]
        tinv = _unit_lower_inverse([jnp.where(strict, g[:c, :c], 0.0) for g in gram], c)
        yk = [_dot1(jnp.where(strict, gram[j][:c, c:], 0.0), vh[j]) for j in range(n)]
        wu = [_dg(tinv[j].astype(BF16),
                  jnp.concatenate([xl[j][:c], (-yk[j]).astype(BF16)], axis=1)) for j in range(n)]
        w_b = [wu[j][:, :pair].astype(BF16) for j in range(n)]
        r_b = [xl[j][c:] for j in range(n)]
        u0t = [wu[j][:, pair:].T for j in range(n)]
        vt = [x.T.astype(BF16) for x in vh]
        m_r = [jnp.where(incl2, g[c:, :], 0.0).astype(BF16) for g in gram]
        return w_b, r_b, u0t, vt, m_r, xe, pe

    w_b, r_b, u0t, vt, m_r, xe, pe = precompute(range(nc))

    nh = nb * N_HEADS
    st = [s_scr[jj // N_HEADS, jj % N_HEADS] for jj in range(nh)]
    gain = gain_ref[...]
    bias = bias_ref[...]
    for q in range(nc):
        rows = slice(q * c, (q + 1) * c)
        js = [q * nh + jj for jj in range(nh)]
        sb = [s.astype(BF16) for s in st]
        ut = [u0t[j] - _dg(sb[jj], w_b[j], _NT) for jj, j in enumerate(js)]
        rst = [_dg(sb[jj], r_b[j], _NT) for jj, j in enumerate(js)]
        uvt = [jnp.concatenate([ut[jj].astype(BF16), vt[j]], axis=1)
               for jj, j in enumerate(js)]
        st = [st[jj] * pe[j] + _dg(uvt[jj], xe[j]) for jj, j in enumerate(js)]
        ot = [rst[jj] + _dg(uvt[jj], m_r[j], _NT) for jj, j in enumerate(js)]
        mu = [jnp.mean(x, axis=0, keepdims=True) for x in ot]
        dev = [x - m for x, m in zip(ot, mu)]
        var = [jnp.mean(jnp.square(d), axis=0, keepdims=True) for d in dev]
        on = [(d * lax.rsqrt(s + GN_EPS)).T for d, s in zip(dev, var)]
        for jj in range(0, nh, 2):
            i, hd = jj // N_HEADS, jj % N_HEADS
            lp = slice(hd * HEAD_SIZE, (hd + 2) * HEAD_SIZE)
            on_pair = jnp.concatenate([on[jj], on[jj + 1]], axis=1)
            y_ref[i, rows, lp] = ((on_pair * gain[:, lp] + bias[:, lp]) * g_ref[i, rows, lp]
                                  + bg_ref[i, rows, lp]).astype(BF16)

    for jj in range(nh):
        s_scr[jj // N_HEADS, jj % N_HEADS] = st[jj]

    @pl.when(pl.program_id(1) == pl.num_programs(1) - 1)
    def _():
        for jj in range(nh):
            hd = jj % N_HEADS
            sout_ref[jj // N_HEADS, hd] = st[jj][:, (hd % 2) * HEAD_SIZE:(hd % 2 + 1) * HEAD_SIZE]


def _wkv_call(seqs, s0, gain, bias, *, nb, c, nc):
    B, T, _ = seqs[0].shape
    tok = pl.BlockSpec((nb, nc * c, RWKV_DIM), lambda b, t: (b, t, 0))
    st = pl.BlockSpec((nb, N_HEADS, HEAD_SIZE, HEAD_SIZE), lambda b, t: (b, 0, 0, 0))
    vec = pl.BlockSpec((1, RWKV_DIM), lambda b, t: (0, 0))
    return pl.pallas_call(
        functools.partial(_wkv_kernel, nb=nb, c=c, nc=nc),
        grid=(B // nb, T // (nc * c)),
        in_specs=[tok] * 8 + [st, vec, vec],
        out_specs=[tok, st],
        out_shape=[jax.ShapeDtypeStruct((B, T, RWKV_DIM), BF16),
                   jax.ShapeDtypeStruct((B, N_HEADS, HEAD_SIZE, HEAD_SIZE), F32)],
        scratch_shapes=[pltpu.VMEM((nb, N_HEADS, HEAD_SIZE, 2 * HEAD_SIZE), F32)],
        compiler_params=pltpu.CompilerParams(
            dimension_semantics=("arbitrary", "arbitrary"), vmem_limit_bytes=VMEM_LIMIT),
        name="wkv",
    )(*seqs, s0, gain, bias)


def _out_stream(rows, x_ref, yc_ref, yr_ref, wout_ref, gpost_ref, gfpre_ref, gfpost_ref,
                wff1_ref, wff2_ref, o_ref):
    wout = wout_ref[...]
    mix = (_dg(yc_ref[rows, :].astype(BF16), wout[:CONV_DIM])
           + _dg(yr_ref[rows, :].astype(BF16), wout[CONV_DIM:]))
    yield
    x1 = x_ref[rows, :] + _rmsnorm(mix, gpost_ref[...])
    h2 = _rmsnorm(x1, gfpre_ref[...]).astype(BF16)
    yield
    f1 = _dg(h2, wff1_ref[...])
    yield
    f1 = jnp.square(jnp.maximum(f1, 0.0)).astype(BF16)
    yield
    f2 = _dg(f1, wff2_ref[...])
    yield
    o_ref[rows, :] = x1 + _rmsnorm(f2, gfpost_ref[...])


def _out_kernel(*refs, tm, ns):
    per = tm // ns
    _interleave([_out_stream(slice(i * per, (i + 1) * per), *refs) for i in range(ns)])


def _out_call(x, yc, yr, params, *, tm, ns):
    n = x.shape[0]
    tok = lambda w: pl.BlockSpec((tm, w), lambda i: (i, 0))
    full = lambda arr: pl.BlockSpec(arr.shape, lambda i: (0,) * arr.ndim,
                                    pipeline_mode=pl.Buffered(1))
    return pl.pallas_call(
        functools.partial(_out_kernel, tm=tm, ns=ns),
        grid=(n // tm,),
        in_specs=[tok(D_MODEL), tok(CONV_DIM), tok(RWKV_DIM)] + [full(p) for p in params],
        out_specs=tok(D_MODEL),
        out_shape=jax.ShapeDtypeStruct((n, D_MODEL), F32),
        compiler_params=pltpu.CompilerParams(
            dimension_semantics=("arbitrary",), vmem_limit_bytes=VMEM_LIMIT),
        name="outffn",
    )(x, yc, yr, *params)


def _tiles(B, T):
    if T >= 256:
        return 2, 256, 4, 2, 64, 4, 512, 2
    return B, T, 1, B, T, 1, B * T, 1


def _layer(x, conv_buf, shift_prev, wkv_state, proj_params, gn, norms, out_w):
    B, T, _ = x.shape
    pnb, ptt, pns, wnb, wc, wnc, tm, ons = _tiles(B, T)
    cast = [w for w in out_w if w.dtype != BF16]
    outs = _proj_call(x, conv_buf, shift_prev, proj_params, cast,
                      nb=pnb, tt=ptt, ns=pns)
    yconv, seqs, new_conv, new_shift = outs[0], outs[1:9], outs[9], outs[10]
    if cast:
        out_w = tuple(outs[11:])
    yr, new_wkv = _wkv_call(seqs, wkv_state, *gn, nb=wnb, c=wc, nc=wnc)
    out_params = (out_w[0], *norms, out_w[1], out_w[2])
    y = _out_call(x.reshape(B * T, D_MODEL), yconv.reshape(B * T, CONV_DIM),
                  yr.reshape(B * T, RWKV_DIM), out_params, tm=tm, ns=ons)
    return (y.reshape(B, T, D_MODEL), new_conv, new_shift, new_wkv,
            out_w)


def kernel(x_prompt, x_sample, state_conv, state_shift, state_wkv, norm_mix_pre, norm_mix_post,
           norm_ffn_pre, norm_ffn_post, w_in, conv_w, shift_mu, w_decay2, decay_w0, w_a2, a0,
           w_g2, k_k, k_a, r_k, gn_gain, gn_bias, w_out, w_ff1, w_ff2):
    depth = w_in.shape[0]
    Bp = x_prompt.shape[0]
    xp, xs = x_prompt, x_sample
    row = lambda t: t.reshape(1, -1).astype(F32)
    res = [[] for _ in range(6)]
    for l in range(depth):
        proj_params = (
            row(norm_mix_pre[l]), w_in[l].astype(BF16), conv_w[l], row(shift_mu[l]),
            w_decay2[l], row(decay_w0[l]), w_a2[l], row(a0[l]), w_g2[l],
            row(k_k[l]), row(k_a[l]), row(r_k[l]))
        gn = (row(gn_gain[l]), row(gn_bias[l]))
        norms = (row(norm_mix_post[l]), row(norm_ffn_pre[l]), row(norm_ffn_post[l]))
        xp, c_p, s_p, w_p, out_w = _layer(
            xp, jnp.zeros((Bp, CONV_K - 1, CONV_DIM), F32), jnp.zeros((Bp, RWKV_PROJ), F32),
            jnp.zeros((Bp, N_HEADS, HEAD_SIZE, HEAD_SIZE), F32), proj_params, gn, norms,
            (w_out[l], w_ff1[l], w_ff2[l]))
        xs, c_s, s_s, w_s, _ = _layer(xs, state_conv[l], state_shift[l], state_wkv[l],
                                      proj_params, gn, norms, out_w)
        for lst, val in zip(res, (c_p, s_p, w_p, c_s, s_s, w_s)):
            lst.append(val)
    return (xp, xs) + tuple(jnp.stack(r) for r in res)
```

```python
import functools

import jax
import jax.numpy as jnp
import numpy as np
from jax import lax
from jax.experimental import pallas as pl
from jax.experimental.pallas import tpu as pltpu

D_MODEL = 1024
CONV_DIM = 512
RWKV_DIM = 512
HEAD_SIZE = 64
N_HEADS = 8
CONV_K = 3
DECAY_LORA = 64
AAA_LORA = 64
GATE_LORA = 128
RWKV_PROJ = 3 * RWKV_DIM + DECAY_LORA + AAA_LORA + GATE_LORA
PROJ_DIM = 3 * CONV_DIM + RWKV_PROJ
NORM_EPS = 1e-6
GN_EPS = 64e-5
DECAY_SCALE = float(np.exp(-0.5))

SUBLANES = 8
LANES = 128
INV_BASE = 8
VMEM_LIMIT = 56 * 1024 * 1024

F32 = jnp.float32
BF16 = jnp.bfloat16

_NN = (((1,), (0,)), ((), ()))
_NT = (((1,), (1,)), ((), ()))


def _dg(a, b, dims=_NN):
    return lax.dot_general(a, b, dims, preferred_element_type=F32)


def _dot1(a, b, dims=_NN):
    return _dg(a.astype(BF16), b.astype(BF16), dims)


def _bf16_terms(a, n):
    terms = []
    for _ in range(n - 1):
        t = a.astype(BF16)
        terms.append(t)
        a = a - t.astype(F32)
    terms.append(a.astype(BF16))
    return terms


def _sigmoid(x):
    return 0.5 * jnp.tanh(0.5 * x) + 0.5


def _interleave(streams):
    live = list(streams)
    while live:
        live = [s for s in live if next(s, True) is None]


def _head_sums(x):
    pair = LANES
    low = lax.broadcasted_iota(jnp.int32, (x.shape[0], pair), 1) < HEAD_SIZE
    tiles = []
    for p in range(RWKV_DIM // pair):
        t = x[:, p * pair:(p + 1) * pair]
        s_lo = jnp.sum(jnp.where(low, t, 0.0), axis=-1, keepdims=True)
        s_hi = jnp.sum(jnp.where(low, 0.0, t), axis=-1, keepdims=True)
        tiles.append(jnp.where(low, s_lo, s_hi))
    return jnp.concatenate(tiles, axis=1)


def _rmsnorm(x, g):
    return x * lax.rsqrt(jnp.mean(x * x, axis=-1, keepdims=True) + NORM_EPS) * g


def _proj_stream(sq, rq, tt, x_ref, gpre_ref, win_ref, convw_ref, mu_ref, wd_ref, w0_ref, wa_ref,
                 a0_ref, wg_ref, kk_ref, ka_ref, rk_ref, yconv_ref, r_ref, k_ref,
                 v_ref, kkn_ref, b_ref, lw_ref, g_ref, bg_ref, nconv_ref, nshift_ref, ubuf, zbuf):
    nbs = sq.stop - sq.start
    rn = rq.stop - rq.start
    m = nbs * rn
    lo = SUBLANES + rq.start
    hi = lo + rn

    x = x_ref[sq, rq].reshape(m, D_MODEL)
    h = _rmsnorm(x, gpre_ref[...]).astype(BF16)
    z = _dg(h, win_ref[...])
    zb = z[:, :CONV_DIM]
    zc = z[:, CONV_DIM:2 * CONV_DIM]
    zh = z[:, 2 * CONV_DIM:3 * CONV_DIM]
    zr = z[:, 3 * CONV_DIM:]
    yield

    ubuf[sq, lo:hi, :] = (zc * zh).reshape(nbs, rn, CONV_DIM)
    cw = convw_ref[...]
    yc = (ubuf[sq, lo - 2:hi - 2, :] * cw[0:1, :]
          + ubuf[sq, lo - 1:hi - 1, :] * cw[1:2, :]
          + ubuf[sq, lo:hi, :] * cw[2:3, :])
    yconv_ref[sq, rq] = (zb.reshape(nbs, rn, CONV_DIM) * yc).astype(BF16)

    zbuf[sq, lo:hi, :] = zr.reshape(nbs, rn, RWKV_PROJ)
    zprev = zbuf[sq, lo - 1:hi - 1, :].reshape(m, RWKV_PROJ)
    zs = zr + mu_ref[...] * (zprev - zr)

    if rq.stop == tt:
        last_u = ubuf[sq, hi - 2:hi, :]
        nconv_ref[sq] = last_u
        ubuf[sq, SUBLANES - 2:SUBLANES, :] = last_u
        for i in range(sq.start, sq.stop):
            last_z = zbuf[i, hi - 1:hi, :]
            nshift_ref[i:i + 1, :] = last_z
            zbuf[i, SUBLANES - 1:SUBLANES, :] = last_z

    r = zs[:, :RWKV_DIM]
    k = zs[:, RWKV_DIM:2 * RWKV_DIM]
    v = zs[:, 2 * RWKV_DIM:3 * RWKV_DIM]
    lwa = zs[:, 3 * RWKV_DIM:3 * RWKV_DIM + DECAY_LORA + AAA_LORA]
    lg = zs[:, 3 * RWKV_DIM + DECAY_LORA + AAA_LORA:]
    yield

    logit_w = w0_ref[...] + _dot1(jnp.tanh(lwa[:, :DECAY_LORA]), wd_ref[...])
    logw = -DECAY_SCALE * _sigmoid(logit_w)
    a = _sigmoid(a0_ref[...] + _dot1(lwa[:, DECAY_LORA:], wa_ref[...]))
    g = _dot1(_sigmoid(lg), wg_ref[...])
    yield

    kk = k * kk_ref[...]
    ss = _head_sums(kk * kk)
    kp = k * (1.0 + (a - 1.0) * ka_ref[...])
    bsum = _head_sums(r * kp * rk_ref[...])
    yield

    kk = kk * lax.rsqrt(jnp.maximum(ss, 1e-24))
    shp = (nbs, rn, RWKV_DIM)
    r_ref[sq, rq] = r.reshape(shp)
    k_ref[sq, rq] = kp.reshape(shp)
    v_ref[sq, rq] = v.reshape(shp).astype(BF16)
    kkn_ref[sq, rq] = kk.reshape(shp)
    b_ref[sq, rq] = (kk * a).reshape(shp)
    lw_ref[sq, rq] = logw.reshape(shp)
    g_ref[sq, rq] = g.reshape(shp)
    bg_ref[sq, rq] = (bsum * v * g).reshape(shp)


def _proj_kernel(x_ref, cbuf_ref, sprev_ref, *rest, nb, tt, ns, ncast):
    ubuf, zbuf = rest[-2:]
    params, cast_in = rest[:12], rest[12:12 + ncast]
    outs, cast_out = rest[12 + ncast:23 + ncast], rest[23 + ncast:23 + 2 * ncast]

    @pl.when(pl.program_id(1) == 0)
    def _():
        ubuf[:, SUBLANES - 2:SUBLANES, :] = cbuf_ref[...]
        for i in range(nb):
            zbuf[i, SUBLANES - 1:SUBLANES, :] = sprev_ref[i:i + 1, :]

    for src, dst in zip(cast_in, cast_out):
        dst[...] = src[...].astype(BF16)

    nseq = min(ns, nb)
    per, rn = nb // nseq, tt // (ns // nseq)
    _interleave([_proj_stream(slice(i * per, (i + 1) * per), slice(r0, r0 + rn), tt, x_ref,
                              *params, *outs, ubuf, zbuf)
                 for i in range(nseq) for r0 in range(0, tt, rn)])


def _proj_call(x, conv_buf, shift_prev, params, cast_srcs, *, nb, tt, ns):
    B, T, _ = x.shape
    nt = T // tt
    grid = (B // nb, nt)
    steps = grid[0] * nt
    tok = lambda w: pl.BlockSpec((nb, tt, w), lambda b, t: (b, t, 0))
    full = lambda arr: pl.BlockSpec(arr.shape, lambda b, t: (0,) * arr.ndim)
    rows = lambda arr: pl.BlockSpec((arr.shape[0] // steps, arr.shape[1]),
                                    lambda b, t: (b * nt + t, 0))
    in_specs = [tok(D_MODEL),
                pl.BlockSpec((nb, CONV_K - 1, CONV_DIM), lambda b, t: (b, 0, 0)),
                pl.BlockSpec((nb, RWKV_PROJ), lambda b, t: (b, 0))]
    in_specs += [full(p) for p in params] + [rows(w) for w in cast_srcs]
    seq_dt =[F32, F32, BF16, F32, F32, F32, F32, F32]
    out_shape = [jax.ShapeDtypeStruct((B, T, CONV_DIM), BF16)]
    out_shape += [jax.ShapeDtypeStruct((B, T, RWKV_DIM), dt) for dt in seq_dt] + [
        jax.ShapeDtypeStruct((B, CONV_K - 1, CONV_DIM), F32),
        jax.ShapeDtypeStruct((B, RWKV_PROJ), F32)]
    out_shape += [jax.ShapeDtypeStruct(w.shape, BF16) for w in cast_srcs]
    out_specs = [tok(CONV_DIM)] + [tok(RWKV_DIM)] * 8 + [
        pl.BlockSpec((nb, CONV_K - 1, CONV_DIM), lambda b, t: (b, 0, 0)),
        pl.BlockSpec((nb, RWKV_PROJ), lambda b, t: (b, 0))]
    out_specs += [rows(w) for w in cast_srcs]
    return pl.pallas_call(
        functools.partial(_proj_kernel, nb=nb, tt=tt, ns=ns, ncast=len(cast_srcs)),
        grid=grid, in_specs=in_specs, out_specs=out_specs, out_shape=out_shape,
        scratch_shapes=[pltpu.VMEM((nb, tt + SUBLANES, CONV_DIM), F32),
                        pltpu.VMEM((nb, tt + SUBLANES, RWKV_PROJ), F32)],
        compiler_params=pltpu.CompilerParams(
            dimension_semantics=("arbitrary", "arbitrary"), vmem_limit_bytes=VMEM_LIMIT),
        name="proj",
    )(x, conv_buf, shift_prev, *params, *cast_srcs)


def _unit_lower_inverse(lows, c):
    base = min(INV_BASE, c)
    nblk = c // base
    prow = lax.broadcasted_iota(jnp.int32, (base, c), 0)
    pcol = lax.broadcasted_iota(jnp.int32, (base, c), 1)
    lane_blk = pcol // base
    eye_p = (pcol - lane_blk * base == prow).astype(F32)

    def pack(m):
        return sum(jnp.where(lane_blk == i, m[i * base:(i + 1) * base, :], 0.0)
                   for i in range(nblk))

    def expand(p):
        return jnp.concatenate([jnp.where(lane_blk == i, p, 0.0) for i in range(nblk)], axis=0)

    ps = [pack(low) for low in lows]
    invs = [eye_p - p for p in ps]
    ps = [_dot1(p, expand(p)) for p in ps]
    span = 4
    while span < base:
        both = [_dot1(jnp.concatenate([inv, p], axis=0), expand(p)) for inv, p in zip(invs, ps)]
        invs = [inv + b[:base] for inv, b in zip(invs, both)]
        ps = [b[base:] for b in both]
        span *= 2
    invs = [inv + _dot1(inv, expand(p)) for inv, p in zip(invs, ps)]
    invs = [expand(inv) for inv in invs]

    row = lax.broadcasted_iota(jnp.int32, (c, c), 0)
    col = lax.broadcasted_iota(jnp.int32, (c, c), 1)
    blk = base
    while blk < c:
        sel = ((row // (2 * blk)) == (col // (2 * blk))) & ((row // blk) != (col // blk))
        odd = [slice((2 * b + 1) * blk, (2 * b + 2) * blk) for b in range(c // (2 * blk))]
        zero = jnp.zeros((blk, c), F32)

        def take(m):
            return jnp.concatenate([m[s] for s in odd], axis=0)

        def place(mr):
            parts = []
            for b in range(len(odd)):
                parts += [zero, mr[b * blk:(b + 1) * blk]]
            return jnp.concatenate(parts, axis=0)

        tmp = [place(_dot1(take(jnp.where(sel, low, 0.0)), inv)) for low, inv in zip(lows, invs)]
        invs = [inv - place(_dot1(take(inv), t)) for inv, t in zip(invs, tmp)]
        blk *= 2
    return invs


def _wkv_kernel(r_ref, k_ref, v_ref, kk_ref, b_ref, lw_ref, g_ref, bg_ref, s0_ref,
                gain_ref, bias_ref, y_ref, sout_ref, s_scr, *, nb, c, nc):
    assert 2 * HEAD_SIZE == LANES
    pair = LANES
    zpad = jnp.zeros((HEAD_SIZE, HEAD_SIZE), F32)

    @pl.when(pl.program_id(1) == 0)
    def _():
        for i in range(nb):
            for hd in range(N_HEADS):
                s0 = s0_ref[i, hd]
                s_scr[i, hd] = jnp.concatenate([s0, zpad] if hd % 2 == 0 else [zpad, s0], axis=1)

    row = lax.broadcasted_iota(jnp.int32, (c, c), 0)
    col = lax.broadcasted_iota(jnp.int32, (c, c), 1)
    strict = row > col
    incl = row >= col
    row2 = lax.broadcasted_iota(jnp.int32, (c, 2 * c), 0)
    col2 = lax.broadcasted_iota(jnp.int32, (c, 2 * c), 1)
    incl2 = row2 >= jnp.where(col2 >= c, col2 - c, col2)
    tri = incl.astype(BF16)
    lane_half = lax.broadcasted_iota(jnp.int32, (c, pair), 1) // HEAD_SIZE
    own = [lane_half == 0, lane_half == 1]
    zero_b = jnp.zeros((c, pair), BF16)

    def precompute(chunks):
        xl, xr, xe, vh, pe = [], [], [], [], []
        for q in chunks:
            rows = slice(q * c, (q + 1) * c)
            for i in range(nb):
                logw = lw_ref[i, rows, :]
                cum = sum(_dg(tri, t) for t in _bf16_terms(logw, 2))
                cum_end = cum[c - 1:c, :]
                k = k_ref[i, rows, :]
                b = b_ref[i, rows, :]
                e_neg = jnp.exp(-cum)
                rt = (r_ref[i, rows, :] * jnp.exp(cum)).astype(BF16)
                at = (kk_ref[i, rows, :] * jnp.exp(cum - logw)).astype(BF16)
                bt = (b * e_neg).astype(BF16)
                kt = (k * e_neg).astype(BF16)
                d_end = jnp.exp(cum_end - cum)
                bh = (b * d_end).astype(BF16)
                kh = (k * d_end).astype(BF16)
                p_end = jnp.exp(cum_end)
                v = v_ref[i, rows, :].astype(F32)
                for hd in range(N_HEADS):
                    lp = slice((hd // 2) * pair, (hd // 2 + 1) * pair)
                    m = own[hd % 2]
                    if hd % 2 == 0:
                        xr_pair = jnp.concatenate([bt[:, lp], kt[:, lp]], axis=0)
                    xr.append(xr_pair)
                    xl.append(jnp.concatenate([jnp.where(m, at[:, lp], zero_b),
                                               jnp.where(m, rt[:, lp], zero_b)], axis=0))
                    xe.append(jnp.concatenate([jnp.where(m, bh[:, lp], zero_b),
                                               jnp.where(m, kh[:, lp], zero_b)], axis=0))
                    vh.append(v[:, hd * HEAD_SIZE:(hd + 1) * HEAD_SIZE])
                    pe.append(p_end[:, lp])
        n = len(xl)

        gram = [_dg(xl[j], xr[j], _NT) for j in range(n)]
        tinv = _unit_lower_inverse([jnp.where(strict, g[:c, :c], 0.0) for g in gram], c)
        yk = [_dot1(jnp.where(strict, gram[j][:c, c:], 0.0), vh[j]) for j in range(n)]
        wu = [_dg(tinv[j].astype(BF16),
                  jnp.concatenate([xl[j][:c], (-yk[j]).astype(BF16)], axis=1)) for j in range(n)]
        w_b = [wu[j][:, :pair].astype(BF16) for j in range(n)]
        r_b = [xl[j][c:] for j in range(n)]
        u0t = [wu[j][:, pair:].T for j in range(n)]
        vt = [x.T.astype(BF16) for x in vh]
        m_r = [jnp.where(incl2, g[c:, :], 0.0).astype(BF16) for g in gram]
        return w_b, r_b, u0t, vt, m_r, xe, pe

    w_b, r_b, u0t, vt, m_r, xe, pe = precompute(range(nc))

    nh = nb * N_HEADS
    st = [s_scr[jj // N_HEADS, jj % N_HEADS] for jj in range(nh)]
    gain = gain_ref[...]
    bias = bias_ref[...]
    for q in range(nc):
        rows = slice(q * c, (q + 1) * c)
        js = [q * nh + jj for jj in range(nh)]
        sb = [s.astype(BF16) for s in st]
        ut = [u0t[j] - _dg(sb[jj], w_b[j], _NT) for jj, j in enumerate(js)]
        rst = [_dg(sb[jj], r_b[j], _NT) for jj, j in enumerate(js)]
        uvt = [jnp.concatenate([ut[jj].astype(BF16), vt[j]], axis=1)
               for jj, j in enumerate(js)]
        st = [st[jj] * pe[j] + _dg(uvt[jj], xe[j]) for jj, j in enumerate(js)]
        ot = [rst[jj] + _dg(uvt[jj], m_r[j], _NT) for jj, j in enumerate(js)]
        mu = [jnp.mean(x, axis=0, keepdims=True) for x in ot]
        dev = [x - m for x, m in zip(ot, mu)]
        var = [jnp.mean(jnp.square(d), axis=0, keepdims=True) for d in dev]
        on = [(d * lax.rsqrt(s + GN_EPS)).T for d, s in zip(dev, var)]
        for jj in range(0, nh, 2):
            i, hd = jj // N_HEADS, jj % N_HEADS
            lp = slice(hd * HEAD_SIZE, (hd + 2) * HEAD_SIZE)
            on_pair = jnp.concatenate([on[jj], on[jj + 1]], axis=1)
            y_ref[i, rows, lp] = ((on_pair * gain[:, lp] + bias[:, lp]) * g_ref[i, rows, lp]
                                  + bg_ref[i, rows, lp]).astype(BF16)

    for jj in range(nh):
        s_scr[jj // N_HEADS, jj % N_HEADS] = st[jj]

    @pl.when(pl.program_id(1) == pl.num_programs(1) - 1)
    def _():
        for jj in range(nh):
            hd = jj % N_HEADS
            sout_ref[jj // N_HEADS, hd] = st[jj][:, (hd % 2) * HEAD_SIZE:(hd % 2 + 1) * HEAD_SIZE]


def _wkv_call(seqs, s0, gain, bias, *, nb, c, nc):
    B, T, _ = seqs[0].shape
    tok = pl.BlockSpec((nb, nc * c, RWKV_DIM), lambda b, t: (b, t, 0))
    st = pl.BlockSpec((nb, N_HEADS, HEAD_SIZE, HEAD_SIZE), lambda b, t: (b, 0, 0, 0))
    vec = pl.BlockSpec((1, RWKV_DIM), lambda b, t: (0, 0))
    return pl.pallas_call(
        functools.partial(_wkv_kernel, nb=nb, c=c, nc=nc),
        grid=(B // nb, T // (nc * c)),
        in_specs=[tok] * 8 + [st, vec, vec],
        out_specs=[tok, st],
        out_shape=[jax.ShapeDtypeStruct((B, T, RWKV_DIM), BF16),
                   jax.ShapeDtypeStruct((B, N_HEADS, HEAD_SIZE, HEAD_SIZE), F32)],
        scratch_shapes=[pltpu.VMEM((nb, N_HEADS, HEAD_SIZE, 2 * HEAD_SIZE), F32)],
        compiler_params=pltpu.CompilerParams(
            dimension_semantics=("arbitrary", "arbitrary"), vmem_limit_bytes=VMEM_LIMIT),
        name="wkv",
    )(*seqs, s0, gain, bias)


def _out_stream(rows, x_ref, yc_ref, yr_ref, wout_ref, gpost_ref, gfpre_ref, gfpost_ref,
                wff1_ref, wff2_ref, o_ref):
    wout = wout_ref[...]
    mix = (_dg(yc_ref[rows, :].astype(BF16), wout[:CONV_DIM])
           + _dg(yr_ref[rows, :].astype(BF16), wout[CONV_DIM:]))
    yield
    x1 = x_ref[rows, :] + _rmsnorm(mix, gpost_ref[...])
    h2 = _rmsnorm(x1, gfpre_ref[...]).astype(BF16)
    yield
    f1 = _dg(h2, wff1_ref[...])
    yield
    f1 = jnp.square(jnp.maximum(f1, 0.0)).astype(BF16)
    yield
    f2 = _dg(f1, wff2_ref[...])
    yield
    o_ref[rows, :] = x1 + _rmsnorm(f2, gfpost_ref[...])


def _out_kernel(*refs, tm, ns):
    per = tm // ns
    _interleave([_out_stream(slice(i * per, (i + 1) * per), *refs) for i in range(ns)])


def _out_call(x, yc, yr, params, *, tm, ns):
    n = x.shape[0]
    tok = lambda w: pl.BlockSpec((tm, w), lambda i: (i, 0))
    full = lambda arr: pl.BlockSpec(arr.shape, lambda i: (0,) * arr.ndim,
                                    pipeline_mode=pl.Buffered(1))
    return pl.pallas_call(
        functools.partial(_out_kernel, tm=tm, ns=ns),
        grid=(n // tm,),
        in_specs=[tok(D_MODEL), tok(CONV_DIM), tok(RWKV_DIM)] + [full(p) for p in params],
        out_specs=tok(D_MODEL),
        out_shape=jax.ShapeDtypeStruct((n, D_MODEL), F32),
        compiler_params=pltpu.CompilerParams(
            dimension_semantics=("arbitrary",), vmem_limit_bytes=VMEM_LIMIT),
        name="outffn",
    )(x, yc, yr, *params)


def _tiles(B, T):
    if T >= 256:
        return 2, 256, 4, 2, 64, 4, 512, 2
    return B, T, 1, B, T, 1, B * T, 1


def _layer(x, conv_buf, shift_prev, wkv_state, proj_params, gn, norms, out_w):
    B, T, _ = x.shape
    pnb, ptt, pns, wnb, wc, wnc, tm, ons = _tiles(B, T)
    cast = [w for w in out_w if w.dtype != BF16]
    outs = _proj_call(x, conv_buf, shift_prev, proj_params, cast,
                      nb=pnb, tt=ptt, ns=pns)
    yconv, seqs, new_conv, new_shift = outs[0], outs[1:9], outs[9], outs[10]
    if cast:
        out_w = tuple(outs[11:])
    yr, new_wkv = _wkv_call(seqs, wkv_state, *gn, nb=wnb, c=wc, nc=wnc)
    out_params = (out_w[0], *norms, out_w[1], out_w[2])
    y = _out_call(x.reshape(B * T, D_MODEL), yconv.reshape(B * T, CONV_DIM),
                  yr.reshape(B * T, RWKV_DIM), out_params, tm=tm, ns=ons)
    return (y.reshape(B, T, D_MODEL), new_conv, new_shift, new_wkv,
            out_w)


def kernel(x_prompt, x_sample, state_conv, state_shift, state_wkv, norm_mix_pre, norm_mix_post,
           norm_ffn_pre, norm_ffn_post, w_in, conv_w, shift_mu, w_decay2, decay_w0, w_a2, a0,
           w_g2, k_k, k_a, r_k, gn_gain, gn_bias, w_out, w_ff1, w_ff2):
    depth = w_in.shape[0]
    Bp = x_prompt.shape[0]
    xp, xs = x_prompt, x_sample
    row = lambda t: t.reshape(1, -1).astype(F32)
    res = [[] for _ in range(6)]
    for l in range(depth):
        proj_params = (
            row(norm_mix_pre[l]), w_in[l].astype(BF16), conv_w[l], row(shift_mu[l]),
            w_decay2[l], row(decay_w0[l]), w_a2[l], row(a0[l]), w_g2[l],
            row(k_k[l]), row(k_a[l]), row(r_k[l]))
        gn = (row(gn_gain[l]), row(gn_bias[l]))
        norms = (row(norm_mix_post[l]), row(norm_ffn_pre[l]), row(norm_ffn_post[l]))
        xp, c_p, s_p, w_p, out_w = _layer(
            xp, jnp.zeros((Bp, CONV_K - 1, CONV_DIM), F32), jnp.zeros((Bp, RWKV_PROJ), F32),
            jnp.zeros((Bp, N_HEADS, HEAD_SIZE, HEAD_SIZE), F32), proj_params, gn, norms,
            (w_out[l], w_ff1[l], w_ff2[l]))
        xs, c_s, s_s, w_s, _ = _layer(xs, state_conv[l], state_shift[l], state_wkv[l],
                                      proj_params, gn, norms, out_w)
        for lst, val in zip(res, (c_p, s_p, w_p, c_s, s_s, w_s)):
            lst.append(val)
    return (xp, xs) + tuple(jnp.stack(r) for r in res)
```
